```python
import jax
import jax.numpy as jnp
from jax import lax
import numpy as np

D_MODEL = 1024
BATCH = 8
SEQ = 2048
DEPTH = 4
DEC_BATCH = 128
DEC_SEQ = 8
PAST_LEN = 2048
PAGE_SIZE = 128

D_CONV = 256
CONV_W = 3
D_POOL = 256
POOL_WINDOWS = (2, 4, 8, 16)
N_POOL_GROUPS = 4
POOL_GROUP = D_POOL // N_POOL_GROUPS
POOL_HIST = 15
N_HEADS = 8
N_KV = 2
GROUP = N_HEADS // N_KV
HEAD_DIM = 64
D_ATTN = N_HEADS * HEAD_DIM
D_KV = N_KV * HEAD_DIM
CMP_BLOCK = 32
SEL_BLOCK = 64
TOP_K = 8
WINDOW = 512
Q_BLOCK = 128
N_BRANCH = 3
D_FF = 2816
FFN_CONV_W = 3
IN_SIZES = (D_CONV, D_CONV, D_CONV, D_POOL, D_ATTN, D_KV, D_KV, D_KV, D_KV, D_KV, D_KV, N_BRANCH * N_HEADS, N_BRANCH * D_MODEL)
D_IN = 3 * D_CONV + D_POOL + D_ATTN + 6 * D_KV + N_BRANCH * N_HEADS + N_BRANCH * D_MODEL
ALPHA = (2.0 * DEPTH) ** 0.25
BETA = (8.0 * DEPTH) ** -0.25
LN_EPS = 1e-5
ATTN_SCALE = HEAD_DIM ** -0.5
NEG_INF = -1e30

kernel_name = 'hybrid_conv_pool_nsa_decoder_step'


def _layer_norm(x, g, b):
    xf = x.astype(jnp.float32)
    mu = jnp.mean(xf, axis=-1, keepdims=True)
    var = jnp.mean(jnp.square(xf - mu), axis=-1, keepdims=True)
    return ((xf - mu) * lax.rsqrt(var + LN_EPS) * g + b).astype(x.dtype)


def _masked_softmax(s, mask, dtype):
    s = jnp.where(mask, s.astype(jnp.float32), NEG_INF)
    p = jax.nn.softmax(s, axis=-1)
    return jnp.where(mask, p, 0.0).astype(dtype)


def _alibi_slopes():
    h = jnp.arange(1, N_HEADS + 1, dtype=jnp.float32)
    return jnp.exp2(-8.0 * h / N_HEADS).reshape(N_KV, GROUP)


def _causal_dwconv(ext, w, b):
    width = w.shape[0]
    t = ext.shape[1] - (width - 1)
    out = b
    for j in range(width):
        out = out + w[j] * ext[:, j:j + t]
    return out


def _pad_rows(a, n_total):
    return jnp.pad(a, ((0, 0), (0, n_total - a.shape[1]), (0, 0), (0, 0)))


def _gather_pages(pool, page_table):
    g = pool[page_table]
    return g.reshape(g.shape[0], g.shape[1] * g.shape[2], N_KV, HEAD_DIM)


def _short_conv_mixer(b_gate, c_gate, h, hist, conv_w, conv_b, w_out):
    ext = jnp.concatenate([hist, c_gate * h], axis=1)
    y = b_gate * _causal_dwconv(ext, conv_w, conv_b)
    return y @ w_out, ext[:, -(CONV_W - 1):]


def _pool_mixer(u, hist, pos0, w_pg, scale):
    b, t, _ = u.shape
    ext = jnp.concatenate([hist, u], axis=1)
    cs = jnp.pad(jnp.cumsum(ext.astype(jnp.float32), axis=1), ((0, 0), (1, 0), (0, 0)))
    pos = pos0 + jnp.arange(t)
    hi = cs[:, POOL_HIST + 1:POOL_HIST + 1 + t]
    uf = u.astype(jnp.float32)
    outs = []
    for g, win in enumerate(POOL_WINDOWS):
        c = slice(g * POOL_GROUP, (g + 1) * POOL_GROUP)
        lo = cs[:, POOL_HIST + 1 - win:POOL_HIST + 1 - win + t, c]
        cnt = jnp.minimum(win, pos + 1).astype(jnp.float32)[None, :, None]
        outs.append((hi[..., c] - lo) / cnt - uf[..., c])
    d = jnp.stack(outs, axis=2).astype(u.dtype)
    y = jnp.einsum('btgc,gce->btge', d, w_pg).reshape(b, t, D_MODEL) * scale
    return y, ext[:, -POOL_HIST:]


def _compress(k_all, w_cmp):
    b, l = k_all.shape[:2]
    blocks = k_all.reshape(b, l // CMP_BLOCK, CMP_BLOCK, N_KV, HEAD_DIM)
    return jnp.einsum('bnlhd,lde->bnhe', blocks, w_cmp)


def _selected_attention(qg, ks_all, vs_all, sel_idx, qpos, slopes):
    b, t = qg.shape[:2]
    n_sel = sel_idx.shape[-1]
    kb = ks_all.reshape(b, -1, SEL_BLOCK, N_KV, HEAD_DIM)
    vb = vs_all.reshape(b, -1, SEL_BLOCK, N_KV, HEAD_DIM)
    bi = jnp.arange(b)[:, None, None, None]
    hi = jnp.arange(N_KV)[None, None, :, None]
    offs = jnp.arange(SEL_BLOCK)

    def attend(args):
        q_blk, idx_blk, pos_blk = args
        nq = q_blk.shape[1]
        kg = kb[bi, idx_blk, :, hi, :].reshape(b, nq, N_KV, n_sel * SEL_BLOCK, HEAD_DIM)
        vg = vb[bi, idx_blk, :, hi, :].reshape(b, nq, N_KV, n_sel * SEL_BLOCK, HEAD_DIM)
        kpos = (idx_blk[..., None] * SEL_BLOCK + offs).reshape(b, nq, N_KV, n_sel * SEL_BLOCK)
        dist = pos_blk[None, :, None, None] - kpos
        s = jnp.einsum('bqhgd,bqhkd->bqhgk', q_blk, kg).astype(jnp.float32) * ATTN_SCALE
        s = s - slopes[None, None, :, :, None] * dist[:, :, :, None, :]
        p = _masked_softmax(s, (dist >= 0)[:, :, :, None, :], vg.dtype)
        return jnp.einsum('bqhgk,bqhkd->bqhgd', p, vg)

    if t > Q_BLOCK and t % Q_BLOCK == 0:
        nb = t // Q_BLOCK

        def split(a):
            return jnp.moveaxis(a.reshape(b, nb, Q_BLOCK, *a.shape[2:]), 1, 0)

        out = lax.map(attend, (split(qg), split(sel_idx), qpos.reshape(nb, Q_BLOCK)))
        return jnp.moveaxis(out, 0, 1).reshape(qg.shape)
    return attend((qg, sel_idx, qpos))


def _window_attention(qb, kb, vb, qpos, kpos, slopes):
    s = jnp.einsum('bnqhgd,bnkhd->bnhgqk', qb, kb).astype(jnp.float32) * ATTN_SCALE
    dist = qpos[:, :, None] - kpos[:, None, :]
    mask = (dist >= 0) & (dist <= WINDOW) & (kpos >= 0)[:, None, :]
    s = s - slopes[None, None, :, :, None, None] * dist[None, :, None, None]
    p = _masked_softmax(s, mask[None, :, None, None], vb.dtype)
    return jnp.einsum('bnhgqk,bnkhd->bnqhgd', p, vb)


def _nsa_mixer(q, kc, vc, ks, vs, kw, vw, branch_gates, pos0, kv_past, w_cmp_k, w_cmp_v, w_out, w_buf):
    b, t = q.shape[:2]
    qg = q.reshape(b, t, N_KV, GROUP, HEAD_DIM)
    slopes = _alibi_slopes()
    qpos = pos0 + jnp.arange(t)
    if kv_past is None:
        kc_all, vc_all, ks_all, vs_all = kc, vc, ks, vs
    else:
        kc_all = jnp.concatenate([kv_past['kc'], kc], axis=1)
        vc_all = jnp.concatenate([kv_past['vc'], vc], axis=1)
        ks_all = jnp.concatenate([kv_past['ks'], ks], axis=1)
        vs_all = jnp.concatenate([kv_past['vs'], vs], axis=1)
    n_sb = -(-kc_all.shape[1] // SEL_BLOCK)
    padded = n_sb * SEL_BLOCK
    kc_all, vc_all, ks_all, vs_all = [_pad_rows(a, padded) for a in (kc_all, vc_all, ks_all, vs_all)]

    kcmp = _compress(kc_all, w_cmp_k)
    vcmp = _compress(vc_all, w_cmp_v)
    blk_end = (jnp.arange(kcmp.shape[1]) + 1) * CMP_BLOCK - 1
    dist_c = qpos[:, None] - blk_end[None, :]
    s_c = jnp.einsum('bthgd,bnhd->bhgtn', qg, kcmp).astype(jnp.float32) * ATTN_SCALE
    s_c = s_c - slopes[None, :, :, None, None] * dist_c
    p_c = _masked_softmax(s_c, dist_c >= 0, jnp.float32)
    o_c = jnp.einsum('bhgtn,bnhd->bthgd', p_c.astype(vcmp.dtype), vcmp)

    imp = p_c.sum(axis=2).reshape(b, N_KV, t, n_sb, SEL_BLOCK // CMP_BLOCK).sum(axis=-1)
    cur = (qpos // SEL_BLOCK)[:, None]
    blk = jnp.arange(n_sb)[None, :]
    score = jnp.where(blk == cur, 2.0 * GROUP, jnp.where(blk < cur, imp, -1.0))
    _, sel_idx = lax.top_k(score, min(TOP_K, n_sb))
    o_s = _selected_attention(qg, ks_all, vs_all, jnp.transpose(sel_idx, (0, 2, 1, 3)), qpos, slopes)

    if kv_past is None:
        nb = t // Q_BLOCK
        span = jnp.arange(nb)[:, None] * Q_BLOCK + jnp.arange(Q_BLOCK + WINDOW)[None, :]
        pad_cfg = ((0, 0), (WINDOW, 0), (0, 0), (0, 0))
        kw_pad = jnp.pad(kw, pad_cfg)
        vw_pad = jnp.pad(vw, pad_cfg)
        o_w = _window_attention(qg.reshape(b, nb, Q_BLOCK, N_KV, GROUP, HEAD_DIM), kw_pad[:, span], vw_pad[:, span], qpos.reshape(nb, Q_BLOCK), span - WINDOW, slopes)
        kw_hist = jnp.zeros((b, w_buf, N_KV, HEAD_DIM), kw.dtype)
        vw_hist = jnp.zeros((b, w_buf, N_KV, HEAD_DIM), vw.dtype)
        kw_ext = jnp.concatenate([kw_hist, kw], axis=1)
        vw_ext = jnp.concatenate([vw_hist, vw], axis=1)
    else:
        kw_ext = jnp.concatenate([kv_past['kw'], kw], axis=1)
        vw_ext = jnp.concatenate([kv_past['vw'], vw], axis=1)
        kpos = pos0 - w_buf + jnp.arange(w_buf + t)
        o_w = _window_attention(qg[:, None], kw_ext[:, None], vw_ext[:, None], qpos[None], kpos[None], slopes)
    o_w = o_w.reshape(b, t, N_KV, GROUP, HEAD_DIM)

    g = jax.nn.sigmoid(branch_gates).reshape(b, t, N_BRANCH, N_KV, GROUP, 1)
    o = g[:, :, 0] * o_c + g[:, :, 1] * o_s + g[:, :, 2] * o_w
    return o.reshape(b, t, D_ATTN) @ w_out, kw_ext[:, -w_buf:], vw_ext[:, -w_buf:]


def _conv_ffn(x, hist, w_up, conv_w, conv_b, w_down):
    z = x @ w_up
    u, g = z[..., :D_FF], z[..., D_FF:]
    ext = jnp.concatenate([hist, u], axis=1)
    a = jax.nn.gelu(_causal_dwconv(ext, conv_w, conv_b))
    return (a * g) @ w_down, ext[:, -(FFN_CONV_W - 1):]


def _trunk_layer(x, pos0, hist, kv_past, prm, w_buf):
    b, t, _ = x.shape
    z = x @ prm['w_in']
    parts, o = [], 0
    for n in IN_SIZES:
        parts.append(z[..., o:o + n])
        o += n
    a_b, a_c, a_h, pool_u, q, kc, vc, ks, vs, kw, vw, nsa_g, merge_g = parts

    def kv(a):
        return a.reshape(b, t, N_KV, HEAD_DIM)

    y_a, conv_st = _short_conv_mixer(a_b, a_c, a_h, hist['conv'], prm['conv_w'], prm['conv_b'], prm['w_out_conv'])
    y_b, pool_st = _pool_mixer(pool_u, hist['pool'], pos0, prm['pool_w'], prm['pool_scale'])
    y_c, kw_st, vw_st = _nsa_mixer(q, kv(kc), kv(vc), kv(ks), kv(vs), kv(kw), kv(vw), nsa_g, pos0, kv_past, prm['w_cmp_k'], prm['w_cmp_v'], prm['w_out_attn'], w_buf)
    gm = jax.nn.sigmoid(merge_g).reshape(b, t, N_BRANCH, D_MODEL)
    mixed = gm[:, :, 0] * y_a + gm[:, :, 1] * y_b + gm[:, :, 2] * y_c
    h = _layer_norm(ALPHA * x + mixed @ prm['w_mix_out'], prm['ln1_g'], prm['ln1_b'])
    f, ffn_st = _conv_ffn(h, hist['ffn'], prm['w_ffn_up'], prm['ffn_conv_w'], prm['ffn_conv_b'], prm['w_ffn_down'])
    out = _layer_norm(ALPHA * h + f, prm['ln2_g'], prm['ln2_b'])
    return out, (kv(kc), kv(vc), kv(ks), kv(vs), kw_st, vw_st, conv_st, pool_st, ffn_st)


def setup_inputs(seed: int = 0) -> dict:
    key = jax.random.key(seed)
    keys = jax.random.split(key, 40)
    counter = iter(range(40))

    def nrm(shape, scale=1.0):
        return jax.random.normal(keys[next(counter)], shape, jnp.float32) * scale

    n_pages = PAST_LEN // PAGE_SIZE
    n_phys = (5 * DEC_BATCH * n_pages + 3) // 4
    w_buf = min(WINDOW, PAST_LEN)
    perm = jax.random.permutation(keys[next(counter)], n_phys)
    page_table = perm[:DEC_BATCH * n_pages].reshape(DEC_BATCH, n_pages).astype(jnp.int32)
    paged = (DEPTH, n_phys, PAGE_SIZE, N_KV, HEAD_DIM)
    win = (DEPTH, DEC_BATCH, w_buf, N_KV, HEAD_DIM)
    return {
        'x_prompt': nrm((BATCH, SEQ, D_MODEL)),
        'x_sample': nrm((DEC_BATCH, DEC_SEQ, D_MODEL)),
        'cache_k_cmp': nrm(paged),
        'cache_v_cmp': nrm(paged),
        'cache_k_sel': nrm(paged),
        'cache_v_sel': nrm(paged),
        'state_k_win': nrm(win),
        'state_v_win': nrm(win),
        'state_conv': nrm((DEPTH, DEC_BATCH, CONV_W - 1, D_CONV)),
        'state_pool': nrm((DEPTH, DEC_BATCH, POOL_HIST, D_POOL)),
        'state_ffn_conv': nrm((DEPTH, DEC_BATCH, FFN_CONV_W - 1, D_FF)),
        'page_table': page_table,
        'w_in': nrm((DEPTH, D_MODEL, D_IN), D_MODEL ** -0.5),
        'conv_w': nrm((DEPTH, CONV_W, D_CONV), CONV_W ** -0.5),
        'conv_b': nrm((DEPTH, D_CONV), 0.01),
        'w_out_conv': nrm((DEPTH, D_CONV, D_MODEL), BETA * D_CONV ** -0.5),
        'pool_w': nrm((DEPTH, N_POOL_GROUPS, POOL_GROUP, D_MODEL // N_POOL_GROUPS), BETA * POOL_GROUP ** -0.5),
        'pool_scale': 1.0 + nrm((DEPTH, D_MODEL), 0.01),
        'w_cmp_k': nrm((DEPTH, CMP_BLOCK, HEAD_DIM, HEAD_DIM), (CMP_BLOCK * HEAD_DIM) ** -0.5),
        'w_cmp_v': nrm((DEPTH, CMP_BLOCK, HEAD_DIM, HEAD_DIM), (CMP_BLOCK * HEAD_DIM) ** -0.5),
        'w_out_attn': nrm((DEPTH, D_ATTN, D_MODEL), BETA * D_ATTN ** -0.5),
        'w_mix_out': nrm((DEPTH, D_MODEL, D_MODEL), BETA * D_MODEL ** -0.5),
        'ln1_g': 1.0 + nrm((DEPTH, D_MODEL), 0.01),
        'ln1_b': nrm((DEPTH, D_MODEL), 0.01),
        'w_ffn_up': nrm((DEPTH, D_MODEL, 2 * D_FF), D_MODEL ** -0.5),
        'ffn_conv_w': nrm((DEPTH, FFN_CONV_W, D_FF), FFN_CONV_W ** -0.5),
        'ffn_conv_b': nrm((DEPTH, D_FF), 0.01),
        'w_ffn_down': nrm((DEPTH, D_FF, D_MODEL), BETA * D_FF ** -0.5),
        'ln2_g': 1.0 + nrm((DEPTH, D_MODEL), 0.01),
        'ln2_b': nrm((DEPTH, D_MODEL), 0.01),
    }


def reference(x_prompt, x_sample, cache_k_cmp, cache_v_cmp, cache_k_sel, cache_v_sel, state_k_win, state_v_win, state_conv, state_pool, state_ffn_conv, page_table, w_in, conv_w, conv_b, w_out_conv, pool_w, pool_scale, w_cmp_k, w_cmp_v, w_out_attn, w_mix_out, ln1_g, ln1_b, w_ffn_up, ffn_conv_w, ffn_conv_b, w_ffn_down, ln2_g, ln2_b):
    w_buf = state_k_win.shape[2]
    past_len = page_table.shape[1] * cache_k_cmp.shape[2]
    bp = x_prompt.shape[0]
    y_p, y_s = x_prompt, x_sample
    st_p, st_s = [], []
    for l in range(DEPTH):
        prm = {'w_in': w_in[l], 'conv_w': conv_w[l], 'conv_b': conv_b[l], 'w_out_conv': w_out_conv[l],
               'pool_w': pool_w[l], 'pool_scale': pool_scale[l], 'w_cmp_k': w_cmp_k[l], 'w_cmp_v': w_cmp_v[l],
               'w_out_attn': w_out_attn[l], 'w_mix_out': w_mix_out[l], 'ln1_g': ln1_g[l], 'ln1_b': ln1_b[l],
               'w_ffn_up': w_ffn_up[l], 'ffn_conv_w': ffn_conv_w[l], 'ffn_conv_b': ffn_conv_b[l],
               'w_ffn_down': w_ffn_down[l], 'ln2_g': ln2_g[l], 'ln2_b': ln2_b[l]}
        hist_p = {'conv': jnp.zeros((bp, CONV_W - 1, D_CONV), x_prompt.dtype),
                  'pool': jnp.zeros((bp, POOL_HIST, D_POOL), x_prompt.dtype),
                  'ffn': jnp.zeros((bp, FFN_CONV_W - 1, D_FF), x_prompt.dtype)}
        y_p, s_p = _trunk_layer(y_p, 0, hist_p, None, prm, w_buf)
        st_p.append(s_p)
        hist_s = {'conv': state_conv[l], 'pool': state_pool[l], 'ffn': state_ffn_conv[l]}
        kv_s = {'kc': _gather_pages(cache_k_cmp[l], page_table), 'vc': _gather_pages(cache_v_cmp[l], page_table),
                'ks': _gather_pages(cache_k_sel[l], page_table), 'vs': _gather_pages(cache_v_sel[l], page_table),
                'kw': state_k_win[l], 'vw': state_v_win[l]}
        y_s, s_s = _trunk_layer(y_s, past_len, hist_s, kv_s, prm, w_buf)
        st_s.append(s_s)
    kc_p, vc_p, ks_p, vs_p, kw_p, vw_p, conv_p, pool_p, ffn_p = [jnp.stack(a) for a in zip(*st_p)]
    kc_s, vc_s, ks_s, vs_s, kw_s, vw_s, conv_s, pool_s, ffn_s = [jnp.stack(a) for a in zip(*st_s)]
    return (y_p, y_s, kc_p, kc_s, vc_p, vc_s, ks_p, ks_s, vs_p, vs_s, kw_p, kw_s, vw_p, vw_s, conv_p, conv_s, pool_p, pool_s, ffn_p, ffn_s)
```

```python
import functools

import jax
import jax.numpy as jnp
from jax import lax
from jax.experimental import pallas as pl
from jax.experimental.pallas import tpu as pltpu

F32 = jnp.float32
BF16 = jnp.bfloat16

D_MODEL = 1024
DEPTH = 4
D_CONV = 256
CONV_W = 3
D_POOL = 256
POOL_WINDOWS = (2, 4, 8, 16)
POOL_GROUP = D_POOL // len(POOL_WINDOWS)
POOL_HIST = 15
N_HEADS = 8
N_KV = 2
GROUP = N_HEADS // N_KV
HEAD_DIM = 64
D_ATTN = N_HEADS * HEAD_DIM
D_KV = N_KV * HEAD_DIM
CMP_BLOCK = 32
SEL_BLOCK = 64
TOP_K = 8
WINDOW = 512
N_BRANCH = 3
D_FF = 2816
ALPHA = (2.0 * DEPTH) ** 0.25
LN_EPS = 1e-5
ATTN_SCALE = HEAD_DIM ** -0.5
NEG_INF = -1e30

LANES = 128
SUBLANES = 8
VMEM_LIMIT = 56 * 1024 * 1024

C_AB, C_AC, C_AH, C_U, C_Q = 0, D_CONV, 2 * D_CONV, 3 * D_CONV, 3 * D_CONV + D_POOL
C_KV = C_Q + D_ATTN
C_NSA = C_KV + 6 * D_KV
C_G01 = C_NSA + LANES
W_FRONT = C_G01 + 2 * D_MODEL
IN_MAIN = C_NSA
IN_GATES = IN_MAIN + N_BRANCH * N_HEADS

FRONT_TILE = 256
ATTN_TQ = 128
ATTN_TK = 256
CMP_ROWS = 8192
FF_CHUNKS = 2


def _mm(a, b):
    return jnp.dot(a.astype(BF16), b.astype(BF16), preferred_element_type=F32)


def _mm_nt(a, b):
    return lax.dot_general(a.astype(BF16), b.astype(BF16), (((1,), (1,)), ((), ())),
                           preferred_element_type=F32)


def _layer_norm(x, g, b):
    mu = jnp.mean(x, axis=-1, keepdims=True)
    var = jnp.mean(jnp.square(x - mu), axis=-1, keepdims=True)
    return (x - mu) * lax.rsqrt(var + LN_EPS) * g + b


def _iota(shape, axis):
    return lax.broadcasted_iota(jnp.int32, shape, axis)


def _pool_delta(u, shifted, pos):
    nd = u.ndim
    acc = u
    sums = {}
    for j in range(1, POOL_WINDOWS[-1]):
        acc = acc + shifted(j)
        if j + 1 in POOL_WINDOWS:
            sums[j + 1] = acc
    grp = _iota((1,) * (nd - 1) + (D_POOL,), nd - 1) // POOL_GROUP
    s = sums[POOL_WINDOWS[-1]]
    win = jnp.full(grp.shape, POOL_WINDOWS[-1], jnp.int32)
    for g in range(len(POOL_WINDOWS) - 2, -1, -1):
        s = jnp.where(grp == g, sums[POOL_WINDOWS[g]], s)
        win = jnp.where(grp == g, POOL_WINDOWS[g], win)
    cnt = jnp.minimum(win, pos + 1).astype(F32)
    return s / cnt - u


def _front_tail(z, conv, d, woc_ref, pbd_ref, psc_ref, outs):
    pab_ref, q_ref, kv_refs, gate_ref = outs
    y_a = _mm(z[:, C_AB:C_AB + D_CONV] * conv, woc_ref[...])
    y_b = _mm(d, pbd_ref[...]) * psc_ref[...]
    g0 = jax.nn.sigmoid(z[:, C_G01:C_G01 + D_MODEL])
    g1 = jax.nn.sigmoid(z[:, C_G01 + D_MODEL:C_G01 + 2 * D_MODEL])
    pab_ref[...] = (g0 * y_a + g1 * y_b).reshape(pab_ref.shape)
    q_ref[...] = z[:, C_Q:C_Q + D_ATTN].reshape(q_ref.shape)
    for i, r in enumerate(kv_refs):
        r[...] = z[:, C_KV + i * D_KV:C_KV + (i + 1) * D_KV].reshape(r.shape)
    gate_ref[...] = jax.nn.sigmoid(z[:, C_NSA:C_NSA + LANES]).reshape(gate_ref.shape)


def _front_prompt_body(x_ref, w_ref, cw_ref, cb_ref, woc_ref, pbd_ref, psc_ref,
                       pab_ref, q_ref, kc_ref, vc_ref, ks_ref, vs_ref, kw_ref, vw_ref, gate_ref,
                       cst_ref, pst_ref, extc_ref, extu_ref, *, tt):
    j = pl.program_id(1)

    @pl.when(j == 0)
    def _():
        extc_ref[0:8, :] = jnp.zeros((8, D_CONV), F32)
        extu_ref[0:16, :] = jnp.zeros((16, D_POOL), F32)

    z = _mm(x_ref[0], w_ref[...])
    ch = z[:, C_AC:C_AC + D_CONV] * z[:, C_AH:C_AH + D_CONV]
    extc_ref[8:8 + tt, :] = ch
    conv = cb_ref[...] + cw_ref[0:1, :] * extc_ref[6:6 + tt, :]
    conv = conv + cw_ref[1:2, :] * extc_ref[7:7 + tt, :]
    conv = conv + cw_ref[2:3, :] * ch
    u = z[:, C_U:C_U + D_POOL]
    extu_ref[16:16 + tt, :] = u
    pos = j * tt + _iota((tt, 1), 0)
    d = _pool_delta(u, lambda s: extu_ref[16 - s:16 - s + tt, :], pos)
    _front_tail(z, conv, d, woc_ref, pbd_ref, psc_ref,
                (pab_ref, q_ref, (kc_ref, vc_ref, ks_ref, vs_ref, kw_ref, vw_ref), gate_ref))
    last_c = extc_ref[tt:tt + 8, :]
    last_u = extu_ref[tt:tt + 16, :]
    cst_ref[0] = last_c
    pst_ref[0] = last_u
    extc_ref[0:8, :] = last_c
    extu_ref[0:16, :] = last_u


def _front_sample_body(x_ref, hc_ref, hu_ref, w_ref, cw_ref, cb_ref, woc_ref, pbd_ref, psc_ref,
                       pab_ref, q_ref, kc_ref, vc_ref, ks_ref, vs_ref, kw_ref, vw_ref, gate_ref,
                       cst_ref, pst_ref, extc_ref, extu_ref, *, ns, t, pos0):
    z = _mm(x_ref[...], w_ref[...])
    ch = z[:, C_AC:C_AC + D_CONV] * z[:, C_AH:C_AH + D_CONV]
    extc_ref[:, 6:8, :] = hc_ref[...]
    extc_ref[:, 8:8 + t, :] = ch.reshape(ns, t, D_CONV)
    cw = cw_ref[...]
    conv = cb_ref[...] + cw[0:1] * extc_ref[:, 6:6 + t, :]
    conv = conv + cw[1:2] * extc_ref[:, 7:7 + t, :]
    conv = conv + cw[2:3] * extc_ref[:, 8:8 + t, :]
    u = z[:, C_U:C_U + D_POOL]
    extu_ref[:, 1:16, :] = hu_ref[...]
    extu_ref[:, 16:16 + t, :] = u.reshape(ns, t, D_POOL)
    pos = pos0 + _iota((1, t, 1), 1)
    d = _pool_delta(extu_ref[:, 16:16 + t, :], lambda s: extu_ref[:, 16 - s:16 - s + t, :], pos)
    _front_tail(z, conv.reshape(ns * t, D_CONV), d.reshape(ns * t, D_POOL), woc_ref, pbd_ref, psc_ref,
                (pab_ref, q_ref, (kc_ref, vc_ref, ks_ref, vs_ref, kw_ref, vw_ref), gate_ref))
    cst_ref[...] = extc_ref[:, t:t + 8, :]
    pst_ref[...] = extu_ref[:, t:t + 16, :]


def _full(shape):
    n = len(shape)
    return pl.BlockSpec(shape, lambda *_: (0,) * n, pipeline_mode=pl.Buffered(1))


def _front_prompt(x, wts):
    b, t, _ = x.shape
    tt = min(FRONT_TILE, t)
    tok = lambda n: pl.BlockSpec((1, tt, n), lambda i, j: (i, j, 0))
    st = lambda r, n: pl.BlockSpec((1, r, n), lambda i, j: (i, 0, 0))
    out_shape = ([jax.ShapeDtypeStruct((b, t, D_MODEL), F32), jax.ShapeDtypeStruct((b, t, D_ATTN), F32)]
                 + [jax.ShapeDtypeStruct((b, t, D_KV), F32)] * 6
                 + [jax.ShapeDtypeStruct((b, t, LANES), F32),
                    jax.ShapeDtypeStruct((b, 8, D_CONV), F32), jax.ShapeDtypeStruct((b, 16, D_POOL), F32)])
    out_specs = ([tok(D_MODEL), tok(D_ATTN)] + [tok(D_KV)] * 6 + [tok(LANES), st(8, D_CONV), st(16, D_POOL)])
    return pl.pallas_call(
        functools.partial(_front_prompt_body, tt=tt),
        grid=(b, t // tt),
        in_specs=[tok(D_MODEL)] + [_full(w.shape) for w in wts],
        out_specs=out_specs,
        out_shape=out_shape,
        scratch_shapes=[pltpu.VMEM((8 + tt, D_CONV), F32), pltpu.VMEM((16 + tt, D_POOL), F32)],
        compiler_params=pltpu.CompilerParams(dimension_semantics=("arbitrary", "arbitrary"),
                                             vmem_limit_bytes=VMEM_LIMIT),
        name="front_prompt",
    )(x, *wts)


def _front_sample(x, hist_c, hist_u, wts, pos0):
    nb, t, _ = x.shape
    ns = min(FRONT_TILE // t, nb)
    rows = ns * t
    xf = x.reshape(nb * t, D_MODEL)
    tok = lambda n: pl.BlockSpec((rows, n), lambda i: (i, 0))
    st = lambda r, n: pl.BlockSpec((ns, r, n), lambda i: (i, 0, 0))
    out_shape = ([jax.ShapeDtypeStruct((nb * t, D_MODEL), F32), jax.ShapeDtypeStruct((nb * t, D_ATTN), F32)]
                 + [jax.ShapeDtypeStruct((nb * t, D_KV), F32)] * 6
                 + [jax.ShapeDtypeStruct((nb * t, LANES), F32),
                    jax.ShapeDtypeStruct((nb, 8, D_CONV), F32), jax.ShapeDtypeStruct((nb, 16, D_POOL), F32)])
    out_specs = ([tok(D_MODEL), tok(D_ATTN)] + [tok(D_KV)] * 6 + [tok(LANES), st(8, D_CONV), st(16, D_POOL)])
    return pl.pallas_call(
        functools.partial(_front_sample_body, ns=ns, t=t, pos0=pos0),
        grid=(nb // ns,),
        in_specs=[tok(D_MODEL), st(CONV_W - 1, D_CONV), st(POOL_HIST, D_POOL)] + [_full(w.shape) for w in wts],
        out_specs=out_specs,
        out_shape=out_shape,
        scratch_shapes=[pltpu.VMEM((ns, 8 + t, D_CONV), F32), pltpu.VMEM((ns, 16 + t, D_POOL), F32)],
        compiler_params=pltpu.CompilerParams(dimension_semantics=("arbitrary",),
                                             vmem_limit_bytes=VMEM_LIMIT),
        name="front_sample",
    )(xf, hist_c, hist_u, *wts)


def _compress_body(k_ref, v_ref, wk_ref, wv_ref, ko_ref, vo_ref, *, nblk):
    acc_k = jnp.zeros((nblk, D_KV), F32)
    acc_v = jnp.zeros((nblk, D_KV), F32)
    for l in range(CMP_BLOCK):
        acc_k = acc_k + _mm(k_ref[pl.ds(l, nblk, stride=CMP_BLOCK), :], wk_ref[l])
        acc_v = acc_v + _mm(v_ref[pl.ds(l, nblk, stride=CMP_BLOCK), :], wv_ref[l])
    ko_ref[...] = acc_k
    vo_ref[...] = acc_v


def _compress(k, v, wk, wv):
    nl, rows, _ = k.shape
    chunk = next(c for c in range(min(CMP_ROWS, rows), 0, -CMP_BLOCK * SUBLANES) if rows % c == 0)
    nblk = chunk // CMP_BLOCK
    rspec = pl.BlockSpec((None, chunk, D_KV), lambda l, i: (l, i, 0))
    wspec = pl.BlockSpec((None, CMP_BLOCK, D_KV, D_KV), lambda l, i: (l, 0, 0, 0))
    ospec = pl.BlockSpec((None, nblk, D_KV), lambda l, i: (l, i, 0))
    return pl.pallas_call(
        functools.partial(_compress_body, nblk=nblk),
        grid=(nl, rows // chunk),
        in_specs=[rspec, rspec, wspec, wspec],
        out_specs=[ospec, ospec],
        out_shape=[jax.ShapeDtypeStruct((nl, rows // CMP_BLOCK, D_KV), F32)] * 2,
        compiler_params=pltpu.CompilerParams(dimension_semantics=("arbitrary", "arbitrary"),
                                             vmem_limit_bytes=VMEM_LIMIT),
        name="compress",
    )(k, v, wk, wv)


def _make_qbd(q, tq):
    lo = _iota((tq, LANES), 1) < HEAD_DIM
    blocks = {}
    for jv in range(D_ATTN // LANES):
        a = q[:, LANES * jv:LANES * (jv + 1)] * ATTN_SCALE
        r = pltpu.roll(a, HEAD_DIM, 1)
        for half in range(2):
            h = 2 * jv + half
            kv, g = h // GROUP, h % GROUP
            if kv == 0:
                blocks[(g, kv)] = jnp.where(lo, a if half == 0 else r, 0.0)
            else:
                blocks[(g, kv)] = jnp.where(lo, 0.0, a if half == 1 else r)
    return jnp.concatenate([blocks[(g, kv)] for g in range(GROUP) for kv in range(N_KV)], axis=0).astype(BF16)


def _extract_heads(o, tq):
    lo = _iota((tq, LANES), 1) < HEAD_DIM
    cols = []
    for jv in range(D_ATTN // LANES):
        parts = []
        for half in range(2):
            h = 2 * jv + half
            kv, g = h // GROUP, h % GROUP
            blk = o[(g * N_KV + kv) * tq:(g * N_KV + kv + 1) * tq]
            if (half == 1) != (kv == 1):
                blk = pltpu.roll(blk, HEAD_DIM, 1)
            parts.append(blk)
        cols.append(jnp.where(lo, parts[0], parts[1]))
    return jnp.concatenate(cols, axis=1)


def _gate_rows(gate, c, tq):
    return jnp.concatenate([gate[:, c * N_HEADS + kv * GROUP + g:c * N_HEADS + kv * GROUP + g + 1]
                            for g in range(GROUP) for kv in range(N_KV)], axis=0)


def _softmax_rows(s, valid):
    s = jnp.where(valid, s, NEG_INF)
    m = jnp.max(s, axis=1, keepdims=True)
    e = jnp.exp(s - m)
    p = e / jnp.sum(e, axis=1, keepdims=True)
    return jnp.where(valid, p, 0.0)


def _topk_select(score, blkf, k):
    sel = jnp.zeros(score.shape, F32)
    for _ in range(k):
        m = jnp.max(score, axis=1, keepdims=True)
        idx = jnp.min(jnp.where(score == m, blkf, 1e9), axis=1, keepdims=True)
        hit = blkf == idx
        sel = jnp.where(hit, 1.0, sel)
        score = jnp.where(hit, -3.0, score)
    return sel


def _select_blocks(imp, qpos2, blkf, n_valid_imp, k):
    cur = (qpos2 // SEL_BLOCK).astype(F32)
    imp = jnp.where(blkf < n_valid_imp, imp, 0.0)
    score = jnp.where(blkf == cur, 2.0 * GROUP, jnp.where(blkf < cur, imp, -1.0))
    score = jnp.where(blkf < 0.0, -3.0, score)
    return _topk_select(score, blkf, k)


def _sum_groups(p, rows2):
    out = p[0:rows2]
    for g in range(1, GROUP):
        out = out + p[g * rows2:(g + 1) * rows2]
    return out


def _attn_prompt_body(q_ref, gate_ref, kc_ref, vc_ref, ks_ref, vs_ref, kw_ref, vw_ref,
                      slope_ref, tcol_ref, blkf_ref, et_ref, o_ref, *, tq, tk, seq, top_k):
    q0 = pl.program_id(1) * tq
    rows = N_HEADS * tq
    rows2 = N_KV * tq
    qbd = _make_qbd(q_ref[0], tq)
    slope = slope_ref[...]
    qpos = q0 + tcol_ref[...]
    gate = gate_ref[0]

    ncb = seq // CMP_BLOCK
    pad = jnp.zeros((LANES - ncb, D_KV), F32)
    kc = jnp.concatenate([kc_ref[0], pad], axis=0)
    vc = jnp.concatenate([vc_ref[0], pad], axis=0)
    blk_end = (_iota((1, LANES), 1) + 1) * CMP_BLOCK - 1
    dist_c = qpos - blk_end
    s_c = _mm_nt(qbd, kc) - slope * dist_c.astype(F32)
    p_c = _softmax_rows(s_c, dist_c >= 0)
    o_acc = _gate_rows(gate, 0, tq) * _mm(p_c, vc)

    p4 = _sum_groups(p_c, rows2)
    imp = p4 + pltpu.roll(p4, LANES - 1, 1)
    blkf = blkf_ref[...]
    sel = _select_blocks(imp, qpos[0:rows2], blkf, float(seq // SEL_BLOCK), top_k)
    sel4 = jnp.concatenate([sel] * GROUP, axis=0).astype(BF16)

    def branch(k_ref, v_ref, j_lo, j_hi, mask_fn):
        def step(j, carry):
            m, l, acc = carry
            k0 = pl.multiple_of(j * tk, tk)
            dist = qpos - (k0 + _iota((1, tk), 1))
            s = _mm_nt(qbd, k_ref[0, pl.ds(k0, tk), :]) - slope * dist.astype(F32)
            s = jnp.where(mask_fn(dist, k0), s, NEG_INF)
            m_new = jnp.maximum(m, jnp.max(s, axis=1, keepdims=True))
            alpha = jnp.exp(m - m_new)
            p = jnp.exp(s - m_new)
            l = alpha * l + jnp.sum(p, axis=1, keepdims=True)
            acc = alpha * acc + _mm(p, v_ref[0, pl.ds(k0, tk), :])
            return m_new, l, acc

        init = (jnp.full((rows, 1), NEG_INF, F32), jnp.zeros((rows, 1), F32), jnp.zeros((rows, D_KV), F32))
        _, l, acc = lax.fori_loop(j_lo, j_hi, step, init)
        return acc / l

    j_hi = (q0 + tq - 1) // tk + 1

    def sel_mask(dist, k0):
        picked = _mm_nt(sel4, et_ref[pl.ds(k0, tk), :])
        return (picked > 0.5) & (dist >= 0)

    o_acc = o_acc + _gate_rows(gate, 1, tq) * branch(ks_ref, vs_ref, 0, j_hi, sel_mask)

    def win_mask(dist, k0):
        return (dist >= 0) & (dist <= WINDOW)

    j_lo = jnp.maximum(q0 - WINDOW, 0) // tk
    o_acc = o_acc + _gate_rows(gate, 2, tq) * branch(kw_ref, vw_ref, j_lo, j_hi, win_mask)
    o_ref[0] = _extract_heads(o_acc, tq)


def _row_consts(tq):
    r = jnp.arange(N_HEADS * tq)
    g, kv, t = r // (N_KV * tq), (r // tq) % N_KV, r % tq
    head = kv * GROUP + g
    slopes = jnp.exp2(-8.0 * (head + 1).astype(F32) / N_HEADS)
    return slopes[:, None], t.astype(jnp.int32)[:, None]


def _expand_matrix(blkf, n_keys):
    kb = (jnp.arange(n_keys) // SEL_BLOCK).astype(F32)
    return (kb[:, None] == blkf.reshape(1, -1)).astype(BF16)


def _attn_prompt(q, gate, kcmp, vcmp, ks, vs, kw, vw):
    b, t, _ = q.shape
    tq = min(ATTN_TQ, t)
    tk = min(ATTN_TK, t)
    n_sb = t // SEL_BLOCK
    slopes, tcol = _row_consts(tq)
    lane = jnp.arange(LANES)
    blkf = jnp.where((lane % 2 == 0) & (lane < 2 * n_sb), lane // 2, -1).astype(F32)[None, :]
    et = _expand_matrix(blkf, t)
    tile = lambda n: pl.BlockSpec((1, tq, n), lambda i, j: (i, j, 0))
    seq = lambda r, n: pl.BlockSpec((1, r, n), lambda i, j: (i, 0, 0))
    return pl.pallas_call(
        functools.partial(_attn_prompt_body, tq=tq, tk=tk, seq=t, top_k=min(TOP_K, n_sb)),
        grid=(b, t // tq),
        in_specs=[tile(D_ATTN), tile(LANES), seq(t // CMP_BLOCK, D_KV), seq(t // CMP_BLOCK, D_KV),
                  seq(t, D_KV), seq(t, D_KV), seq(t, D_KV), seq(t, D_KV),
                  _full(slopes.shape), _full(tcol.shape), _full(blkf.shape), _full(et.shape)],
        out_specs=tile(D_ATTN),
        out_shape=jax.ShapeDtypeStruct((b, t, D_ATTN), F32),
        compiler_params=pltpu.CompilerParams(dimension_semantics=("arbitrary", "arbitrary"),
                                             vmem_limit_bytes=VMEM_LIMIT),
        name="attn_prompt",
    )(q, gate, kcmp, vcmp, ks, vs, kw, vw, slopes, tcol, blkf, et)


def _attn_sample_body(pt_ref, q_ref, gate_ref, ksn_ref, vsn_ref, kwn_ref, vwn_ref, kwin_ref, vwin_ref,
                      kcp_ref, vcp_ref, cvalid_ref, cend_ref, slope_ref, tcol_ref, blkf_ref, et_ref, *rest,
                      t, n_pages, page, past, w_buf, top_k):
    ks_pages = rest[:n_pages]
    vs_pages = rest[n_pages:2 * n_pages]
    o_ref, kwo_ref, vwo_ref = rest[2 * n_pages:]
    b = pl.program_id(0)
    rows2 = N_KV * t
    qbd = _make_qbd(q_ref[0], t)
    slope = slope_ref[...]
    qpos = past + tcol_ref[...]
    gate = gate_ref[0]

    def pad_rows(x):
        if x.shape[0] == LANES:
            return x
        return jnp.concatenate([x, jnp.zeros((LANES - x.shape[0], D_KV), F32)], axis=0)

    tiles_k = [kcp_ref[pt_ref[b, p] // 2] for p in range(n_pages)]
    tiles_v = [vcp_ref[pt_ref[b, p] // 2] for p in range(n_pages)]
    kc = pad_rows(jnp.concatenate(tiles_k, axis=0))
    vc = pad_rows(jnp.concatenate(tiles_v, axis=0))
    dist_c = qpos - cend_ref[...]
    s_c = _mm_nt(qbd, kc) - slope * dist_c.astype(F32)
    p_c = _softmax_rows(s_c, (cvalid_ref[0] > 0.5) & (dist_c >= 0))
    o_acc = _gate_rows(gate, 0, t) * _mm(p_c, vc)

    p4 = _sum_groups(p_c, rows2)
    x1 = p4 + pltpu.roll(p4, LANES - 1, 1)
    imp = x1 + pltpu.roll(x1, LANES - 4, 1)
    blkf = blkf_ref[...]
    sel = _select_blocks(imp, qpos[0:rows2], blkf, float(past // SEL_BLOCK), top_k)
    sel4 = jnp.concatenate([sel] * GROUP, axis=0)

    ks_new = pad_rows(ksn_ref[0])
    vs_new = pad_rows(vsn_ref[0])
    s_s = jnp.concatenate([_mm_nt(qbd, r[...]) for r in ks_pages] + [_mm_nt(qbd, ks_new)], axis=1)
    n_keys = past + LANES
    dist_s = qpos - _iota((1, n_keys), 1)
    picked = _mm_nt(sel4, et_ref[...])
    p_s = _softmax_rows(s_s - slope * dist_s.astype(F32), (picked > 0.5) & (dist_s >= 0))
    o_s = _mm(p_s[:, past:], vs_new)
    for p in range(n_pages):
        o_s = o_s + _mm(p_s[:, p * page:(p + 1) * page], vs_pages[p][...])
    o_acc = o_acc + _gate_rows(gate, 1, t) * o_s

    kwin = kwin_ref[0]
    vwin = vwin_ref[0]
    kw_new = kwn_ref[0]
    vw_new = vwn_ref[0]
    s_w = jnp.concatenate([_mm_nt(qbd, kwin), _mm_nt(qbd, pad_rows(kw_new))], axis=1)
    dist_w = qpos - (past - w_buf + _iota((1, w_buf + LANES), 1))
    p_w = _softmax_rows(s_w - slope * dist_w.astype(F32), (dist_w >= 0) & (dist_w <= WINDOW))
    o_w = _mm(p_w[:, :w_buf], vwin) + _mm(p_w[:, w_buf:], pad_rows(vw_new))
    o_acc = o_acc + _gate_rows(gate, 2, t) * o_w
    o_ref[0] = _extract_heads(o_acc, t)

    kwo_ref[0, 0:w_buf - t, :] = kwin[t:w_buf]
    kwo_ref[0, w_buf - t:w_buf, :] = kw_new
    vwo_ref[0, 0:w_buf - t, :] = vwin[t:w_buf]
    vwo_ref[0, w_buf - t:w_buf, :] = vw_new


def _page_map(l, p, b, pt):
    return (l, pt[b, p], 0, 0)


def _attn_sample(l, page_table, q, gate, ks_new, vs_new, kw_new, vw_new, kwin, vwin, kcp, vcp,
                 cache_ks, cache_vs):
    nb, t, _ = q.shape
    n_pages = page_table.shape[1]
    page = cache_ks.shape[2]
    past = n_pages * page
    w_buf = kwin.shape[1]
    n_sb = -(-(past + t) // SEL_BLOCK)
    assert t == SUBLANES and n_pages * 8 <= LANES and page == LANES and past % SEL_BLOCK == 0
    slopes, tcol = _row_consts(t)
    lane = jnp.arange(LANES)
    pg, r = lane // 8, lane % 8
    in_range = pg < n_pages
    cend = jnp.where(in_range, (4 * pg + r % 4 + 1) * CMP_BLOCK - 1, 1 << 30).astype(jnp.int32)[None, :]
    par = page_table % 2
    par_l = jnp.take(par, jnp.minimum(pg, n_pages - 1), axis=1)
    cvalid = ((r[None, :] // 4 == par_l) & in_range[None, :]).astype(F32)[:, None, :]
    blk = jnp.where(in_range & (r % 2 == 0) & (r < 4), 2 * pg + r // 2, -1)
    blk = jnp.where(lane == 1, n_sb - 1, blk)
    blkf = blk.astype(F32)[None, :]
    et = _expand_matrix(blkf, past + LANES)
    tok = lambda n: pl.BlockSpec((1, t, n), lambda i, pt: (i, 0, 0))
    win = pl.BlockSpec((1, w_buf, D_KV), lambda i, pt: (i, 0, 0))
    res = pl.BlockSpec((None,) + kcp.shape[1:], lambda i, pt: (l, 0, 0, 0))
    const = lambda a: pl.BlockSpec(a.shape, lambda i, pt: (0,) * a.ndim)
    pages = [pl.BlockSpec((None, None, page, D_KV), functools.partial(_page_map, l, p)) for p in range(n_pages)]
    grid_spec = pltpu.PrefetchScalarGridSpec(
        num_scalar_prefetch=1,
        grid=(nb,),
        in_specs=[tok(D_ATTN), tok(LANES), tok(D_KV), tok(D_KV), tok(D_KV), tok(D_KV), win, win, res, res,
                  pl.BlockSpec((1, 1, LANES), lambda i, pt: (i, 0, 0)),
                  const(cend), const(slopes), const(tcol), const(blkf), const(et)] + pages + pages,
        out_specs=[tok(D_ATTN), win, win],
    )
    return pl.pallas_call(
        functools.partial(_attn_sample_body, t=t, n_pages=n_pages, page=page, past=past, w_buf=w_buf,
                          top_k=min(TOP_K, n_sb)),
        grid_spec=grid_spec,
        out_shape=[jax.ShapeDtypeStruct((nb, t, D_ATTN), F32),
                   jax.ShapeDtypeStruct((nb, w_buf, D_KV), F32), jax.ShapeDtypeStruct((nb, w_buf, D_KV), F32)],
        compiler_params=pltpu.CompilerParams(dimension_semantics=("arbitrary",), vmem_limit_bytes=VMEM_LIMIT),
        name="attn_sample",
    )(page_table, q, gate, ks_new, vs_new, kw_new, vw_new, kwin, vwin, kcp, vcp, cvalid,
      cend, slopes, tcol, blkf, et, *([cache_ks] * n_pages), *([cache_vs] * n_pages))


def _back_head(x, o, pab, wg2_ref, woa_ref, wmix_ref, ln1g_ref, ln1b_ref):
    y_c = _mm(o, woa_ref[...])
    g2 = jax.nn.sigmoid(_mm(x, wg2_ref[...]))
    mixed = pab + g2 * y_c
    return _layer_norm(ALPHA * x + _mm(mixed, wmix_ref[...]), ln1g_ref[...], ln1b_ref[...])


def _back_prompt_body(x_ref, o_ref, pab_ref, wg2_ref, woa_ref, wmix_ref, ln1g_ref, ln1b_ref,
                      wup_ref, fcw_ref, fcb_ref, wdn_ref, ln2g_ref, ln2b_ref,
                      y_ref, fst_ref, ext_ref, *, tt):
    j = pl.program_id(1)

    @pl.when(j == 0)
    def _():
        ext_ref[0:8, :] = jnp.zeros((8, D_FF), F32)

    h = _back_head(x_ref[0], o_ref[0], pab_ref[0], wg2_ref, woa_ref, wmix_ref, ln1g_ref, ln1b_ref)
    hb = h.astype(BF16)
    cf = D_FF // FF_CHUNKS
    f = jnp.zeros((tt, D_MODEL), F32)
    for c in range(FF_CHUNKS):
        cs = slice(c * cf, (c + 1) * cf)
        u = _mm(hb, wup_ref[:, c * cf:(c + 1) * cf])
        gt = _mm(hb, wup_ref[:, D_FF + c * cf:D_FF + (c + 1) * cf])
        ext_ref[8:8 + tt, cs] = u
        conv = fcb_ref[:, cs] + fcw_ref[0:1, cs] * ext_ref[6:6 + tt, cs]
        conv = conv + fcw_ref[1:2, cs] * ext_ref[7:7 + tt, cs]
        conv = conv + fcw_ref[2:3, cs] * u
        f = f + _mm(jax.nn.gelu(conv) * gt, wdn_ref[c * cf:(c + 1) * cf, :])
    y_ref[0] = _layer_norm(ALPHA * h + f, ln2g_ref[...], ln2b_ref[...])
    last = ext_ref[tt:tt + 8, :]
    fst_ref[0] = last
    ext_ref[0:8, :] = last


def _back_sample_body(x_ref, o_ref, pab_ref, hf_ref, wg2_ref, woa_ref, wmix_ref, ln1g_ref, ln1b_ref,
                      wup_ref, fcw_ref, fcb_ref, wdn_ref, ln2g_ref, ln2b_ref,
                      y_ref, fst_ref, ext_ref, *, ns, t):
    h = _back_head(x_ref[...], o_ref[...], pab_ref[...], wg2_ref, woa_ref, wmix_ref, ln1g_ref, ln1b_ref)
    hb = h.astype(BF16)
    cf = D_FF // FF_CHUNKS
    ext_ref[:, 6:8, :] = hf_ref[...]
    f = jnp.zeros((ns * t, D_MODEL), F32)
    for c in range(FF_CHUNKS):
        cs = slice(c * cf, (c + 1) * cf)
        u = _mm(hb, wup_ref[:, c * cf:(c + 1) * cf])
        gt = _mm(hb, wup_ref[:, D_FF + c * cf:D_FF + (c + 1) * cf])
        ext_ref[:, 8:8 + t, cs] = u.reshape(ns, t, cf)
        fcw = fcw_ref[:, cs]
        conv = fcb_ref[:, cs] + fcw[0:1] * ext_ref[:, 6:6 + t, cs]
        conv = conv + fcw[1:2] * ext_ref[:, 7:7 + t, cs]
        conv = conv + fcw[2:3] * ext_ref[:, 8:8 + t, cs]
        f = f + _mm(jax.nn.gelu(conv).reshape(ns * t, cf) * gt, wdn_ref[c * cf:(c + 1) * cf, :])
    y_ref[...] = _layer_norm(ALPHA * h + f, ln2g_ref[...], ln2b_ref[...])
    fst_ref[...] = ext_ref[:, t:t + 8, :]


def _back_prompt(x, o, pab, wts):
    b, t, _ = x.shape
    tt = min(FRONT_TILE, t)
    tok = lambda n: pl.BlockSpec((1, tt, n), lambda i, j: (i, j, 0))
    return pl.pallas_call(
        functools.partial(_back_prompt_body, tt=tt),
        grid=(b, t // tt),
        in_specs=[tok(D_MODEL), tok(D_ATTN), tok(D_MODEL)] + [_full(w.shape) for w in wts],
        out_specs=[tok(D_MODEL), pl.BlockSpec((1, 8, D_FF), lambda i, j: (i, 0, 0))],
        out_shape=[jax.ShapeDtypeStruct((b, t, D_MODEL), F32), jax.ShapeDtypeStruct((b, 8, D_FF), F32)],
        scratch_shapes=[pltpu.VMEM((8 + tt, D_FF), F32)],
        compiler_params=pltpu.CompilerParams(dimension_semantics=("arbitrary", "arbitrary"),
                                             vmem_limit_bytes=VMEM_LIMIT),
        name="back_prompt",
    )(x, o, pab, *wts)


def _back_sample(x, o, pab, hist_f, wts, t):
    n = x.shape[0]
    nb = n // t
    ns = min(FRONT_TILE // t, nb)
    rows = ns * t
    tok = lambda c: pl.BlockSpec((rows, c), lambda i: (i, 0))
    return pl.pallas_call(
        functools.partial(_back_sample_body, ns=ns, t=t),
        grid=(nb // ns,),
        in_specs=[tok(D_MODEL), tok(D_ATTN), tok(D_MODEL),
                  pl.BlockSpec((ns, CONV_W - 1, D_FF), lambda i: (i, 0, 0))] + [_full(w.shape) for w in wts],
        out_specs=[tok(D_MODEL), pl.BlockSpec((ns, 8, D_FF), lambda i: (i, 0, 0))],
        out_shape=[jax.ShapeDtypeStruct((n, D_MODEL), F32), jax.ShapeDtypeStruct((nb, 8, D_FF), F32)],
        scratch_shapes=[pltpu.VMEM((ns, 8 + t, D_FF), F32)],
        compiler_params=pltpu.CompilerParams(dimension_semantics=("arbitrary",), vmem_limit_bytes=VMEM_LIMIT),
        name="back_sample",
    )(x, o, pab, hist_f, *wts)


def _block_diag2(w):
    z = jnp.zeros_like(w)
    return jnp.concatenate([jnp.concatenate([w, z], axis=-1), jnp.concatenate([z, w], axis=-1)], axis=-2)


def kernel(x_prompt, x_sample, cache_k_cmp, cache_v_cmp, cache_k_sel, cache_v_sel, state_k_win, state_v_win, state_conv, state_pool, state_ffn_conv, page_table, w_in, conv_w, conv_b, w_out_conv, pool_w, pool_scale, w_cmp_k, w_cmp_v, w_out_attn, w_mix_out, ln1_g, ln1_b, w_ffn_up, ffn_conv_w, ffn_conv_b, w_ffn_down, ln2_g, ln2_b):
    depth = w_in.shape[0]
    bp, seq, _ = x_prompt.shape
    nb, t, _ = x_sample.shape
    n_phys, page = cache_k_cmp.shape[1], cache_k_cmp.shape[2]
    past = page_table.shape[1] * page
    w_buf = state_k_win.shape[2]

    w_front = jnp.concatenate(
        [w_in[:, :, :IN_GATES], jnp.zeros((depth, D_MODEL, C_G01 - IN_GATES), F32),
         w_in[:, :, IN_GATES:IN_GATES + 2 * D_MODEL]], axis=2).astype(BF16)
    w_g2 = w_in[:, :, IN_GATES + 2 * D_MODEL:].astype(BF16)
    ng = pool_w.shape[1]
    pool_bd = jnp.zeros((depth, D_POOL, D_MODEL), F32)
    for g in range(ng):
        pool_bd = pool_bd.at[:, g * POOL_GROUP:(g + 1) * POOL_GROUP,
                             g * (D_MODEL // ng):(g + 1) * (D_MODEL // ng)].set(pool_w[:, g])
    pool_bd = pool_bd.astype(BF16)
    wck = _block_diag2(w_cmp_k).astype(BF16)
    wcv = _block_diag2(w_cmp_v).astype(BF16)
    w_oc, w_oa, w_mix = w_out_conv.astype(BF16), w_out_attn.astype(BF16), w_mix_out.astype(BF16)
    w_up, w_dn = w_ffn_up.astype(BF16), w_ffn_down.astype(BF16)
    row = lambda a: a[:, None, :]

    ck_s = cache_k_sel.reshape(depth, n_phys, page, D_KV)
    cv_s = cache_v_sel.reshape(depth, n_phys, page, D_KV)
    kcp, vcp = _compress(cache_k_cmp.reshape(depth, n_phys * page, D_KV),
                         cache_v_cmp.reshape(depth, n_phys * page, D_KV), wck, wcv)
    kcp = kcp.reshape(depth, n_phys // 2, 8, D_KV)
    vcp = vcp.reshape(depth, n_phys // 2, 8, D_KV)

    y_p = x_prompt
    y_s = x_sample.reshape(nb * t, D_MODEL)
    st_p, st_s = [], []
    for l in range(depth):
        fw = (w_front[l], conv_w[l], row(conv_b)[l], w_oc[l], pool_bd[l], row(pool_scale)[l])
        bw = (w_g2[l], w_oa[l], w_mix[l], row(ln1_g)[l], row(ln1_b)[l], w_up[l], ffn_conv_w[l],
              row(ffn_conv_b)[l], w_dn[l], row(ln2_g)[l], row(ln2_b)[l])

        pab, q, kc, vc, ks, vs, kw, vw, gate, cst, pst = _front_prompt(y_p, fw)
        kcmp, vcmp = _compress(kc.reshape(1, bp * seq, D_KV), vc.reshape(1, bp * seq, D_KV),
                               wck[l:l + 1], wcv[l:l + 1])
        o = _attn_prompt(q, gate, kcmp.reshape(bp, seq // CMP_BLOCK, D_KV),
                         vcmp.reshape(bp, seq // CMP_BLOCK, D_KV), ks, vs, kw, vw)
        y_p, fst = _back_prompt(y_p, o, pab, bw)
        kv5 = lambda a: a.reshape(bp, seq, N_KV, HEAD_DIM)
        st_p.append((kv5(kc), kv5(vc), kv5(ks), kv5(vs), kv5(kw)[:, seq - w_buf:], kv5(vw)[:, seq - w_buf:],
                     cst[:, 8 - (CONV_W - 1):], pst[:, 16 - POOL_HIST:], fst[:, 8 - (CONV_W - 1):]))

        pab, q, kc, vc, ks, vs, kw, vw, gate, cst, pst = _front_sample(
            y_s.reshape(nb, t, D_MODEL), state_conv[l], state_pool[l], fw, past)
        r3 = lambda a: a.reshape(nb, t, a.shape[-1])
        o, kwin, vwin = _attn_sample(l, page_table, r3(q), r3(gate), r3(ks), r3(vs), r3(kw), r3(vw),
                                     state_k_win[l].reshape(nb, w_buf, D_KV),
                                     state_v_win[l].reshape(nb, w_buf, D_KV), kcp, vcp, ck_s, cv_s)
        y_s, fst = _back_sample(y_s, o.reshape(nb * t, D_ATTN), pab, state_ffn_conv[l], bw, t)
        kv5 = lambda a: a.reshape(nb, -1, N_KV, HEAD_DIM)
        st_s.append((kv5(kc), kv5(vc), kv5(ks), kv5(vs), kv5(kwin), kv5(vwin),
                     cst[:, 8 - (CONV_W - 1):], pst[:, 16 - POOL_HIST:], fst[:, 8 - (CONV_W - 1):]))

    kc_p, vc_p, ks_p, vs_p, kw_p, vw_p, conv_p, pool_p, ffn_p = [jnp.stack(a) for a in zip(*st_p)]
    kc_s, vc_s, ks_s, vs_s, kw_s, vw_s, conv_s, pool_s, ffn_s = [jnp.stack(a) for a in zip(*st_s)]
    return (y_p, y_s.reshape(nb, t, D_MODEL), kc_p, kc_s, vc_p, vc_s, ks_p, ks_s, vs_p, vs_s,
            kw_p, kw_s, vw_p, vw_s, conv_p, conv_s, pool_p, pool_s, ffn_p, ffn_s)
```

```python
import functools

import jax
import jax.numpy as jnp
from jax import lax
from jax.experimental import pallas as pl
from jax.experimental.pallas import tpu as pltpu

F32 = jnp.float32
BF16 = jnp.bfloat16

D_MODEL = 1024
DEPTH = 4
D_CONV = 256
CONV_W = 3
D_POOL = 256
POOL_WINDOWS = (2, 4, 8, 16)
POOL_GROUP = D_POOL // len(POOL_WINDOWS)
POOL_HIST = 15
N_HEADS = 8
N_KV = 2
GROUP = N_HEADS // N_KV
HEAD_DIM = 64
D_ATTN = N_HEADS * HEAD_DIM
D_KV = N_KV * HEAD_DIM
CMP_BLOCK = 32
SEL_BLOCK = 64
TOP_K = 8
WINDOW = 512
N_BRANCH = 3
D_FF = 2816
ALPHA = (2.0 * DEPTH) ** 0.25
LN_EPS = 1e-5
ATTN_SCALE = HEAD_DIM ** -0.5
NEG_INF = -1e30
POS_SPLIT = 64

LANES = 128
SUBLANES = 8
VMEM_LIMIT = 56 * 1024 * 1024

C_AB, C_AC, C_AH, C_U, C_Q = 0, D_CONV, 2 * D_CONV, 3 * D_CONV, 3 * D_CONV + D_POOL
C_KV = C_Q + D_ATTN
C_NSA = C_KV + 6 * D_KV
C_G01 = C_NSA + LANES
W_FRONT = C_G01 + 2 * D_MODEL
IN_MAIN = C_NSA
IN_GATES = IN_MAIN + N_BRANCH * N_HEADS

FRONT_TILE = 256
ATTN_TQ = 128
ATTN_TK = 256
CMP_ROWS = 8192
DEC_STEP = 4
FF_CHUNKS = 2


def _mm(a, b):
    return jnp.dot(a.astype(BF16), b.astype(BF16), preferred_element_type=F32)


def _mm_nt(a, b):
    return lax.dot_general(a.astype(BF16), b.astype(BF16), (((1,), (1,)), ((), ())),
                           preferred_element_type=F32)


def _layer_norm(x, g, b):
    mu = jnp.mean(x, axis=-1, keepdims=True)
    var = jnp.mean(jnp.square(x - mu), axis=-1, keepdims=True)
    return (x - mu) * lax.rsqrt(var + LN_EPS) * g + b


def _iota(shape, axis):
    return lax.broadcasted_iota(jnp.int32, shape, axis)


def _pool_delta(u, shifted, pos):
    nd = u.ndim
    acc = u
    sums = {}
    for j in range(1, POOL_WINDOWS[-1]):
        acc = acc + shifted(j)
        if j + 1 in POOL_WINDOWS:
            sums[j + 1] = acc
    grp = _iota((1,) * (nd - 1) + (D_POOL,), nd - 1) // POOL_GROUP
    s = sums[POOL_WINDOWS[-1]]
    win = jnp.full(grp.shape, POOL_WINDOWS[-1], jnp.int32)
    for g in range(len(POOL_WINDOWS) - 2, -1, -1):
        s = jnp.where(grp == g, sums[POOL_WINDOWS[g]], s)
        win = jnp.where(grp == g, POOL_WINDOWS[g], win)
    cnt = jnp.minimum(win, pos + 1).astype(F32)
    return s / cnt - u


def _front_tail(z, conv, d, woc_ref, pbd_ref, psc_ref, outs, kv_transposed):
    pab_ref, q_ref, kv_refs, gate_ref = outs
    y_a = _mm(z[:, C_AB:C_AB + D_CONV] * conv, woc_ref[...])
    y_b = _mm(d, pbd_ref[...]) * psc_ref[...]
    g0 = jax.nn.sigmoid(z[:, C_G01:C_G01 + D_MODEL])
    g1 = jax.nn.sigmoid(z[:, C_G01 + D_MODEL:C_G01 + 2 * D_MODEL])
    pab_ref[...] = (g0 * y_a + g1 * y_b).reshape(pab_ref.shape)
    q_ref[...] = z[:, C_Q:C_Q + D_ATTN].reshape(q_ref.shape)
    for i, r in enumerate(kv_refs):
        kv = z[:, C_KV + i * D_KV:C_KV + (i + 1) * D_KV]
        r[...] = (kv.T if kv_transposed else kv).reshape(r.shape)
    gate_ref[...] = jax.nn.sigmoid(z[:, C_NSA:C_NSA + LANES]).reshape(gate_ref.shape)


def _front_prompt_body(x_ref, w_ref, cw_ref, cb_ref, woc_ref, pbd_ref, psc_ref,
                       pab_ref, q_ref, kc_ref, vc_ref, ks_ref, vs_ref, kw_ref, vw_ref, gate_ref,
                       cst_ref, pst_ref, extc_ref, extu_ref, *, tt):
    j = pl.program_id(1)

    @pl.when(j == 0)
    def _():
        extc_ref[0:8, :] = jnp.zeros((8, D_CONV), F32)
        extu_ref[0:16, :] = jnp.zeros((16, D_POOL), F32)

    z = _mm(x_ref[0], w_ref[...])
    ch = z[:, C_AC:C_AC + D_CONV] * z[:, C_AH:C_AH + D_CONV]
    extc_ref[8:8 + tt, :] = ch
    conv = cb_ref[...] + cw_ref[0:1, :] * extc_ref[6:6 + tt, :]
    conv = conv + cw_ref[1:2, :] * extc_ref[7:7 + tt, :]
    conv = conv + cw_ref[2:3, :] * ch
    u = z[:, C_U:C_U + D_POOL]
    extu_ref[16:16 + tt, :] = u
    pos = j * tt + _iota((tt, 1), 0)
    d = _pool_delta(u, lambda s: extu_ref[16 - s:16 - s + tt, :], pos)
    _front_tail(z, conv, d, woc_ref, pbd_ref, psc_ref,
                (pab_ref, q_ref, (kc_ref, vc_ref, ks_ref, vs_ref, kw_ref, vw_ref), gate_ref), True)
    last_c = extc_ref[tt:tt + 8, :]
    last_u = extu_ref[tt:tt + 16, :]
    cst_ref[0] = last_c
    pst_ref[0] = last_u
    extc_ref[0:8, :] = last_c
    extu_ref[0:16, :] = last_u


def _front_sample_body(x_ref, hc_ref, hu_ref, w_ref, cw_ref, cb_ref, woc_ref, pbd_ref, psc_ref,
                       pab_ref, q_ref, kc_ref, vc_ref, ks_ref, vs_ref, kw_ref, vw_ref, gate_ref,
                       cst_ref, pst_ref, extc_ref, extu_ref, *, ns, t, pos0):
    z = _mm(x_ref[...], w_ref[...])
    ch = z[:, C_AC:C_AC + D_CONV] * z[:, C_AH:C_AH + D_CONV]
    extc_ref[:, 6:8, :] = hc_ref[...]
    extc_ref[:, 8:8 + t, :] = ch.reshape(ns, t, D_CONV)
    cw = cw_ref[...]
    conv = cb_ref[...] + cw[0:1] * extc_ref[:, 6:6 + t, :]
    conv = conv + cw[1:2] * extc_ref[:, 7:7 + t, :]
    conv = conv + cw[2:3] * extc_ref[:, 8:8 + t, :]
    u = z[:, C_U:C_U + D_POOL]
    extu_ref[:, 1:16, :] = hu_ref[...]
    extu_ref[:, 16:16 + t, :] = u.reshape(ns, t, D_POOL)
    pos = pos0 + _iota((1, t, 1), 1)
    d = _pool_delta(extu_ref[:, 16:16 + t, :], lambda s: extu_ref[:, 16 - s:16 - s + t, :], pos)
    _front_tail(z, conv.reshape(ns * t, D_CONV), d.reshape(ns * t, D_POOL), woc_ref, pbd_ref, psc_ref,
                (pab_ref, q_ref, (kc_ref, vc_ref, ks_ref, vs_ref, kw_ref, vw_ref), gate_ref), False)
    cst_ref[...] = extc_ref[:, t:t + 8, :]
    pst_ref[...] = extu_ref[:, t:t + 16, :]


def _full(shape):
    n = len(shape)
    return pl.BlockSpec(shape, lambda *_: (0,) * n, pipeline_mode=pl.Buffered(1))


def _front_prompt(x, wts):
    b, t, _ = x.shape
    tt = min(FRONT_TILE, t)
    tok = lambda n: pl.BlockSpec((1, tt, n), lambda i, j: (i, j, 0))
    st = lambda r, n: pl.BlockSpec((1, r, n), lambda i, j: (i, 0, 0))
    kvt = pl.BlockSpec((1, D_KV, tt), lambda i, j: (i, 0, j))
    out_shape = ([jax.ShapeDtypeStruct((b, t, D_MODEL), F32), jax.ShapeDtypeStruct((b, t, D_ATTN), F32)]
                 + [jax.ShapeDtypeStruct((b, D_KV, t), F32)] * 6
                 + [jax.ShapeDtypeStruct((b, t, LANES), F32),
                    jax.ShapeDtypeStruct((b, 8, D_CONV), F32), jax.ShapeDtypeStruct((b, 16, D_POOL), F32)])
    out_specs = ([tok(D_MODEL), tok(D_ATTN)] + [kvt] * 6 + [tok(LANES), st(8, D_CONV), st(16, D_POOL)])
    return pl.pallas_call(
        functools.partial(_front_prompt_body, tt=tt),
        grid=(b, t // tt),
        in_specs=[tok(D_MODEL)] + [_full(w.shape) for w in wts],
        out_specs=out_specs,
        out_shape=out_shape,
        scratch_shapes=[pltpu.VMEM((8 + tt, D_CONV), F32), pltpu.VMEM((16 + tt, D_POOL), F32)],
        compiler_params=pltpu.CompilerParams(dimension_semantics=("arbitrary", "arbitrary"),
                                             vmem_limit_bytes=VMEM_LIMIT),
        name="front_prompt",
    )(x, *wts)


def _front_sample(x, hist_c, hist_u, wts, pos0):
    nb, t, _ = x.shape
    ns = min(FRONT_TILE // t, nb)
    rows = ns * t
    xf = x.reshape(nb * t, D_MODEL)
    tok = lambda n: pl.BlockSpec((rows, n), lambda i: (i, 0))
    st = lambda r, n: pl.BlockSpec((ns, r, n), lambda i: (i, 0, 0))
    out_shape = ([jax.ShapeDtypeStruct((nb * t, D_MODEL), F32), jax.ShapeDtypeStruct((nb * t, D_ATTN), F32)]
                 + [jax.ShapeDtypeStruct((nb * t, D_KV), F32)] * 6
                 + [jax.ShapeDtypeStruct((nb * t, LANES), F32),
                    jax.ShapeDtypeStruct((nb, 8, D_CONV), F32), jax.ShapeDtypeStruct((nb, 16, D_POOL), F32)])
    out_specs = ([tok(D_MODEL), tok(D_ATTN)] + [tok(D_KV)] * 6 + [tok(LANES), st(8, D_CONV), st(16, D_POOL)])
    return pl.pallas_call(
        functools.partial(_front_sample_body, ns=ns, t=t, pos0=pos0),
        grid=(nb // ns,),
        in_specs=[tok(D_MODEL), st(CONV_W - 1, D_CONV), st(POOL_HIST, D_POOL)] + [_full(w.shape) for w in wts],
        out_specs=out_specs,
        out_shape=out_shape,
        scratch_shapes=[pltpu.VMEM((ns, 8 + t, D_CONV), F32), pltpu.VMEM((ns, 16 + t, D_POOL), F32)],
        compiler_params=pltpu.CompilerParams(dimension_semantics=("arbitrary",),
                                             vmem_limit_bytes=VMEM_LIMIT),
        name="front_sample",
    )(xf, hist_c, hist_u, *wts)


def _compress_body(k_ref, v_ref, wk_ref, wv_ref, ko_ref, vo_ref, kr_ref, vr_ref, *, npg, r):
    for i in range(npg):
        kr_ref[i * r:(i + 1) * r, :] = k_ref[i].T
        vr_ref[i * r:(i + 1) * r, :] = v_ref[i].T
    nblk = npg * r // CMP_BLOCK
    acc_k = jnp.zeros((nblk, D_KV), F32)
    acc_v = jnp.zeros((nblk, D_KV), F32)
    for l in range(CMP_BLOCK):
        acc_k = acc_k + _mm(kr_ref[pl.ds(l, nblk, stride=CMP_BLOCK), :], wk_ref[l])
        acc_v = acc_v + _mm(vr_ref[pl.ds(l, nblk, stride=CMP_BLOCK), :], wv_ref[l])
    ko_ref[...] = acc_k
    vo_ref[...] = acc_v


def _compress(k, v, wk, wv):
    nl, pages, _, r = k.shape
    npg = next(c for c in range(min(max(CMP_ROWS // r, 1), pages), 0, -1) if pages % c == 0)
    nblk = npg * r // CMP_BLOCK
    rspec = pl.BlockSpec((None, npg, D_KV, r), lambda l, i: (l, i, 0, 0))
    wspec = pl.BlockSpec((None, CMP_BLOCK, D_KV, D_KV), lambda l, i: (l, 0, 0, 0))
    ospec = pl.BlockSpec((None, nblk, D_KV), lambda l, i: (l, i, 0))
    return pl.pallas_call(
        functools.partial(_compress_body, npg=npg, r=r),
        grid=(nl, pages // npg),
        in_specs=[rspec, rspec, wspec, wspec],
        out_specs=[ospec, ospec],
        out_shape=[jax.ShapeDtypeStruct((nl, pages * r // CMP_BLOCK, D_KV), F32)] * 2,
        scratch_shapes=[pltpu.VMEM((npg * r, D_KV), F32)] * 2,
        compiler_params=pltpu.CompilerParams(dimension_semantics=("arbitrary", "arbitrary"),
                                             vmem_limit_bytes=VMEM_LIMIT),
        name="compress",
    )(k, v, wk, wv)


def _make_qbd(q, tq):
    lo = _iota((tq, LANES), 1) < HEAD_DIM
    blocks = {}
    for jv in range(D_ATTN // LANES):
        a = q[:, LANES * jv:LANES * (jv + 1)] * ATTN_SCALE
        r = pltpu.roll(a, HEAD_DIM, 1)
        for half in range(2):
            h = 2 * jv + half
            kv, g = h // GROUP, h % GROUP
            if kv == 0:
                blocks[(g, kv)] = jnp.where(lo, a if half == 0 else r, 0.0)
            else:
                blocks[(g, kv)] = jnp.where(lo, 0.0, a if half == 1 else r)
    return jnp.concatenate([blocks[(g, kv)] for g in range(GROUP) for kv in range(N_KV)], axis=0).astype(BF16)


def _extract_heads(o, tq):
    lo = _iota((tq, LANES), 1) < HEAD_DIM
    cols = []
    for jv in range(D_ATTN // LANES):
        parts = []
        for half in range(2):
            h = 2 * jv + half
            kv, g = h // GROUP, h % GROUP
            blk = o[(g * N_KV + kv) * tq:(g * N_KV + kv + 1) * tq]
            if (half == 1) != (kv == 1):
                blk = pltpu.roll(blk, HEAD_DIM, 1)
            parts.append(blk)
        cols.append(jnp.where(lo, parts[0], parts[1]))
    return jnp.concatenate(cols, axis=1)


def _gate_rows(gate, c, tq):
    return jnp.concatenate([gate[:, c * N_HEADS + kv * GROUP + g:c * N_HEADS + kv * GROUP + g + 1]
                            for g in range(GROUP) for kv in range(N_KV)], axis=0)


def _softmax_rows(s, valid):
    s = jnp.where(valid, s, NEG_INF)
    m = jnp.max(s, axis=-1, keepdims=True)
    e = jnp.exp(s - m)
    p = e / jnp.sum(e, axis=-1, keepdims=True)
    return jnp.where(valid, p, 0.0)


def _select_blocks(imp, qpos2, blkf, n_valid_imp, k, axis):
    cur = (qpos2 // SEL_BLOCK).astype(F32)
    imp = jnp.where(blkf < n_valid_imp, imp, 0.0)
    score = jnp.where(blkf == cur, 2.0 * GROUP, jnp.where(blkf < cur, imp, -1.0))
    score = jnp.where(blkf < 0.0, -3.0, score)
    sel = jnp.zeros(score.shape, F32)
    for _ in range(k):
        m = jnp.max(score, axis=axis, keepdims=True)
        idx = jnp.min(jnp.where(score == m, blkf, 1e9), axis=axis, keepdims=True)
        hit = blkf == idx
        sel = jnp.where(hit, 1.0, sel)
        score = jnp.where(hit, -3.0, score)
    return sel


def _sum_groups(p, rows2):
    out = p[0:rows2]
    for g in range(1, GROUP):
        out = out + p[g * rows2:(g + 1) * rows2]
    return out


def _attn_prompt_body(q_ref, gate_ref, kc_ref, vc_ref, ks_ref, vs_ref, kw_ref, vw_ref,
                      tcol_ref, qs_ref, blkc_ref, trow_ref, e_ref, pos_ref, cpos_ref, o_ref,
                      kas_ref, vas_ref, kaw_ref, vaw_ref, *, tq, tk, seq, top_k):
    qt = pl.program_id(1)
    q0 = qt * tq
    rows = N_HEADS * tq
    rows2 = N_KV * tq
    nt = seq // tk

    @pl.when(qt == 0)
    def _():
        ones = jnp.ones((D_KV, tk), BF16)
        for j in range(nt):
            cs = slice(j * tk, (j + 1) * tk)
            for src, dst in ((ks_ref, kas_ref), (kw_ref, kaw_ref)):
                dst[j, 0:D_KV, :] = src[0, :, cs].astype(BF16)
                dst[j, D_KV:2 * D_KV, :] = pos_ref[j]
            for src, dst in ((vs_ref, vas_ref), (vw_ref, vaw_ref)):
                dst[j, 0:D_KV, :] = src[0, :, cs].astype(BF16)
                dst[j, D_KV:2 * D_KV, :] = ones

    qbd = _make_qbd(q_ref[0], tq)
    qaug = jnp.concatenate([qbd, qs_ref[...]], axis=1)
    qpos2 = q0 + tcol_ref[...]
    gate = gate_ref[0]

    ncb = seq // CMP_BLOCK
    pad = jnp.zeros((LANES - ncb, D_KV), F32)
    kc = jnp.concatenate([kc_ref[0], pad], axis=0)
    vc = jnp.concatenate([vc_ref[0], pad], axis=0)
    kcaug = jnp.concatenate([kc.T.astype(BF16), cpos_ref[...]], axis=0)
    s_c = jnp.dot(qaug, kcaug, preferred_element_type=F32).reshape(GROUP, rows2, LANES)
    blk_end = (_iota((1, LANES), 1) + 1) * CMP_BLOCK - 1
    p_c = _softmax_rows(s_c, (qpos2 >= blk_end)[None]).reshape(rows, LANES)
    o_acc = _gate_rows(gate, 0, tq) * _mm(p_c, vc)

    p4 = _sum_groups(p_c, rows2)
    imp_t = (p4 + pltpu.roll(p4, LANES - 1, 1)).T
    sel = _select_blocks(imp_t, q0 + trow_ref[...], blkc_ref[...], float(seq // SEL_BLOCK), top_k, 0).T
    sel = sel.astype(BF16)

    def branch(kaug_ref, vaug_ref, j_lo, j_hi, mask_fn):
        def step(j, carry):
            m, acc = carry
            s = jnp.dot(qaug, kaug_ref[j], preferred_element_type=F32).reshape(GROUP, rows2, tk)
            dist = qpos2 - (j * tk + _iota((1, tk), 1))
            s = jnp.where(mask_fn(j, dist)[None], s, NEG_INF)
            m_new = jnp.maximum(m, jnp.max(s, axis=2, keepdims=True))
            alpha = jnp.exp(m - m_new).reshape(rows, 1)
            p = jnp.exp(s - m_new).astype(BF16).reshape(rows, tk)
            pv = lax.dot_general(p, vaug_ref[j], (((1,), (1,)), ((), ())), preferred_element_type=F32)
            return m_new, alpha * acc + pv

        init = (jnp.full((GROUP, rows2, 1), NEG_INF, F32), jnp.zeros((rows, 2 * D_KV), F32))
        _, acc = lax.fori_loop(j_lo, j_hi, step, init)
        return acc[:, 0:D_KV] / acc[:, D_KV:2 * D_KV]

    j_hi = (q0 + tq - 1) // tk + 1

    def sel_mask(j, dist):
        picked = jnp.dot(sel, e_ref[j], preferred_element_type=F32)
        return (picked > 0.5) & (dist >= 0)

    o_acc = o_acc + _gate_rows(gate, 1, tq) * branch(kas_ref, vas_ref, 0, j_hi, sel_mask)

    def win_mask(j, dist):
        return (dist >= 0) & (dist <= WINDOW)

    j_lo = jnp.maximum(q0 - WINDOW, 0) // tk
    o_acc = o_acc + _gate_rows(gate, 2, tq) * branch(kaw_ref, vaw_ref, j_lo, j_hi, win_mask)
    o_ref[0] = _extract_heads(o_acc, tq)


def _row_consts(tq):
    r = jnp.arange(N_HEADS * tq)
    g, kv, t = r // (N_KV * tq), (r // tq) % N_KV, r % tq
    head = kv * GROUP + g
    slopes = jnp.exp2(-8.0 * (head + 1).astype(F32) / N_HEADS)
    return slopes[:, None], t.astype(jnp.int32)[:, None]


def _key_tiles(a, tk):
    return a.reshape(a.shape[0], -1, tk).transpose(1, 0, 2)


def _attn_prompt(q, gate, kcmp, vcmp, ks, vs, kw, vw):
    b, t, _ = q.shape
    tq = min(ATTN_TQ, t)
    tk = min(ATTN_TK, t)
    n_sb = t // SEL_BLOCK
    assert t // POS_SPLIT < 256
    slopes, tcol = _row_consts(tq)
    qs = jnp.zeros((N_HEADS * tq, LANES), F32).at[:, 0].set(POS_SPLIT * slopes[:, 0]).at[:, 1].set(slopes[:, 0])
    kpos = jnp.arange(t)
    pos = jnp.zeros((LANES, t), F32).at[0].set(kpos // POS_SPLIT).at[1].set(kpos % POS_SPLIT)
    lane = jnp.arange(LANES)
    blk = jnp.where((lane % 2 == 0) & (lane < 2 * n_sb), lane // 2, -1).astype(F32)
    e = (blk[:, None] == (kpos // SEL_BLOCK).astype(F32)[None, :])
    blk_end = (lane + 1) * CMP_BLOCK - 1
    cpos = jnp.zeros((LANES, LANES), F32).at[0].set(blk_end // POS_SPLIT).at[1].set(blk_end % POS_SPLIT)
    tcol2 = tcol[:N_KV * tq]
    consts = (tcol2, qs.astype(BF16), blk[:, None], tcol2.reshape(1, -1),
              _key_tiles(e.astype(BF16), tk), _key_tiles(pos.astype(BF16), tk), cpos.astype(BF16))
    tile = lambda n: pl.BlockSpec((1, tq, n), lambda i, j: (i, j, 0))
    seq = lambda r, n: pl.BlockSpec((1, r, n), lambda i, j: (i, 0, 0))
    return pl.pallas_call(
        functools.partial(_attn_prompt_body, tq=tq, tk=tk, seq=t, top_k=min(TOP_K, n_sb)),
        grid=(b, t // tq),
        in_specs=[tile(D_ATTN), tile(LANES), seq(t // CMP_BLOCK, D_KV), seq(t // CMP_BLOCK, D_KV),
                  seq(D_KV, t), seq(D_KV, t), seq(D_KV, t), seq(D_KV, t)] + [_full(c.shape) for c in consts],
        out_specs=tile(D_ATTN),
        out_shape=jax.ShapeDtypeStruct((b, t, D_ATTN), F32),
        scratch_shapes=[pltpu.VMEM((t // tk, 2 * D_KV, tk), BF16)] * 4,
        compiler_params=pltpu.CompilerParams(dimension_semantics=("arbitrary", "arbitrary"),
                                             vmem_limit_bytes=VMEM_LIMIT),
        name="attn_prompt",
    )(q, gate, kcmp, vcmp, ks, vs, kw, vw, *consts)


def _attn_sample_body(*refs, nbs, **static):
    for bb in range(nbs):
        _attn_sample_one(bb, *refs, nbs=nbs, **static)


def _attn_sample_one(bb, pt_ref, q_ref, gate_ref, ksn_ref, vsn_ref, kwn_ref, vwn_ref, kwin_ref, vwin_ref,
                     kcp_ref, vcp_ref, cvalid_ref, cend_ref, slope_ref, tcol_ref, blkf_ref, e_ref, *rest,
                     t, n_pages, page, past, w_buf, top_k, nbs):
    ks_pages = rest[bb * n_pages:(bb + 1) * n_pages]
    vs_pages = rest[(nbs + bb) * n_pages:(nbs + bb + 1) * n_pages]
    o_ref, kwo_ref, vwo_ref = rest[2 * nbs * n_pages:]
    b = pl.program_id(0) * nbs + bb
    rows2 = N_KV * t
    qbd = _make_qbd(q_ref[bb], t)
    slope = slope_ref[...]
    qpos = past + tcol_ref[...]
    gate = gate_ref[bb]

    def pad_rows(x):
        if x.shape[0] == LANES:
            return x
        return jnp.concatenate([x, jnp.zeros((LANES - x.shape[0], D_KV), F32)], axis=0)

    tiles_k = [kcp_ref[pt_ref[b, p] // 2] for p in range(n_pages)]
    tiles_v = [vcp_ref[pt_ref[b, p] // 2] for p in range(n_pages)]
    kc = pad_rows(jnp.concatenate(tiles_k, axis=0))
    vc = pad_rows(jnp.concatenate(tiles_v, axis=0))
    dist_c = qpos - cend_ref[...]
    s_c = _mm_nt(qbd, kc) - slope * dist_c.astype(F32)
    p_c = _softmax_rows(s_c, (cvalid_ref[bb] > 0.5) & (dist_c >= 0))
    o_acc = _gate_rows(gate, 0, t) * _mm(p_c, vc)

    p4 = _sum_groups(p_c, rows2)
    x1 = p4 + pltpu.roll(p4, LANES - 1, 1)
    imp = x1 + pltpu.roll(x1, LANES - 4, 1)
    sel = _select_blocks(imp, qpos[0:rows2], blkf_ref[...], float(past // SEL_BLOCK), top_k, 1)
    sel4 = jnp.concatenate([sel] * GROUP, axis=0)

    ks_new = pad_rows(ksn_ref[bb])
    vs_new = pad_rows(vsn_ref[bb])
    s_s = jnp.concatenate([_mm(qbd, r[...]) for r in ks_pages] + [_mm_nt(qbd, ks_new)], axis=1)
    n_keys = past + LANES
    dist_s = qpos - _iota((1, n_keys), 1)
    picked = _mm(sel4, e_ref[...])
    p_s = _softmax_rows(s_s - slope * dist_s.astype(F32), (picked > 0.5) & (dist_s >= 0))
    o_s = _mm(p_s[:, past:], vs_new)
    for p in range(n_pages):
        o_s = o_s + _mm_nt(p_s[:, p * page:(p + 1) * page], vs_pages[p][...])
    o_acc = o_acc + _gate_rows(gate, 1, t) * o_s

    kwin = kwin_ref[bb]
    vwin = vwin_ref[bb]
    kw_new = kwn_ref[bb]
    vw_new = vwn_ref[bb]
    s_w = jnp.concatenate([_mm(qbd, kwin), _mm_nt(qbd, pad_rows(kw_new))], axis=1)
    dist_w = qpos - (past - w_buf + _iota((1, w_buf + LANES), 1))
    p_w = _softmax_rows(s_w - slope * dist_w.astype(F32), (dist_w >= 0) & (dist_w <= WINDOW))
    o_w = _mm_nt(p_w[:, :w_buf], vwin) + _mm(p_w[:, w_buf:], pad_rows(vw_new))
    o_acc = o_acc + _gate_rows(gate, 2, t) * o_w
    o_ref[bb] = _extract_heads(o_acc, t)

    is_new = _iota((D_KV, LANES), 1) >= LANES - t

    def shift_in(win, new):
        new_t = jnp.concatenate([jnp.zeros((LANES - t, D_KV), F32), new], axis=0).T
        rolled = pltpu.roll(win, w_buf - t, 1)
        last = jnp.where(is_new, new_t, rolled[:, w_buf - LANES:])
        return jnp.concatenate([rolled[:, :w_buf - LANES], last], axis=1)

    kwo_ref[bb] = shift_in(kwin, kw_new)
    vwo_ref[bb] = shift_in(vwin, vw_new)


def _page_map(l, nbs, bb, p, i, pt):
    return (l, pt[i * nbs + bb, p], 0, 0)


def _attn_sample(l, page_table, q, gate, ks_new, vs_new, kw_new, vw_new, kwin, vwin, kcp, vcp,
                 cache_ks, cache_vs):
    nb, t, _ = q.shape
    n_pages = page_table.shape[1]
    page = cache_ks.shape[3]
    past = n_pages * page
    w_buf = kwin.shape[3]
    n_sb = -(-(past + t) // SEL_BLOCK)
    assert t == SUBLANES and n_pages * 8 <= LANES and page == LANES and past % SEL_BLOCK == 0
    assert w_buf % LANES == 0
    slopes, tcol = _row_consts(t)
    lane = jnp.arange(LANES)
    pg, r = lane // 8, lane % 8
    in_range = pg < n_pages
    cend = jnp.where(in_range, (4 * pg + r % 4 + 1) * CMP_BLOCK - 1, 1 << 30).astype(jnp.int32)[None, :]
    par = page_table % 2
    par_l = jnp.take(par, jnp.minimum(pg, n_pages - 1), axis=1)
    cvalid = ((r[None, :] // 4 == par_l) & in_range[None, :]).astype(F32)[:, None, :]
    blk = jnp.where(in_range & (r % 2 == 0) & (r < 4), 2 * pg + r // 2, -1)
    blk = jnp.where(lane == 1, n_sb - 1, blk)
    blkf = blk.astype(F32)[None, :]
    e = (blkf.reshape(-1, 1) == (jnp.arange(past + LANES) // SEL_BLOCK).astype(F32)[None, :]).astype(BF16)
    nbs = next(c for c in range(min(DEC_STEP, nb), 0, -1) if nb % c == 0)
    tok = lambda n: pl.BlockSpec((nbs, t, n), lambda i, pt: (i, 0, 0))
    win = pl.BlockSpec((None, nbs, D_KV, w_buf), lambda i, pt: (l, i, 0, 0))
    win_out = pl.BlockSpec((nbs, D_KV, w_buf), lambda i, pt: (i, 0, 0))
    res = pl.BlockSpec((None,) + kcp.shape[1:], lambda i, pt: (l, 0, 0, 0))
    const = lambda a: pl.BlockSpec(a.shape, lambda i, pt: (0,) * a.ndim)
    pages = [pl.BlockSpec((None, None, D_KV, page), functools.partial(_page_map, l, nbs, bb, p))
             for bb in range(nbs) for p in range(n_pages)]
    grid_spec = pltpu.PrefetchScalarGridSpec(
        num_scalar_prefetch=1,
        grid=(nb // nbs,),
        in_specs=[tok(D_ATTN), tok(LANES), tok(D_KV), tok(D_KV), tok(D_KV), tok(D_KV), win, win, res, res,
                  pl.BlockSpec((nbs, 1, LANES), lambda i, pt: (i, 0, 0)),
                  const(cend), const(slopes), const(tcol), const(blkf), const(e)] + pages + pages,
        out_specs=[tok(D_ATTN), win_out, win_out],
    )
    return pl.pallas_call(
        functools.partial(_attn_sample_body, t=t, n_pages=n_pages, page=page, past=past, w_buf=w_buf,
                          top_k=min(TOP_K, n_sb), nbs=nbs),
        grid_spec=grid_spec,
        out_shape=[jax.ShapeDtypeStruct((nb, t, D_ATTN), F32),
                   jax.ShapeDtypeStruct((nb, D_KV, w_buf), F32), jax.ShapeDtypeStruct((nb, D_KV, w_buf), F32)],
        compiler_params=pltpu.CompilerParams(dimension_semantics=("arbitrary",), vmem_limit_bytes=VMEM_LIMIT),
        name="attn_sample",
    )(page_table, q, gate, ks_new, vs_new, kw_new, vw_new, kwin, vwin, kcp, vcp, cvalid,
      cend, slopes, tcol, blkf, e, *([cache_ks] * (nbs * n_pages)), *([cache_vs] * (nbs * n_pages)))


def _back_head(x, o, pab, wg2_ref, woa_ref, wmix_ref, ln1g_ref, ln1b_ref):
    y_c = _mm(o, woa_ref[...])
    g2 = jax.nn.sigmoid(_mm(x, wg2_ref[...]))
    mixed = pab + g2 * y_c
    return _layer_norm(ALPHA * x + _mm(mixed, wmix_ref[...]), ln1g_ref[...], ln1b_ref[...])


def _back_prompt_body(x_ref, o_ref, pab_ref, wg2_ref, woa_ref, wmix_ref, ln1g_ref, ln1b_ref,
                      wup_ref, fcw_ref, fcb_ref, wdn_ref, ln2g_ref, ln2b_ref,
                      y_ref, fst_ref, ext_ref, *, tt):
    j = pl.program_id(1)

    @pl.when(j == 0)
    def _():
        ext_ref[0:8, :] = jnp.zeros((8, D_FF), F32)

    h = _back_head(x_ref[0], o_ref[0], pab_ref[0], wg2_ref, woa_ref, wmix_ref, ln1g_ref, ln1b_ref)
    hb = h.astype(BF16)
    cf = D_FF // FF_CHUNKS
    f = jnp.zeros((tt, D_MODEL), F32)
    for c in range(FF_CHUNKS):
        cs = slice(c * cf, (c + 1) * cf)
        u = _mm(hb, wup_ref[:, c * cf:(c + 1) * cf])
        gt = _mm(hb, wup_ref[:, D_FF + c * cf:D_FF + (c + 1) * cf])
        ext_ref[8:8 + tt, cs] = u
        conv = fcb_ref[:, cs] + fcw_ref[0:1, cs] * ext_ref[6:6 + tt, cs]
        conv = conv + fcw_ref[1:2, cs] * ext_ref[7:7 + tt, cs]
        conv = conv + fcw_ref[2:3, cs] * u
        f = f + _mm(jax.nn.gelu(conv) * gt, wdn_ref[c * cf:(c + 1) * cf, :])
    y_ref[0] = _layer_norm(ALPHA * h + f, ln2g_ref[...], ln2b_ref[...])
    last = ext_ref[tt:tt + 8, :]
    fst_ref[0] = last
    ext_ref[0:8, :] = last


def _back_sample_body(x_ref, o_ref, pab_ref, hf_ref, wg2_ref, woa_ref, wmix_ref, ln1g_ref, ln1b_ref,
                      wup_ref, fcw_ref, fcb_ref, wdn_ref, ln2g_ref, ln2b_ref,
                      y_ref, fst_ref, ext_ref, *, ns, t):
    h = _back_head(x_ref[...], o_ref[...], pab_ref[...], wg2_ref, woa_ref, wmix_ref, ln1g_ref, ln1b_ref)
    hb = h.astype(BF16)
    cf = D_FF // FF_CHUNKS
    ext_ref[:, 6:8, :] = hf_ref[...]
    f = jnp.zeros((ns * t, D_MODEL), F32)
    for c in range(FF_CHUNKS):
        cs = slice(c * cf, (c + 1) * cf)
        u = _mm(hb, wup_ref[:, c * cf:(c + 1) * cf])
        gt = _mm(hb, wup_ref[:, D_FF + c * cf:D_FF + (c + 1) * cf])
        ext_ref[:, 8:8 + t, cs] = u.reshape(ns, t, cf)
        fcw = fcw_ref[:, cs]
        conv = fcb_ref[:, cs] + fcw[0:1] * ext_ref[:, 6:6 + t, cs]
        conv = conv + fcw[1:2] * ext_ref[:, 7:7 + t, cs]
        conv = conv + fcw[2:3] * ext_ref[:, 8:8 + t, cs]
        f = f + _mm(jax.nn.gelu(conv).reshape(ns * t, cf) * gt, wdn_ref[c * cf:(c + 1) * cf, :])
    y_ref[...] = _layer_norm(ALPHA * h + f, ln2g_ref[...], ln2b_ref[...])
    fst_ref[...] = ext_ref[:, t:t + 8, :]


def _back_prompt(x, o, pab, wts):
    b, t, _ = x.shape
    tt = min(FRONT_TILE, t)
    tok = lambda n: pl.BlockSpec((1, tt, n), lambda i, j: (i, j, 0))
    return pl.pallas_call(
        functools.partial(_back_prompt_body, tt=tt),
        grid=(b, t // tt),
        in_specs=[tok(D_MODEL), tok(D_ATTN), tok(D_MODEL)] + [_full(w.shape) for w in wts],
        out_specs=[tok(D_MODEL), pl.BlockSpec((1, 8, D_FF), lambda i, j: (i, 0, 0))],
        out_shape=[jax.ShapeDtypeStruct((b, t, D_MODEL), F32), jax.ShapeDtypeStruct((b, 8, D_FF), F32)],
        scratch_shapes=[pltpu.VMEM((8 + tt, D_FF), F32)],
        compiler_params=pltpu.CompilerParams(dimension_semantics=("arbitrary", "arbitrary"),
                                             vmem_limit_bytes=VMEM_LIMIT),
        name="back_prompt",
    )(x, o, pab, *wts)


def _back_sample(x, o, pab, hist_f, wts, t):
    n = x.shape[0]
    nb = n // t
    ns = min(FRONT_TILE // t, nb)
    rows = ns * t
    tok = lambda c: pl.BlockSpec((rows, c), lambda i: (i, 0))
    return pl.pallas_call(
        functools.partial(_back_sample_body, ns=ns, t=t),
        grid=(nb // ns,),
        in_specs=[tok(D_MODEL), tok(D_ATTN), tok(D_MODEL),
                  pl.BlockSpec((ns, CONV_W - 1, D_FF), lambda i: (i, 0, 0))] + [_full(w.shape) for w in wts],
        out_specs=[tok(D_MODEL), pl.BlockSpec((ns, 8, D_FF), lambda i: (i, 0, 0))],
        out_shape=[jax.ShapeDtypeStruct((n, D_MODEL), F32), jax.ShapeDtypeStruct((nb, 8, D_FF), F32)],
        scratch_shapes=[pltpu.VMEM((ns, 8 + t, D_FF), F32)],
        compiler_params=pltpu.CompilerParams(dimension_semantics=("arbitrary",), vmem_limit_bytes=VMEM_LIMIT),
        name="back_sample",
    )(x, o, pab, hist_f, *wts)


def _block_diag2(w):
    z = jnp.zeros_like(w)
    return jnp.concatenate([jnp.concatenate([w, z], axis=-1), jnp.concatenate([z, w], axis=-1)], axis=-2)


def kernel(x_prompt, x_sample, cache_k_cmp, cache_v_cmp, cache_k_sel, cache_v_sel, state_k_win, state_v_win, state_conv, state_pool, state_ffn_conv, page_table, w_in, conv_w, conv_b, w_out_conv, pool_w, pool_scale, w_cmp_k, w_cmp_v, w_out_attn, w_mix_out, ln1_g, ln1_b, w_ffn_up, ffn_conv_w, ffn_conv_b, w_ffn_down, ln2_g, ln2_b):
    depth = w_in.shape[0]
    bp, seq, _ = x_prompt.shape
    nb, t, _ = x_sample.shape
    n_phys, page = cache_k_cmp.shape[1], cache_k_cmp.shape[2]
    past = page_table.shape[1] * page
    w_buf = state_k_win.shape[2]

    w_front = jnp.concatenate(
        [w_in[:, :, :IN_GATES], jnp.zeros((depth, D_MODEL, C_G01 - IN_GATES), F32),
         w_in[:, :, IN_GATES:IN_GATES + 2 * D_MODEL]], axis=2).astype(BF16)
    w_g2 = w_in[:, :, IN_GATES + 2 * D_MODEL:].astype(BF16)
    ng = pool_w.shape[1]
    pool_bd = jnp.zeros((depth, D_POOL, D_MODEL), F32)
    for g in range(ng):
        pool_bd = pool_bd.at[:, g * POOL_GROUP:(g + 1) * POOL_GROUP,
                             g * (D_MODEL // ng):(g + 1) * (D_MODEL // ng)].set(pool_w[:, g])
    pool_bd = pool_bd.astype(BF16)
    wck = _block_diag2(w_cmp_k).astype(BF16)
    wcv = _block_diag2(w_cmp_v).astype(BF16)
    w_oc, w_oa, w_mix = w_out_conv.astype(BF16), w_out_attn.astype(BF16), w_mix_out.astype(BF16)
    w_up, w_dn = w_ffn_up.astype(BF16), w_ffn_down.astype(BF16)
    row = lambda a: a[:, None, :]

    keys_minor = lambda a: jnp.transpose(a, (0, 1, 3, 4, 2)).reshape(a.shape[0], a.shape[1], D_KV, a.shape[2])
    tokens_major = lambda a: jnp.transpose(a.reshape(a.shape[0], N_KV, HEAD_DIM, a.shape[2]), (0, 3, 1, 2))
    ck_s, cv_s = keys_minor(cache_k_sel), keys_minor(cache_v_sel)
    kwin_all, vwin_all = keys_minor(state_k_win), keys_minor(state_v_win)
    kcp, vcp = _compress(keys_minor(cache_k_cmp), keys_minor(cache_v_cmp), wck, wcv)
    kcp = kcp.reshape(depth, n_phys // 2, 8, D_KV)
    vcp = vcp.reshape(depth, n_phys // 2, 8, D_KV)

    y_p = x_prompt
    y_s = x_sample.reshape(nb * t, D_MODEL)
    st_p, st_s = [], []
    for l in range(depth):
        fw = (w_front[l], conv_w[l], row(conv_b)[l], w_oc[l], pool_bd[l], row(pool_scale)[l])
        bw = (w_g2[l], w_oa[l], w_mix[l], row(ln1_g)[l], row(ln1_b)[l], w_up[l], ffn_conv_w[l],
              row(ffn_conv_b)[l], w_dn[l], row(ln2_g)[l], row(ln2_b)[l])

        pab, q, kc, vc, ks, vs, kw, vw, gate, cst, pst = _front_prompt(y_p, fw)
        kcmp, vcmp = _compress(kc[None], vc[None], wck[l:l + 1], wcv[l:l + 1])
        o = _attn_prompt(q, gate, kcmp.reshape(bp, seq // CMP_BLOCK, D_KV),
                         vcmp.reshape(bp, seq // CMP_BLOCK, D_KV), ks, vs, kw, vw)
        y_p, fst = _back_prompt(y_p, o, pab, bw)
        st_p.append(tuple(tokens_major(a) for a in (kc, vc, ks, vs, kw[:, :, seq - w_buf:], vw[:, :, seq - w_buf:]))
                    + (cst[:, 8 - (CONV_W - 1):], pst[:, 16 - POOL_HIST:], fst[:, 8 - (CONV_W - 1):]))

        pab, q, kc, vc, ks, vs, kw, vw, gate, cst, pst = _front_sample(
            y_s.reshape(nb, t, D_MODEL), state_conv[l], state_pool[l], fw, past)
        r3 = lambda a: a.reshape(nb, t, a.shape[-1])
        o, kwin, vwin = _attn_sample(l, page_table, r3(q), r3(gate), r3(ks), r3(vs), r3(kw), r3(vw),
                                     kwin_all, vwin_all, kcp, vcp, ck_s, cv_s)
        y_s, fst = _back_sample(y_s, o.reshape(nb * t, D_ATTN), pab, state_ffn_conv[l], bw, t)
        kv5 = lambda a: a.reshape(nb, t, N_KV, HEAD_DIM)
        st_s.append((kv5(kc), kv5(vc), kv5(ks), kv5(vs), tokens_major(kwin), tokens_major(vwin),
                     cst[:, 8 - (CONV_W - 1):], pst[:, 16 - POOL_HIST:], fst[:, 8 - (CONV_W - 1):]))

    kc_p, vc_p, ks_p, vs_p, kw_p, vw_p, conv_p, pool_p, ffn_p = [jnp.stack(a) for a in zip(*st_p)]
    kc_s, vc_s, ks_s, vs_s, kw_s, vw_s, conv_s, pool_s, ffn_s = [jnp.stack(a) for a in zip(*st_s)]
    return (y_p, y_s.reshape(nb, t, D_MODEL), kc_p, kc_s, vc_p, vc_s, ks_p, ks_s, vs_p, vs_s,
            kw_p, kw_s, vw_p, vw_s, conv_p, conv_s, pool_p, pool_s, ffn_p, ffn_s)
```

```python
import functools

import jax
import jax.numpy as jnp
from jax import lax
from jax.experimental import pallas as pl
from jax.experimental.pallas import tpu as pltpu

F32 = jnp.float32
BF16 = jnp.bfloat16

D_MODEL = 1024
DEPTH = 4
D_CONV = 256
CONV_W = 3
D_POOL = 256
POOL_WINDOWS = (2, 4, 8, 16)
POOL_GROUP = D_POOL // len(POOL_WINDOWS)
POOL_HIST = 15
N_HEADS = 8
N_KV = 2
GROUP = N_HEADS // N_KV
HEAD_DIM = 64
D_ATTN = N_HEADS * HEAD_DIM
D_KV = N_KV * HEAD_DIM
CMP_BLOCK = 32
SEL_BLOCK = 64
TOP_K = 8
WINDOW = 512
N_BRANCH = 3
D_FF = 2816
ALPHA = (2.0 * DEPTH) ** 0.25
LN_EPS = 1e-5
ATTN_SCALE = HEAD_DIM ** -0.5
NEG_INF = -1e30
POS_SPLIT = 64

LANES = 128
SUBLANES = 8
VMEM_LIMIT = 56 * 1024 * 1024

C_AB, C_AC, C_AH, C_U, C_Q = 0, D_CONV, 2 * D_CONV, 3 * D_CONV, 3 * D_CONV + D_POOL
C_KV = C_Q + D_ATTN
C_NSA = C_KV + 6 * D_KV
C_G01 = C_NSA + LANES
W_FRONT = C_G01 + 2 * D_MODEL
IN_MAIN = C_NSA
IN_GATES = IN_MAIN + N_BRANCH * N_HEADS

FRONT_TILE = 256
ATTN_TQ = 128
ATTN_TK = 256
CMP_ROWS = 8192
CMP_PITCH = LANES + SUBLANES
DEC_STEP = 4
FF_CHUNKS = 2


def _mm(a, b):
    return jnp.dot(a.astype(BF16), b.astype(BF16), preferred_element_type=F32)


def _mm_nt(a, b):
    return lax.dot_general(a.astype(BF16), b.astype(BF16), (((1,), (1,)), ((), ())),
                           preferred_element_type=F32)


def _layer_norm(x, g, b):
    mu = jnp.mean(x, axis=-1, keepdims=True)
    var = jnp.mean(jnp.square(x - mu), axis=-1, keepdims=True)
    return (x - mu) * lax.rsqrt(var + LN_EPS) * g + b


def _iota(shape, axis):
    return lax.broadcasted_iota(jnp.int32, shape, axis)


def _pool_delta(u, shifted, pos):
    nd = u.ndim
    acc = u
    sums = {}
    for j in range(1, POOL_WINDOWS[-1]):
        acc = acc + shifted(j)
        if j + 1 in POOL_WINDOWS:
            sums[j + 1] = acc
    grp = _iota((1,) * (nd - 1) + (D_POOL,), nd - 1) // POOL_GROUP
    s = sums[POOL_WINDOWS[-1]]
    win = jnp.full(grp.shape, POOL_WINDOWS[-1], jnp.int32)
    for g in range(len(POOL_WINDOWS) - 2, -1, -1):
        s = jnp.where(grp == g, sums[POOL_WINDOWS[g]], s)
        win = jnp.where(grp == g, POOL_WINDOWS[g], win)
    cnt = jnp.minimum(win, pos + 1).astype(F32)
    return s / cnt - u


def _front_tail(z, conv, d, woc_ref, pbd_ref, psc_ref, outs, kv_transposed):
    pab_ref, q_ref, kv_refs, gate_ref = outs
    y_a = _mm(z[:, C_AB:C_AB + D_CONV] * conv, woc_ref[...])
    y_b = _mm(d, pbd_ref[...]) * psc_ref[...]
    g0 = jax.nn.sigmoid(z[:, C_G01:C_G01 + D_MODEL])
    g1 = jax.nn.sigmoid(z[:, C_G01 + D_MODEL:C_G01 + 2 * D_MODEL])
    pab_ref[...] = (g0 * y_a + g1 * y_b).reshape(pab_ref.shape)
    q_ref[...] = z[:, C_Q:C_Q + D_ATTN].reshape(q_ref.shape)
    for i, r in enumerate(kv_refs):
        kv = z[:, C_KV + i * D_KV:C_KV + (i + 1) * D_KV]
        r[...] = (kv.T if kv_transposed else kv).reshape(r.shape)
    gate_ref[...] = jax.nn.sigmoid(z[:, C_NSA:C_NSA + LANES]).reshape(gate_ref.shape)


def _front_prompt_body(x_ref, w_ref, cw_ref, cb_ref, woc_ref, pbd_ref, psc_ref,
                       pab_ref, q_ref, kc_ref, vc_ref, ks_ref, vs_ref, kw_ref, vw_ref, gate_ref,
                       cst_ref, pst_ref, extc_ref, extu_ref, *, tt):
    j = pl.program_id(1)

    @pl.when(j == 0)
    def _():
        extc_ref[0:8, :] = jnp.zeros((8, D_CONV), F32)
        extu_ref[0:16, :] = jnp.zeros((16, D_POOL), F32)

    z = _mm(x_ref[0], w_ref[...])
    ch = z[:, C_AC:C_AC + D_CONV] * z[:, C_AH:C_AH + D_CONV]
    extc_ref[8:8 + tt, :] = ch
    conv = cb_ref[...] + cw_ref[0:1, :] * extc_ref[6:6 + tt, :]
    conv = conv + cw_ref[1:2, :] * extc_ref[7:7 + tt, :]
    conv = conv + cw_ref[2:3, :] * ch
    u = z[:, C_U:C_U + D_POOL]
    extu_ref[16:16 + tt, :] = u
    pos = j * tt + _iota((tt, 1), 0)
    d = _pool_delta(u, lambda s: extu_ref[16 - s:16 - s + tt, :], pos)
    _front_tail(z, conv, d, woc_ref, pbd_ref, psc_ref,
                (pab_ref, q_ref, (kc_ref, vc_ref, ks_ref, vs_ref, kw_ref, vw_ref), gate_ref), True)
    last_c = extc_ref[tt:tt + 8, :]
    last_u = extu_ref[tt:tt + 16, :]
    cst_ref[0] = last_c
    pst_ref[0] = last_u
    extc_ref[0:8, :] = last_c
    extu_ref[0:16, :] = last_u


def _front_sample_body(x_ref, hc_ref, hu_ref, w_ref, cw_ref, cb_ref, woc_ref, pbd_ref, psc_ref,
                       pab_ref, q_ref, kc_ref, vc_ref, ks_ref, vs_ref, kw_ref, vw_ref, gate_ref,
                       cst_ref, pst_ref, extc_ref, extu_ref, *, ns, t, pos0):
    z = _mm(x_ref[...], w_ref[...])
    ch = z[:, C_AC:C_AC + D_CONV] * z[:, C_AH:C_AH + D_CONV]
    extc_ref[:, 6:8, :] = hc_ref[...]
    extc_ref[:, 8:8 + t, :] = ch.reshape(ns, t, D_CONV)
    cw = cw_ref[...]
    conv = cb_ref[...] + cw[0:1] * extc_ref[:, 6:6 + t, :]
    conv = conv + cw[1:2] * extc_ref[:, 7:7 + t, :]
    conv = conv + cw[2:3] * extc_ref[:, 8:8 + t, :]
    u = z[:, C_U:C_U + D_POOL]
    extu_ref[:, 1:16, :] = hu_ref[...]
    extu_ref[:, 16:16 + t, :] = u.reshape(ns, t, D_POOL)
    pos = pos0 + _iota((1, t, 1), 1)
    d = _pool_delta(extu_ref[:, 16:16 + t, :], lambda s: extu_ref[:, 16 - s:16 - s + t, :], pos)
    _front_tail(z, conv.reshape(ns * t, D_CONV), d.reshape(ns * t, D_POOL), woc_ref, pbd_ref, psc_ref,
                (pab_ref, q_ref, (kc_ref, vc_ref, ks_ref, vs_ref, kw_ref, vw_ref), gate_ref), False)
    cst_ref[...] = extc_ref[:, t:t + 8, :]
    pst_ref[...] = extu_ref[:, t:t + 16, :]


def _full(shape):
    n = len(shape)
    return pl.BlockSpec(shape, lambda *_: (0,) * n, pipeline_mode=pl.Buffered(1))


def _front_prompt(x, wts):
    b, t, _ = x.shape
    tt = min(FRONT_TILE, t)
    tok = lambda n: pl.BlockSpec((1, tt, n), lambda i, j: (i, j, 0))
    st = lambda r, n: pl.BlockSpec((1, r, n), lambda i, j: (i, 0, 0))
    kvt = pl.BlockSpec((1, D_KV, tt), lambda i, j: (i, 0, j))
    out_shape = ([jax.ShapeDtypeStruct((b, t, D_MODEL), F32), jax.ShapeDtypeStruct((b, t, D_ATTN), F32)]
                 + [jax.ShapeDtypeStruct((b, D_KV, t), F32)] * 6
                 + [jax.ShapeDtypeStruct((b, t, LANES), F32),
                    jax.ShapeDtypeStruct((b, 8, D_CONV), F32), jax.ShapeDtypeStruct((b, 16, D_POOL), F32)])
    out_specs = ([tok(D_MODEL), tok(D_ATTN)] + [kvt] * 6 + [tok(LANES), st(8, D_CONV), st(16, D_POOL)])
    return pl.pallas_call(
        functools.partial(_front_prompt_body, tt=tt),
        grid=(b, t // tt),
        in_specs=[tok(D_MODEL)] + [_full(w.shape) for w in wts],
        out_specs=out_specs,
        out_shape=out_shape,
        scratch_shapes=[pltpu.VMEM((8 + tt, D_CONV), F32), pltpu.VMEM((16 + tt, D_POOL), F32)],
        compiler_params=pltpu.CompilerParams(dimension_semantics=("arbitrary", "arbitrary"),
                                             vmem_limit_bytes=VMEM_LIMIT),
        name="front_prompt",
    )(x, *wts)


def _front_sample(x, hist_c, hist_u, wts, pos0):
    nb, t, _ = x.shape
    ns = min(FRONT_TILE // t, nb)
    rows = ns * t
    xf = x.reshape(nb * t, D_MODEL)
    tok = lambda n: pl.BlockSpec((rows, n), lambda i: (i, 0))
    st = lambda r, n: pl.BlockSpec((ns, r, n), lambda i: (i, 0, 0))
    out_shape = ([jax.ShapeDtypeStruct((nb * t, D_MODEL), F32), jax.ShapeDtypeStruct((nb * t, D_ATTN), F32)]
                 + [jax.ShapeDtypeStruct((nb * t, D_KV), F32)] * 6
                 + [jax.ShapeDtypeStruct((nb * t, LANES), F32),
                    jax.ShapeDtypeStruct((nb, 8, D_CONV), F32), jax.ShapeDtypeStruct((nb, 16, D_POOL), F32)])
    out_specs = ([tok(D_MODEL), tok(D_ATTN)] + [tok(D_KV)] * 6 + [tok(LANES), st(8, D_CONV), st(16, D_POOL)])
    return pl.pallas_call(
        functools.partial(_front_sample_body, ns=ns, t=t, pos0=pos0),
        grid=(nb // ns,),
        in_specs=[tok(D_MODEL), st(CONV_W - 1, D_CONV), st(POOL_HIST, D_POOL)] + [_full(w.shape) for w in wts],
        out_specs=out_specs,
        out_shape=out_shape,
        scratch_shapes=[pltpu.VMEM((ns, 8 + t, D_CONV), F32), pltpu.VMEM((ns, 16 + t, D_POOL), F32)],
        compiler_params=pltpu.CompilerParams(dimension_semantics=("arbitrary",),
                                             vmem_limit_bytes=VMEM_LIMIT),
        name="front_sample",
    )(xf, hist_c, hist_u, *wts)


def _compress_body(k_ref, v_ref, wk_ref, wv_ref, ko_ref, vo_ref, kr_ref, vr_ref, *, npg, r):
    cols = r // LANES
    nvp = npg * cols
    for i in range(npg):
        for c in range(cols):
            row0 = (i * cols + c) * CMP_PITCH
            kr_ref[row0:row0 + LANES, :] = k_ref[i, :, c * LANES:(c + 1) * LANES].T
            vr_ref[row0:row0 + LANES, :] = v_ref[i, :, c * LANES:(c + 1) * LANES].T
    nj = LANES // CMP_BLOCK
    acc_k = jnp.zeros((nj * nvp, D_KV), F32)
    acc_v = jnp.zeros((nj * nvp, D_KV), F32)
    for l in range(CMP_BLOCK):
        rows_k = [kr_ref[pl.ds(j * CMP_BLOCK + l, nvp, stride=CMP_PITCH), :] for j in range(nj)]
        rows_v = [vr_ref[pl.ds(j * CMP_BLOCK + l, nvp, stride=CMP_PITCH), :] for j in range(nj)]
        acc_k = acc_k + _mm(jnp.concatenate(rows_k, axis=0), wk_ref[l])
        acc_v = acc_v + _mm(jnp.concatenate(rows_v, axis=0), wv_ref[l])
    for j in range(nj):
        ko_ref[pl.ds(j, nvp, stride=nj), :] = acc_k[j * nvp:(j + 1) * nvp]
        vo_ref[pl.ds(j, nvp, stride=nj), :] = acc_v[j * nvp:(j + 1) * nvp]


def _compress(k, v, wk, wv):
    nl, pages, _, r = k.shape
    npg = next(c for c in range(min(max(CMP_ROWS // r, 1), pages), 0, -1) if pages % c == 0)
    nblk = npg * r // CMP_BLOCK
    rspec = pl.BlockSpec((None, npg, D_KV, r), lambda l, i: (l, i, 0, 0))
    wspec = pl.BlockSpec((None, CMP_BLOCK, D_KV, D_KV), lambda l, i: (l, 0, 0, 0))
    ospec = pl.BlockSpec((None, nblk, D_KV), lambda l, i: (l, i, 0))
    return pl.pallas_call(
        functools.partial(_compress_body, npg=npg, r=r),
        grid=(nl, pages // npg),
        in_specs=[rspec, rspec, wspec, wspec],
        out_specs=[ospec, ospec],
        out_shape=[jax.ShapeDtypeStruct((nl, pages * r // CMP_BLOCK, D_KV), F32)] * 2,
        scratch_shapes=[pltpu.VMEM((npg * (r // LANES) * CMP_PITCH, D_KV), F32)] * 2,
        compiler_params=pltpu.CompilerParams(dimension_semantics=("arbitrary", "arbitrary"),
                                             vmem_limit_bytes=VMEM_LIMIT),
        name="compress",
    )(k, v, wk, wv)


def _make_qbd(q, tq):
    lo = _iota((tq, LANES), 1) < HEAD_DIM
    blocks = {}
    for jv in range(D_ATTN // LANES):
        a = q[:, LANES * jv:LANES * (jv + 1)] * ATTN_SCALE
        r = pltpu.roll(a, HEAD_DIM, 1)
        for half in range(2):
            h = 2 * jv + half
            kv, g = h // GROUP, h % GROUP
            if kv == 0:
                blocks[(g, kv)] = jnp.where(lo, a if half == 0 else r, 0.0)
            else:
                blocks[(g, kv)] = jnp.where(lo, 0.0, a if half == 1 else r)
    return jnp.concatenate([blocks[(g, kv)] for g in range(GROUP) for kv in range(N_KV)], axis=0).astype(BF16)


def _extract_heads(o, tq):
    lo = _iota((tq, LANES), 1) < HEAD_DIM
    cols = []
    for jv in range(D_ATTN // LANES):
        parts = []
        for half in range(2):
            h = 2 * jv + half
            kv, g = h // GROUP, h % GROUP
            blk = o[(g * N_KV + kv) * tq:(g * N_KV + kv + 1) * tq]
            if (half == 1) != (kv == 1):
                blk = pltpu.roll(blk, HEAD_DIM, 1)
            parts.append(blk)
        cols.append(jnp.where(lo, parts[0], parts[1]))
    return jnp.concatenate(cols, axis=1)


def _gate_rows(gate, c, tq):
    return jnp.concatenate([gate[:, c * N_HEADS + kv * GROUP + g:c * N_HEADS + kv * GROUP + g + 1]
                            for g in range(GROUP) for kv in range(N_KV)], axis=0)


def _softmax_rows(s, valid):
    s = jnp.where(valid, s, NEG_INF)
    m = jnp.max(s, axis=-1, keepdims=True)
    e = jnp.exp(s - m)
    p = e / jnp.sum(e, axis=-1, keepdims=True)
    return jnp.where(valid, p, 0.0)


def _select_blocks(imp, qpos2, blkf, n_valid_imp, k, axis):
    cur = (qpos2 // SEL_BLOCK).astype(F32)
    imp = jnp.where(blkf < n_valid_imp, imp, 0.0)
    score = jnp.where(blkf == cur, 2.0 * GROUP, jnp.where(blkf < cur, imp, -1.0))
    score = jnp.where(blkf < 0.0, -3.0, score)
    sel = jnp.zeros(score.shape, F32)
    for _ in range(k):
        m = jnp.max(score, axis=axis, keepdims=True)
        idx = jnp.min(jnp.where(score == m, blkf, 1e9), axis=axis, keepdims=True)
        hit = blkf == idx
        sel = jnp.where(hit, 1.0, sel)
        score = jnp.where(hit, -3.0, score)
    return sel


def _sum_groups(p, rows2):
    out = p[0:rows2]
    for g in range(1, GROUP):
        out = out + p[g * rows2:(g + 1) * rows2]
    return out


def _attn_prompt_body(q_ref, gate_ref, kc_ref, vc_ref, ks_ref, vs_ref, kw_ref, vw_ref,
                      tcol_ref, qs_ref, blkc_ref, trow_ref, e_ref, pos_ref, cpos_ref, o_ref,
                      kas_ref, vas_ref, kaw_ref, vaw_ref, *, tq, tk, seq, top_k):
    qt = pl.program_id(1)
    q0 = qt * tq
    rows = N_HEADS * tq
    rows2 = N_KV * tq
    nt = seq // tk

    @pl.when(qt == 0)
    def _():
        ones = jnp.ones((D_KV, tk), BF16)
        for j in range(nt):
            cs = slice(j * tk, (j + 1) * tk)
            for src, dst in ((ks_ref, kas_ref), (kw_ref, kaw_ref)):
                dst[j, 0:D_KV, :] = src[0, :, cs].astype(BF16)
                dst[j, D_KV:2 * D_KV, :] = pos_ref[j]
            for src, dst in ((vs_ref, vas_ref), (vw_ref, vaw_ref)):
                dst[j, 0:D_KV, :] = src[0, :, cs].astype(BF16)
                dst[j, D_KV:2 * D_KV, :] = ones

    qbd = _make_qbd(q_ref[0], tq)
    qaug = jnp.concatenate([qbd, qs_ref[...]], axis=1)
    qpos2 = q0 + tcol_ref[...]
    gate = gate_ref[0]

    ncb = seq // CMP_BLOCK
    pad = jnp.zeros((LANES - ncb, D_KV), F32)
    kc = jnp.concatenate([kc_ref[0], pad], axis=0)
    vc = jnp.concatenate([vc_ref[0], pad], axis=0)
    kcaug = jnp.concatenate([kc.T.astype(BF16), cpos_ref[...]], axis=0)
    s_c = jnp.dot(qaug, kcaug, preferred_element_type=F32).reshape(GROUP, rows2, LANES)
    blk_end = (_iota((1, LANES), 1) + 1) * CMP_BLOCK - 1
    p_c = _softmax_rows(s_c, (qpos2 >= blk_end)[None]).reshape(rows, LANES)
    o_acc = _gate_rows(gate, 0, tq) * _mm(p_c, vc)

    p4 = _sum_groups(p_c, rows2)
    imp_t = (p4 + pltpu.roll(p4, LANES - 1, 1)).T
    sel = _select_blocks(imp_t, q0 + trow_ref[...], blkc_ref[...], float(seq // SEL_BLOCK), top_k, 0).T
    sel = sel.astype(BF16)

    def branch(kaug_ref, vaug_ref, j_lo, j_hi, mask_fn):
        def step(j, carry):
            m, acc = carry
            s = jnp.dot(qaug, kaug_ref[j], preferred_element_type=F32).reshape(GROUP, rows2, tk)
            dist = qpos2 - (j * tk + _iota((1, tk), 1))
            s = jnp.where(mask_fn(j, dist)[None], s, NEG_INF)
            m_new = jnp.maximum(m, jnp.max(s, axis=2, keepdims=True))
            alpha = jnp.exp(m - m_new).reshape(rows, 1)
            p = jnp.exp(s - m_new).astype(BF16).reshape(rows, tk)
            pv = lax.dot_general(p, vaug_ref[j], (((1,), (1,)), ((), ())), preferred_element_type=F32)
            return m_new, alpha * acc + pv

        init = (jnp.full((GROUP, rows2, 1), NEG_INF, F32), jnp.zeros((rows, 2 * D_KV), F32))
        _, acc = lax.fori_loop(j_lo, j_hi, step, init)
        return acc[:, 0:D_KV] / acc[:, D_KV:2 * D_KV]

    j_hi = (q0 + tq - 1) // tk + 1

    def sel_mask(j, dist):
        picked = jnp.dot(sel, e_ref[j], preferred_element_type=F32)
        return (picked > 0.5) & (dist >= 0)

    o_acc = o_acc + _gate_rows(gate, 1, tq) * branch(kas_ref, vas_ref, 0, j_hi, sel_mask)

    def win_mask(j, dist):
        return (dist >= 0) & (dist <= WINDOW)

    j_lo = jnp.maximum(q0 - WINDOW, 0) // tk
    o_acc = o_acc + _gate_rows(gate, 2, tq) * branch(kaw_ref, vaw_ref, j_lo, j_hi, win_mask)
    o_ref[0] = _extract_heads(o_acc, tq)


def _row_consts(tq):
    r = jnp.arange(N_HEADS * tq)
    g, kv, t = r // (N_KV * tq), (r // tq) % N_KV, r % tq
    head = kv * GROUP + g
    slopes = jnp.exp2(-8.0 * (head + 1).astype(F32) / N_HEADS)
    return slopes[:, None], t.astype(jnp.int32)[:, None]


def _key_tiles(a, tk):
    return a.reshape(a.shape[0], -1, tk).transpose(1, 0, 2)


def _attn_prompt(q, gate, kcmp, vcmp, ks, vs, kw, vw):
    b, t, _ = q.shape
    tq = min(ATTN_TQ, t)
    tk = min(ATTN_TK, t)
    n_sb = t // SEL_BLOCK
    assert t // POS_SPLIT < 256
    slopes, tcol = _row_consts(tq)
    qs = jnp.zeros((N_HEADS * tq, LANES), F32).at[:, 0].set(POS_SPLIT * slopes[:, 0]).at[:, 1].set(slopes[:, 0])
    kpos = jnp.arange(t)
    pos = jnp.zeros((LANES, t), F32).at[0].set(kpos // POS_SPLIT).at[1].set(kpos % POS_SPLIT)
    lane = jnp.arange(LANES)
    blk = jnp.where((lane % 2 == 0) & (lane < 2 * n_sb), lane // 2, -1).astype(F32)
    e = (blk[:, None] == (kpos // SEL_BLOCK).astype(F32)[None, :])
    blk_end = (lane + 1) * CMP_BLOCK - 1
    cpos = jnp.zeros((LANES, LANES), F32).at[0].set(blk_end // POS_SPLIT).at[1].set(blk_end % POS_SPLIT)
    tcol2 = tcol[:N_KV * tq]
    consts = (tcol2, qs.astype(BF16), blk[:, None], tcol2.reshape(1, -1),
              _key_tiles(e.astype(BF16), tk), _key_tiles(pos.astype(BF16), tk), cpos.astype(BF16))
    tile = lambda n: pl.BlockSpec((1, tq, n), lambda i, j: (i, j, 0))
    seq = lambda r, n: pl.BlockSpec((1, r, n), lambda i, j: (i, 0, 0))
    return pl.pallas_call(
        functools.partial(_attn_prompt_body, tq=tq, tk=tk, seq=t, top_k=min(TOP_K, n_sb)),
        grid=(b, t // tq),
        in_specs=[tile(D_ATTN), tile(LANES), seq(t // CMP_BLOCK, D_KV), seq(t // CMP_BLOCK, D_KV),
                  seq(D_KV, t), seq(D_KV, t), seq(D_KV, t), seq(D_KV, t)] + [_full(c.shape) for c in consts],
        out_specs=tile(D_ATTN),
        out_shape=jax.ShapeDtypeStruct((b, t, D_ATTN), F32),
        scratch_shapes=[pltpu.VMEM((t // tk, 2 * D_KV, tk), BF16)] * 4,
        compiler_params=pltpu.CompilerParams(dimension_semantics=("arbitrary", "arbitrary"),
                                             vmem_limit_bytes=VMEM_LIMIT),
        name="attn_prompt",
    )(q, gate, kcmp, vcmp, ks, vs, kw, vw, *consts)


def _attn_sample_body(pt_ref, q_ref, gate_ref, ksn_ref, vsn_ref, kwn_ref, vwn_ref, kwin_ref, vwin_ref,
                      kcp_ref, vcp_ref, cvalid_ref, cend_ref, slope_ref, tcol_ref, blkf_ref, e_ref, *rest,
                      t, n_pages, page, past, w_buf, top_k, nbs):
    ks_pages = rest[:nbs * n_pages]
    vs_pages = rest[nbs * n_pages:2 * nbs * n_pages]
    o_ref, kwo_ref, vwo_ref = rest[2 * nbs * n_pages:]
    b0 = pl.program_id(0) * nbs
    rows = N_HEADS * t
    rows2 = N_KV * t
    seqs = range(nbs)

    def stack(xs):
        return jnp.concatenate(xs, axis=0)

    def part(x, bb, n):
        return x[bb * n:(bb + 1) * n]

    def pad_rows(x):
        if x.shape[0] == LANES:
            return x
        return jnp.concatenate([x, jnp.zeros((LANES - x.shape[0], D_KV), F32)], axis=0)

    slope = stack([slope_ref[...]] * nbs)
    qpos = stack([past + tcol_ref[...]] * nbs)
    qpos2 = stack([past + tcol_ref[0:rows2, :]] * nbs)
    qbd = [_make_qbd(q_ref[bb], t) for bb in seqs]
    gates = [gate_ref[bb] for bb in seqs]

    def gate_rows(c):
        return stack([_gate_rows(gates[bb], c, t) for bb in seqs])

    kc = [pad_rows(stack([kcp_ref[pt_ref[b0 + bb, p] // 2] for p in range(n_pages)])) for bb in seqs]
    vc = [pad_rows(stack([vcp_ref[pt_ref[b0 + bb, p] // 2] for p in range(n_pages)])) for bb in seqs]
    dist_c = qpos - cend_ref[...]
    s_c = stack([_mm_nt(qbd[bb], kc[bb]) for bb in seqs]) - slope * dist_c.astype(F32)
    cvalid = stack([jnp.broadcast_to(cvalid_ref[bb], (rows, LANES)) for bb in seqs])
    p_c = _softmax_rows(s_c, (cvalid > 0.5) & (dist_c >= 0))
    o_acc = gate_rows(0) * stack([_mm(part(p_c, bb, rows), vc[bb]) for bb in seqs])

    p4 = stack([_sum_groups(part(p_c, bb, rows), rows2) for bb in seqs])
    x1 = p4 + pltpu.roll(p4, LANES - 1, 1)
    imp = x1 + pltpu.roll(x1, LANES - 4, 1)
    sel = _select_blocks(imp, qpos2, blkf_ref[...], float(past // SEL_BLOCK), top_k, 1)
    sel4 = stack([part(sel, bb, rows2) for bb in seqs for _ in range(GROUP)])

    s_s = stack([jnp.concatenate([_mm(qbd[bb], ks_pages[bb * n_pages + p][...]) for p in range(n_pages)]
                                 + [_mm_nt(qbd[bb], pad_rows(ksn_ref[bb]))], axis=1) for bb in seqs])
    dist_s = qpos - _iota((1, past + LANES), 1)
    picked = _mm(sel4, e_ref[...])
    p_s = _softmax_rows(s_s - slope * dist_s.astype(F32), (picked > 0.5) & (dist_s >= 0))
    o_s = []
    for bb in seqs:
        pb = part(p_s, bb, rows)
        acc = _mm(pb[:, past:], pad_rows(vsn_ref[bb]))
        for p in range(n_pages):
            acc = acc + _mm_nt(pb[:, p * page:(p + 1) * page], vs_pages[bb * n_pages + p][...])
        o_s.append(acc)
    o_acc = o_acc + gate_rows(1) * stack(o_s)

    kwin = [kwin_ref[bb] for bb in seqs]
    vwin = [vwin_ref[bb] for bb in seqs]
    s_w = stack([jnp.concatenate([_mm(qbd[bb], kwin[bb]), _mm_nt(qbd[bb], pad_rows(kwn_ref[bb]))], axis=1)
                 for bb in seqs])
    dist_w = qpos - (past - w_buf + _iota((1, w_buf + LANES), 1))
    p_w = _softmax_rows(s_w - slope * dist_w.astype(F32), (dist_w >= 0) & (dist_w <= WINDOW))
    o_w = stack([_mm_nt(part(p_w, bb, rows)[:, :w_buf], vwin[bb])
                 + _mm(part(p_w, bb, rows)[:, w_buf:], pad_rows(vwn_ref[bb])) for bb in seqs])
    o_acc = o_acc + gate_rows(2) * o_w

    is_new = _iota((D_KV, LANES), 1) >= LANES - t

    def shift_in(win, new):
        new_t = jnp.concatenate([jnp.zeros((LANES - t, D_KV), F32), new], axis=0).T
        rolled = pltpu.roll(win, w_buf - t, 1)
        last = jnp.where(is_new, new_t, rolled[:, w_buf - LANES:])
        return jnp.concatenate([rolled[:, :w_buf - LANES], last], axis=1)

    for bb in seqs:
        o_ref[bb] = _extract_heads(part(o_acc, bb, rows), t)
        kwo_ref[bb] = shift_in(kwin[bb], kwn_ref[bb])
        vwo_ref[bb] = shift_in(vwin[bb], vwn_ref[bb])


def _page_map(l, nbs, bb, p, i, pt):
    return (l, pt[i * nbs + bb, p], 0, 0)


def _attn_sample(l, page_table, q, gate, ks_new, vs_new, kw_new, vw_new, kwin, vwin, kcp, vcp,
                 cache_ks, cache_vs):
    nb, t, _ = q.shape
    n_pages = page_table.shape[1]
    page = cache_ks.shape[3]
    past = n_pages * page
    w_buf = kwin.shape[3]
    n_sb = -(-(past + t) // SEL_BLOCK)
    assert t == SUBLANES and n_pages * 8 <= LANES and page == LANES and past % SEL_BLOCK == 0
    assert w_buf % LANES == 0
    slopes, tcol = _row_consts(t)
    lane = jnp.arange(LANES)
    pg, r = lane // 8, lane % 8
    in_range = pg < n_pages
    cend = jnp.where(in_range, (4 * pg + r % 4 + 1) * CMP_BLOCK - 1, 1 << 30).astype(jnp.int32)[None, :]
    par = page_table % 2
    par_l = jnp.take(par, jnp.minimum(pg, n_pages - 1), axis=1)
    cvalid = ((r[None, :] // 4 == par_l) & in_range[None, :]).astype(F32)[:, None, :]
    blk = jnp.where(in_range & (r % 2 == 0) & (r < 4), 2 * pg + r // 2, -1)
    blk = jnp.where(lane == 1, n_sb - 1, blk)
    blkf = blk.astype(F32)[None, :]
    e = (blkf.reshape(-1, 1) == (jnp.arange(past + LANES) // SEL_BLOCK).astype(F32)[None, :]).astype(BF16)
    nbs = next(c for c in range(min(DEC_STEP, nb), 0, -1) if nb % c == 0)
    tok = lambda n: pl.BlockSpec((nbs, t, n), lambda i, pt: (i, 0, 0))
    win = pl.BlockSpec((None, nbs, D_KV, w_buf), lambda i, pt: (l, i, 0, 0))
    win_out = pl.BlockSpec((nbs, D_KV, w_buf), lambda i, pt: (i, 0, 0))
    res = pl.BlockSpec((None,) + kcp.shape[1:], lambda i, pt: (l, 0, 0, 0))
    const = lambda a: pl.BlockSpec(a.shape, lambda i, pt: (0,) * a.ndim)
    pages = [pl.BlockSpec((None, None, D_KV, page), functools.partial(_page_map, l, nbs, bb, p))
             for bb in range(nbs) for p in range(n_pages)]
    grid_spec = pltpu.PrefetchScalarGridSpec(
        num_scalar_prefetch=1,
        grid=(nb // nbs,),
        in_specs=[tok(D_ATTN), tok(LANES), tok(D_KV), tok(D_KV), tok(D_KV), tok(D_KV), win, win, res, res,
                  pl.BlockSpec((nbs, 1, LANES), lambda i, pt: (i, 0, 0)),
                  const(cend), const(slopes), const(tcol), const(blkf), const(e)] + pages + pages,
        out_specs=[tok(D_ATTN), win_out, win_out],
    )
    return pl.pallas_call(
        functools.partial(_attn_sample_body, t=t, n_pages=n_pages, page=page, past=past, w_buf=w_buf,
                          top_k=min(TOP_K, n_sb), nbs=nbs),
        grid_spec=grid_spec,
        out_shape=[jax.ShapeDtypeStruct((nb, t, D_ATTN), F32),
                   jax.ShapeDtypeStruct((nb, D_KV, w_buf), F32), jax.ShapeDtypeStruct((nb, D_KV, w_buf), F32)],
        compiler_params=pltpu.CompilerParams(dimension_semantics=("arbitrary",), vmem_limit_bytes=VMEM_LIMIT),
        name="attn_sample",
    )(page_table, q, gate, ks_new, vs_new, kw_new, vw_new, kwin, vwin, kcp, vcp, cvalid,
      cend, slopes, tcol, blkf, e, *([cache_ks] * (nbs * n_pages)), *([cache_vs] * (nbs * n_pages)))


def _back_head(x, o, pab, wg2_ref, woa_ref, wmix_ref, ln1g_ref, ln1b_ref):
    y_c = _mm(o, woa_ref[...])
    g2 = jax.nn.sigmoid(_mm(x, wg2_ref[...]))
    mixed = pab + g2 * y_c
    return _layer_norm(ALPHA * x + _mm(mixed, wmix_ref[...]), ln1g_ref[...], ln1b_ref[...])


def _back_prompt_body(x_ref, o_ref, pab_ref, wg2_ref, woa_ref, wmix_ref, ln1g_ref, ln1b_ref,
                      wup_ref, fcw_ref, fcb_ref, wdn_ref, ln2g_ref, ln2b_ref,
                      y_ref, fst_ref, ext_ref, *, tt):
    j = pl.program_id(1)

    @pl.when(j == 0)
    def _():
        ext_ref[0:8, :] = jnp.zeros((8, D_FF), F32)

    h = _back_head(x_ref[0], o_ref[0], pab_ref[0], wg2_ref, woa_ref, wmix_ref, ln1g_ref, ln1b_ref)
    hb = h.astype(BF16)
    cf = D_FF // FF_CHUNKS
    f = jnp.zeros((tt, D_MODEL), F32)
    for c in range(FF_CHUNKS):
        cs = slice(c * cf, (c + 1) * cf)
        u = _mm(hb, wup_ref[:, c * cf:(c + 1) * cf])
        gt = _mm(hb, wup_ref[:, D_FF + c * cf:D_FF + (c + 1) * cf])
        ext_ref[8:8 + tt, cs] = u
        conv = fcb_ref[:, cs] + fcw_ref[0:1, cs] * ext_ref[6:6 + tt, cs]
        conv = conv + fcw_ref[1:2, cs] * ext_ref[7:7 + tt, cs]
        conv = conv + fcw_ref[2:3, cs] * u
        f = f + _mm(jax.nn.gelu(conv) * gt, wdn_ref[c * cf:(c + 1) * cf, :])
    y_ref[0] = _layer_norm(ALPHA * h + f, ln2g_ref[...], ln2b_ref[...])
    last = ext_ref[tt:tt + 8, :]
    fst_ref[0] = last
    ext_ref[0:8, :] = last


def _back_sample_body(x_ref, o_ref, pab_ref, hf_ref, wg2_ref, woa_ref, wmix_ref, ln1g_ref, ln1b_ref,
                      wup_ref, fcw_ref, fcb_ref, wdn_ref, ln2g_ref, ln2b_ref,
                      y_ref, fst_ref, ext_ref, *, ns, t):
    h = _back_head(x_ref[...], o_ref[...], pab_ref[...], wg2_ref, woa_ref, wmix_ref, ln1g_ref, ln1b_ref)
    hb = h.astype(BF16)
    cf = D_FF // FF_CHUNKS
    ext_ref[:, 6:8, :] = hf_ref[...]
    f = jnp.zeros((ns * t, D_MODEL), F32)
    for c in range(FF_CHUNKS):
        cs = slice(c * cf, (c + 1) * cf)
        u = _mm(hb, wup_ref[:, c * cf:(c + 1) * cf])
        gt = _mm(hb, wup_ref[:, D_FF + c * cf:D_FF + (c + 1) * cf])
        ext_ref[:, 8:8 + t, cs] = u.reshape(ns, t, cf)
        fcw = fcw_ref[:, cs]
        conv = fcb_ref[:, cs] + fcw[0:1] * ext_ref[:, 6:6 + t, cs]
        conv = conv + fcw[1:2] * ext_ref[:, 7:7 + t, cs]
        conv = conv + fcw[2:3] * ext_ref[:, 8:8 + t, cs]
        f = f + _mm(jax.nn.gelu(conv).reshape(ns * t, cf) * gt, wdn_ref[c * cf:(c + 1) * cf, :])
    y_ref[...] = _layer_norm(ALPHA * h + f, ln2g_ref[...], ln2b_ref[...])
    fst_ref[...] = ext_ref[:, t:t + 8, :]


def _back_prompt(x, o, pab, wts):
    b, t, _ = x.shape
    tt = min(FRONT_TILE, t)
    tok = lambda n: pl.BlockSpec((1, tt, n), lambda i, j: (i, j, 0))
    return pl.pallas_call(
        functools.partial(_back_prompt_body, tt=tt),
        grid=(b, t // tt),
        in_specs=[tok(D_MODEL), tok(D_ATTN), tok(D_MODEL)] + [_full(w.shape) for w in wts],
        out_specs=[tok(D_MODEL), pl.BlockSpec((1, 8, D_FF), lambda i, j: (i, 0, 0))],
        out_shape=[jax.ShapeDtypeStruct((b, t, D_MODEL), F32), jax.ShapeDtypeStruct((b, 8, D_FF), F32)],
        scratch_shapes=[pltpu.VMEM((8 + tt, D_FF), F32)],
        compiler_params=pltpu.CompilerParams(dimension_semantics=("arbitrary", "arbitrary"),
                                             vmem_limit_bytes=VMEM_LIMIT),
        name="back_prompt",
    )(x, o, pab, *wts)


def _back_sample(x, o, pab, hist_f, wts, t):
    n = x.shape[0]
    nb = n // t
    ns = min(FRONT_TILE // t, nb)
    rows = ns * t
    tok = lambda c: pl.BlockSpec((rows, c), lambda i: (i, 0))
    return pl.pallas_call(
        functools.partial(_back_sample_body, ns=ns, t=t),
        grid=(nb // ns,),
        in_specs=[tok(D_MODEL), tok(D_ATTN), tok(D_MODEL),
                  pl.BlockSpec((ns, CONV_W - 1, D_FF), lambda i: (i, 0, 0))] + [_full(w.shape) for w in wts],
        out_specs=[tok(D_MODEL), pl.BlockSpec((ns, 8, D_FF), lambda i: (i, 0, 0))],
        out_shape=[jax.ShapeDtypeStruct((n, D_MODEL), F32), jax.ShapeDtypeStruct((nb, 8, D_FF), F32)],
        scratch_shapes=[pltpu.VMEM((ns, 8 + t, D_FF), F32)],
        compiler_params=pltpu.CompilerParams(dimension_semantics=("arbitrary",), vmem_limit_bytes=VMEM_LIMIT),
        name="back_sample",
    )(x, o, pab, hist_f, *wts)


def _block_diag2(w):
    z = jnp.zeros_like(w)
    return jnp.concatenate([jnp.concatenate([w, z], axis=-1), jnp.concatenate([z, w], axis=-1)], axis=-2)


def kernel(x_prompt, x_sample, cache_k_cmp, cache_v_cmp, cache_k_sel, cache_v_sel, state_k_win, state_v_win, state_conv, state_pool, state_ffn_conv, page_table, w_in, conv_w, conv_b, w_out_conv, pool_w, pool_scale, w_cmp_k, w_cmp_v, w_out_attn, w_mix_out, ln1_g, ln1_b, w_ffn_up, ffn_conv_w, ffn_conv_b, w_ffn_down, ln2_g, ln2_b):
    depth = w_in.shape[0]
    bp, seq, _ = x_prompt.shape
    nb, t, _ = x_sample.shape
    n_phys, page = cache_k_cmp.shape[1], cache_k_cmp.shape[2]
    past = page_table.shape[1] * page
    w_buf = state_k_win.shape[2]

    w_front = jnp.concatenate(
        [w_in[:, :, :IN_GATES], jnp.zeros((depth, D_MODEL, C_G01 - IN_GATES), F32),
         w_in[:, :, IN_GATES:IN_GATES + 2 * D_MODEL]], axis=2).astype(BF16)
    w_g2 = w_in[:, :, IN_GATES + 2 * D_MODEL:].astype(BF16)
    ng = pool_w.shape[1]
    pool_bd = jnp.zeros((depth, D_POOL, D_MODEL), F32)
    for g in range(ng):
        pool_bd = pool_bd.at[:, g * POOL_GROUP:(g + 1) * POOL_GROUP,
                             g * (D_MODEL // ng):(g + 1) * (D_MODEL // ng)].set(pool_w[:, g])
    pool_bd = pool_bd.astype(BF16)
    wck = _block_diag2(w_cmp_k).astype(BF16)
    wcv = _block_diag2(w_cmp_v).astype(BF16)
    w_oc, w_oa, w_mix = w_out_conv.astype(BF16), w_out_attn.astype(BF16), w_mix_out.astype(BF16)
    w_up, w_dn = w_ffn_up.astype(BF16), w_ffn_down.astype(BF16)
    row = lambda a: a[:, None, :]

    keys_minor = lambda a: jnp.transpose(a, (0, 1, 3, 4, 2)).reshape(a.shape[0], a.shape[1], D_KV, a.shape[2])
    tokens_major = lambda a: jnp.transpose(a.reshape(a.shape[0], N_KV, HEAD_DIM, a.shape[2]), (0, 3, 1, 2))
    ck_s, cv_s = keys_minor(cache_k_sel), keys_minor(cache_v_sel)
    kwin_all, vwin_all = keys_minor(state_k_win), keys_minor(state_v_win)
    kcp, vcp = _compress(keys_minor(cache_k_cmp), keys_minor(cache_v_cmp), wck, wcv)
    kcp = kcp.reshape(depth, n_phys // 2, 8, D_KV)
    vcp = vcp.reshape(depth, n_phys // 2, 8, D_KV)

    y_p = x_prompt
    y_s = x_sample.reshape(nb * t, D_MODEL)
    st_p, st_s = [], []
    for l in range(depth):
        fw = (w_front[l], conv_w[l], row(conv_b)[l], w_oc[l], pool_bd[l], row(pool_scale)[l])
        bw = (w_g2[l], w_oa[l], w_mix[l], row(ln1_g)[l], row(ln1_b)[l], w_up[l], ffn_conv_w[l],
              row(ffn_conv_b)[l], w_dn[l], row(ln2_g)[l], row(ln2_b)[l])

        pab, q, kc, vc, ks, vs, kw, vw, gate, cst, pst = _front_prompt(y_p, fw)
        kcmp, vcmp = _compress(kc[None], vc[None], wck[l:l + 1], wcv[l:l + 1])
        o = _attn_prompt(q, gate, kcmp.reshape(bp, seq // CMP_BLOCK, D_KV),
                         vcmp.reshape(bp, seq // CMP_BLOCK, D_KV), ks, vs, kw, vw)
        y_p, fst = _back_prompt(y_p, o, pab, bw)
        st_p.append(tuple(tokens_major(a) for a in (kc, vc, ks, vs, kw[:, :, seq - w_buf:], vw[:, :, seq - w_buf:]))
                    + (cst[:, 8 - (CONV_W - 1):], pst[:, 16 - POOL_HIST:], fst[:, 8 - (CONV_W - 1):]))

        pab, q, kc, vc, ks, vs, kw, vw, gate, cst, pst = _front_sample(
            y_s.reshape(nb, t, D_MODEL), state_conv[l], state_pool[l], fw, past)
        r3 = lambda a: a.reshape(nb, t, a.shape[-1])
        o, kwin, vwin = _attn_sample(l, page_table, r3(q), r3(gate), r3(ks), r3(vs), r3(kw), r3(vw),
                                     kwin_all, vwin_all, kcp, vcp, ck_s, cv_s)
        y_s, fst = _back_sample(y_s, o.reshape(nb * t, D_ATTN), pab, state_ffn_conv[l], bw, t)
        kv5 = lambda a: a.reshape(nb, t, N_KV, HEAD_DIM)
        st_s.append((kv5(kc), kv5(vc), kv5(ks), kv5(vs), tokens_major(kwin), tokens_major(vwin),
                     cst[:, 8 - (CONV_W - 1):], pst[:, 16 - POOL_HIST:], fst[:, 8 - (CONV_W - 1):]))

    kc_p, vc_p, ks_p, vs_p, kw_p, vw_p, conv_p, pool_p, ffn_p = [jnp.stack(a) for a in zip(*st_p)]
    kc_s, vc_s, ks_s, vs_s, kw_s, vw_s, conv_s, pool_s, ffn_s = [jnp.stack(a) for a in zip(*st_s)]
    return (y_p, y_s.reshape(nb, t, D_MODEL), kc_p, kc_s, vc_p, vc_s, ks_p, ks_s, vs_p, vs_s,
            kw_p, kw_s, vw_p, vw_s, conv_p, conv_s, pool_p, pool_s, ffn_p, ffn_s)
```

```python
import functools

import jax
import jax.numpy as jnp
from jax import lax
from jax.experimental import pallas as pl
from jax.experimental.pallas import tpu as pltpu

F32 = jnp.float32
BF16 = jnp.bfloat16

D_MODEL = 1024
DEPTH = 4
D_CONV = 256
CONV_W = 3
D_POOL = 256
POOL_WINDOWS = (2, 4, 8, 16)
POOL_GROUP = D_POOL // len(POOL_WINDOWS)
POOL_HIST = 15
N_HEADS = 8
N_KV = 2
GROUP = N_HEADS // N_KV
HEAD_DIM = 64
D_ATTN = N_HEADS * HEAD_DIM
D_KV = N_KV * HEAD_DIM
CMP_BLOCK = 32
SEL_BLOCK = 64
TOP_K = 8
WINDOW = 512
N_BRANCH = 3
D_FF = 2816
ALPHA = (2.0 * DEPTH) ** 0.25
LN_EPS = 1e-5
ATTN_SCALE = HEAD_DIM ** -0.5
NEG_INF = -1e30
POS_SPLIT = 64

LANES = 128
SUBLANES = 8
VMEM_LIMIT = 56 * 1024 * 1024

C_AB, C_AC, C_AH, C_U, C_Q = 0, D_CONV, 2 * D_CONV, 3 * D_CONV, 3 * D_CONV + D_POOL
C_KV = C_Q + D_ATTN
C_NSA = C_KV + 6 * D_KV
C_G01 = C_NSA + LANES
W_FRONT = C_G01 + 2 * D_MODEL
IN_MAIN = C_NSA
IN_GATES = IN_MAIN + N_BRANCH * N_HEADS

FRONT_TILE = 256
ATTN_TQ = 128
ATTN_TK = 256
CMP_ROWS = 8192
CMP_PITCH = LANES + SUBLANES
DEC_STEP = 4
FF_CHUNKS = 2


def _mm(a, b):
    return jnp.dot(a.astype(BF16), b.astype(BF16), preferred_element_type=F32)


def _mm_nt(a, b):
    return lax.dot_general(a.astype(BF16), b.astype(BF16), (((1,), (1,)), ((), ())),
                           preferred_element_type=F32)


def _layer_norm(x, g, b):
    mu = jnp.mean(x, axis=-1, keepdims=True)
    var = jnp.mean(jnp.square(x - mu), axis=-1, keepdims=True)
    return (x - mu) * lax.rsqrt(var + LN_EPS) * g + b


def _iota(shape, axis):
    return lax.broadcasted_iota(jnp.int32, shape, axis)


def _pool_delta(u, shifted, pos):
    nd = u.ndim
    acc = u
    sums = {}
    for j in range(1, POOL_WINDOWS[-1]):
        acc = acc + shifted(j)
        if j + 1 in POOL_WINDOWS:
            sums[j + 1] = acc
    grp = _iota((1,) * (nd - 1) + (D_POOL,), nd - 1) // POOL_GROUP
    s = sums[POOL_WINDOWS[-1]]
    win = jnp.full(grp.shape, POOL_WINDOWS[-1], jnp.int32)
    for g in range(len(POOL_WINDOWS) - 2, -1, -1):
        s = jnp.where(grp == g, sums[POOL_WINDOWS[g]], s)
        win = jnp.where(grp == g, POOL_WINDOWS[g], win)
    cnt = jnp.minimum(win, pos + 1).astype(F32)
    return s / cnt - u


def _front_tail(z, conv, d, woc_ref, pbd_ref, psc_ref, outs, kv_transposed):
    pab_ref, q_ref, kv_refs, gate_ref = outs
    y_a = _mm(z[:, C_AB:C_AB + D_CONV] * conv, woc_ref[...])
    y_b = _mm(d, pbd_ref[...]) * psc_ref[...]
    g0 = jax.nn.sigmoid(z[:, C_G01:C_G01 + D_MODEL])
    g1 = jax.nn.sigmoid(z[:, C_G01 + D_MODEL:C_G01 + 2 * D_MODEL])
    pab_ref[...] = (g0 * y_a + g1 * y_b).reshape(pab_ref.shape)
    q_ref[...] = z[:, C_Q:C_Q + D_ATTN].reshape(q_ref.shape)
    for i, r in enumerate(kv_refs):
        kv = z[:, C_KV + i * D_KV:C_KV + (i + 1) * D_KV]
        r[...] = (kv.T if kv_transposed else kv).reshape(r.shape)
    gate_ref[...] = jax.nn.sigmoid(z[:, C_NSA:C_NSA + LANES]).reshape(gate_ref.shape)


def _front_prompt_body(x_ref, w_ref, cw_ref, cb_ref, woc_ref, pbd_ref, psc_ref,
                       pab_ref, q_ref, kc_ref, vc_ref, ks_ref, vs_ref, kw_ref, vw_ref, gate_ref,
                       cst_ref, pst_ref, extc_ref, extu_ref, *, tt):
    j = pl.program_id(1)

    @pl.when(j == 0)
    def _():
        extc_ref[0:8, :] = jnp.zeros((8, D_CONV), F32)
        extu_ref[0:16, :] = jnp.zeros((16, D_POOL), F32)

    z = _mm(x_ref[0], w_ref[...])
    ch = z[:, C_AC:C_AC + D_CONV] * z[:, C_AH:C_AH + D_CONV]
    extc_ref[8:8 + tt, :] = ch
    conv = cb_ref[...] + cw_ref[0:1, :] * extc_ref[6:6 + tt, :]
    conv = conv + cw_ref[1:2, :] * extc_ref[7:7 + tt, :]
    conv = conv + cw_ref[2:3, :] * ch
    u = z[:, C_U:C_U + D_POOL]
    extu_ref[16:16 + tt, :] = u
    pos = j * tt + _iota((tt, 1), 0)
    d = _pool_delta(u, lambda s: extu_ref[16 - s:16 - s + tt, :], pos)
    _front_tail(z, conv, d, woc_ref, pbd_ref, psc_ref,
                (pab_ref, q_ref, (kc_ref, vc_ref, ks_ref, vs_ref, kw_ref, vw_ref), gate_ref), True)
    last_c = extc_ref[tt:tt + 8, :]
    last_u = extu_ref[tt:tt + 16, :]
    cst_ref[0] = last_c
    pst_ref[0] = last_u
    extc_ref[0:8, :] = last_c
    extu_ref[0:16, :] = last_u


def _front_sample_body(x_ref, hc_ref, hu_ref, w_ref, cw_ref, cb_ref, woc_ref, pbd_ref, psc_ref,
                       pab_ref, q_ref, kc_ref, vc_ref, ks_ref, vs_ref, kw_ref, vw_ref, gate_ref,
                       cst_ref, pst_ref, extc_ref, extu_ref, *, ns, t, pos0):
    z = _mm(x_ref[...], w_ref[...])
    ch = z[:, C_AC:C_AC + D_CONV] * z[:, C_AH:C_AH + D_CONV]
    extc_ref[:, 6:8, :] = hc_ref[...]
    extc_ref[:, 8:8 + t, :] = ch.reshape(ns, t, D_CONV)
    cw = cw_ref[...]
    conv = cb_ref[...] + cw[0:1] * extc_ref[:, 6:6 + t, :]
    conv = conv + cw[1:2] * extc_ref[:, 7:7 + t, :]
    conv = conv + cw[2:3] * extc_ref[:, 8:8 + t, :]
    u = z[:, C_U:C_U + D_POOL]
    extu_ref[:, 1:16, :] = hu_ref[...]
    extu_ref[:, 16:16 + t, :] = u.reshape(ns, t, D_POOL)
    pos = pos0 + _iota((1, t, 1), 1)
    d = _pool_delta(extu_ref[:, 16:16 + t, :], lambda s: extu_ref[:, 16 - s:16 - s + t, :], pos)
    _front_tail(z, conv.reshape(ns * t, D_CONV), d.reshape(ns * t, D_POOL), woc_ref, pbd_ref, psc_ref,
                (pab_ref, q_ref, (kc_ref, vc_ref, ks_ref, vs_ref, kw_ref, vw_ref), gate_ref), False)
    cst_ref[...] = extc_ref[:, t:t + 8, :]
    pst_ref[...] = extu_ref[:, t:t + 16, :]


def _full(shape):
    n = len(shape)
    return pl.BlockSpec(shape, lambda *_: (0,) * n, pipeline_mode=pl.Buffered(1))


def _front_prompt(x, wts):
    b, t, _ = x.shape
    tt = min(FRONT_TILE, t)
    tok = lambda n: pl.BlockSpec((1, tt, n), lambda i, j: (i, j, 0))
    st = lambda r, n: pl.BlockSpec((1, r, n), lambda i, j: (i, 0, 0))
    kvt = pl.BlockSpec((1, D_KV, tt), lambda i, j: (i, 0, j))
    out_shape = ([jax.ShapeDtypeStruct((b, t, D_MODEL), F32), jax.ShapeDtypeStruct((b, t, D_ATTN), F32)]
                 + [jax.ShapeDtypeStruct((b, D_KV, t), F32)] * 6
                 + [jax.ShapeDtypeStruct((b, t, LANES), F32),
                    jax.ShapeDtypeStruct((b, 8, D_CONV), F32), jax.ShapeDtypeStruct((b, 16, D_POOL), F32)])
    out_specs = ([tok(D_MODEL), tok(D_ATTN)] + [kvt] * 6 + [tok(LANES), st(8, D_CONV), st(16, D_POOL)])
    return pl.pallas_call(
        functools.partial(_front_prompt_body, tt=tt),
        grid=(b, t // tt),
        in_specs=[tok(D_MODEL)] + [_full(w.shape) for w in wts],
        out_specs=out_specs,
        out_shape=out_shape,
        scratch_shapes=[pltpu.VMEM((8 + tt, D_CONV), F32), pltpu.VMEM((16 + tt, D_POOL), F32)],
        compiler_params=pltpu.CompilerParams(dimension_semantics=("arbitrary", "arbitrary"),
                                             vmem_limit_bytes=VMEM_LIMIT),
        name="front_prompt",
    )(x, *wts)


def _front_sample(x, hist_c, hist_u, wts, pos0):
    nb, t, _ = x.shape
    ns = min(FRONT_TILE // t, nb)
    rows = ns * t
    xf = x.reshape(nb * t, D_MODEL)
    tok = lambda n: pl.BlockSpec((rows, n), lambda i: (i, 0))
    st = lambda r, n: pl.BlockSpec((ns, r, n), lambda i: (i, 0, 0))
    out_shape = ([jax.ShapeDtypeStruct((nb * t, D_MODEL), F32), jax.ShapeDtypeStruct((nb * t, D_ATTN), F32)]
                 + [jax.ShapeDtypeStruct((nb * t, D_KV), F32)] * 6
                 + [jax.ShapeDtypeStruct((nb * t, LANES), F32),
                    jax.ShapeDtypeStruct((nb, 8, D_CONV), F32), jax.ShapeDtypeStruct((nb, 16, D_POOL), F32)])
    out_specs = ([tok(D_MODEL), tok(D_ATTN)] + [tok(D_KV)] * 6 + [tok(LANES), st(8, D_CONV), st(16, D_POOL)])
    return pl.pallas_call(
        functools.partial(_front_sample_body, ns=ns, t=t, pos0=pos0),
        grid=(nb // ns,),
        in_specs=[tok(D_MODEL), st(CONV_W - 1, D_CONV), st(POOL_HIST, D_POOL)] + [_full(w.shape) for w in wts],
        out_specs=out_specs,
        out_shape=out_shape,
        scratch_shapes=[pltpu.VMEM((ns, 8 + t, D_CONV), F32), pltpu.VMEM((ns, 16 + t, D_POOL), F32)],
        compiler_params=pltpu.CompilerParams(dimension_semantics=("arbitrary",),
                                             vmem_limit_bytes=VMEM_LIMIT),
        name="front_sample",
    )(xf, hist_c, hist_u, *wts)


def _compress_body(k_ref, v_ref, wk_ref, wv_ref, ko_ref, vo_ref, kr_ref, vr_ref, *, npg, r):
    cols = r // LANES
    nvp = npg * cols
    for i in range(npg):
        for c in range(cols):
            row0 = (i * cols + c) * CMP_PITCH
            kr_ref[row0:row0 + LANES, :] = k_ref[i, :, c * LANES:(c + 1) * LANES].astype(BF16).T.astype(F32)
            vr_ref[row0:row0 + LANES, :] = v_ref[i, :, c * LANES:(c + 1) * LANES].astype(BF16).T.astype(F32)
    nj = LANES // CMP_BLOCK
    acc_k = jnp.zeros((nj * nvp, D_KV), F32)
    acc_v = jnp.zeros((nj * nvp, D_KV), F32)
    for l in range(CMP_BLOCK):
        rows_k = [kr_ref[pl.ds(j * CMP_BLOCK + l, nvp, stride=CMP_PITCH), :] for j in range(nj)]
        rows_v = [vr_ref[pl.ds(j * CMP_BLOCK + l, nvp, stride=CMP_PITCH), :] for j in range(nj)]
        acc_k = acc_k + _mm(jnp.concatenate(rows_k, axis=0), wk_ref[l])
        acc_v = acc_v + _mm(jnp.concatenate(rows_v, axis=0), wv_ref[l])
    for j in range(nj):
        ko_ref[pl.ds(j, nvp, stride=nj), :] = acc_k[j * nvp:(j + 1) * nvp]
        vo_ref[pl.ds(j, nvp, stride=nj), :] = acc_v[j * nvp:(j + 1) * nvp]


def _compress(k, v, wk, wv):
    nl, pages, _, r = k.shape
    npg = next(c for c in range(min(max(CMP_ROWS // r, 1), pages), 0, -1) if pages % c == 0)
    nblk = npg * r // CMP_BLOCK
    rspec = pl.BlockSpec((None, npg, D_KV, r), lambda l, i: (l, i, 0, 0))
    wspec = pl.BlockSpec((None, CMP_BLOCK, D_KV, D_KV), lambda l, i: (l, 0, 0, 0))
    ospec = pl.BlockSpec((None, nblk, D_KV), lambda l, i: (l, i, 0))
    return pl.pallas_call(
        functools.partial(_compress_body, npg=npg, r=r),
        grid=(nl, pages // npg),
        in_specs=[rspec, rspec, wspec, wspec],
        out_specs=[ospec, ospec],
        out_shape=[jax.ShapeDtypeStruct((nl, pages * r // CMP_BLOCK, D_KV), F32)] * 2,
        scratch_shapes=[pltpu.VMEM((npg * (r // LANES) * CMP_PITCH, D_KV), F32)] * 2,
        compiler_params=pltpu.CompilerParams(dimension_semantics=("arbitrary", "arbitrary"),
                                             vmem_limit_bytes=VMEM_LIMIT),
        name="compress",
    )(k, v, wk, wv)


def _make_qbd(q, tq):
    lo = _iota((tq, LANES), 1) < HEAD_DIM
    blocks = {}
    for jv in range(D_ATTN // LANES):
        a = q[:, LANES * jv:LANES * (jv + 1)] * ATTN_SCALE
        r = pltpu.roll(a, HEAD_DIM, 1)
        for half in range(2):
            h = 2 * jv + half
            kv, g = h // GROUP, h % GROUP
            if kv == 0:
                blocks[(g, kv)] = jnp.where(lo, a if half == 0 else r, 0.0)
            else:
                blocks[(g, kv)] = jnp.where(lo, 0.0, a if half == 1 else r)
    return jnp.concatenate([blocks[(g, kv)] for g in range(GROUP) for kv in range(N_KV)], axis=0).astype(BF16)


def _extract_heads(o, tq):
    lo = _iota((tq, LANES), 1) < HEAD_DIM
    cols = []
    for jv in range(D_ATTN // LANES):
        parts = []
        for half in range(2):
            h = 2 * jv + half
            kv, g = h // GROUP, h % GROUP
            blk = o[(g * N_KV + kv) * tq:(g * N_KV + kv + 1) * tq]
            if (half == 1) != (kv == 1):
                blk = pltpu.roll(blk, HEAD_DIM, 1)
            parts.append(blk)
        cols.append(jnp.where(lo, parts[0], parts[1]))
    return jnp.concatenate(cols, axis=1)


def _gate_rows(gate, c, tq):
    return jnp.concatenate([gate[:, c * N_HEADS + kv * GROUP + g:c * N_HEADS + kv * GROUP + g + 1]
                            for g in range(GROUP) for kv in range(N_KV)], axis=0)


def _softmax_rows(s, valid):
    s = jnp.where(valid, s, NEG_INF)
    m = jnp.max(s, axis=-1, keepdims=True)
    e = jnp.exp(s - m)
    p = e / jnp.sum(e, axis=-1, keepdims=True)
    return jnp.where(valid, p, 0.0)


def _select_blocks(imp, qpos2, blkf, n_valid_imp, k, axis):
    cur = (qpos2 // SEL_BLOCK).astype(F32)
    imp = jnp.where(blkf < n_valid_imp, imp, 0.0)
    score = jnp.where(blkf == cur, 2.0 * GROUP, jnp.where(blkf < cur, imp, -1.0))
    score = jnp.where(blkf < 0.0, -3.0, score)
    sel = jnp.zeros(score.shape, F32)
    for _ in range(k):
        m = jnp.max(score, axis=axis, keepdims=True)
        idx = jnp.min(jnp.where(score == m, blkf, 1e9), axis=axis, keepdims=True)
        hit = blkf == idx
        sel = jnp.where(hit, 1.0, sel)
        score = jnp.where(hit, -3.0, score)
    return sel


def _sum_groups(p, rows2):
    out = p[0:rows2]
    for g in range(1, GROUP):
        out = out + p[g * rows2:(g + 1) * rows2]
    return out


def _attn_prompt_body(q_ref, gate_ref, kc_ref, vc_ref, ks_ref, vs_ref, kw_ref, vw_ref,
                      tcol_ref, qs_ref, blkc_ref, trow_ref, e_ref, pos_ref, cpos_ref, o_ref,
                      kas_ref, vas_ref, kaw_ref, vaw_ref, *, tq, tk, seq, top_k):
    qt = pl.program_id(1)
    q0 = qt * tq
    rows = N_HEADS * tq
    rows2 = N_KV * tq
    nt = seq // tk

    @pl.when(qt == 0)
    def _():
        ones = jnp.ones((D_KV, tk), BF16)
        for j in range(nt):
            cs = slice(j * tk, (j + 1) * tk)
            for src, dst in ((ks_ref, kas_ref), (kw_ref, kaw_ref)):
                dst[j, 0:D_KV, :] = src[0, :, cs].astype(BF16)
                dst[j, D_KV:2 * D_KV, :] = pos_ref[j]
            for src, dst in ((vs_ref, vas_ref), (vw_ref, vaw_ref)):
                dst[j, 0:D_KV, :] = src[0, :, cs].astype(BF16)
                dst[j, D_KV:2 * D_KV, :] = ones

    qbd = _make_qbd(q_ref[0], tq)
    qaug = jnp.concatenate([qbd, qs_ref[...]], axis=1)
    qpos2 = q0 + tcol_ref[...]
    gate = gate_ref[0]

    ncb = seq // CMP_BLOCK
    pad = jnp.zeros((LANES - ncb, D_KV), F32)
    kc = jnp.concatenate([kc_ref[0], pad], axis=0)
    vc = jnp.concatenate([vc_ref[0], pad], axis=0)
    kcaug = jnp.concatenate([kc.T.astype(BF16), cpos_ref[...]], axis=0)
    s_c = jnp.dot(qaug, kcaug, preferred_element_type=F32).reshape(GROUP, rows2, LANES)
    blk_end = (_iota((1, LANES), 1) + 1) * CMP_BLOCK - 1
    p_c = _softmax_rows(s_c, (qpos2 >= blk_end)[None]).reshape(rows, LANES)
    o_acc = _gate_rows(gate, 0, tq) * _mm(p_c, vc)

    p4 = _sum_groups(p_c, rows2)
    nbr = blkc_ref.shape[0]
    imp_t = (p4 + pltpu.roll(p4, LANES - 1, 1)).T[0:nbr]
    sel = _select_blocks(imp_t, q0 + trow_ref[...], blkc_ref[...], float(seq // SEL_BLOCK), top_k, 0).T
    sel = sel.astype(BF16)

    def make_step(kaug_ref, vaug_ref, mask_fn):
        def step(j, carry):
            m, acc = carry
            s = jnp.dot(qaug, kaug_ref[j], preferred_element_type=F32).reshape(GROUP, rows2, tk)
            dist = qpos2 - (j * tk + _iota((1, tk), 1))
            s = jnp.where(mask_fn(j, dist)[None], s, NEG_INF)
            m_new = jnp.maximum(m, jnp.max(s, axis=2, keepdims=True))
            alpha = jnp.exp(m - m_new).reshape(rows, 1)
            p = jnp.exp(s - m_new).astype(BF16).reshape(rows, tk)
            pv = lax.dot_general(p, vaug_ref[j], (((1,), (1,)), ((), ())), preferred_element_type=F32)
            return m_new, alpha * acc + pv

        return step

    def sel_mask(j, dist):
        picked = jnp.dot(sel, e_ref[j], preferred_element_type=F32)
        return (picked > 0.5) & (dist >= 0)

    def win_mask(j, dist):
        return (dist >= 0) & (dist <= WINDOW)

    step_s = make_step(kas_ref, vas_ref, sel_mask)
    step_w = make_step(kaw_ref, vaw_ref, win_mask)
    init = (jnp.full((GROUP, rows2, 1), NEG_INF, F32), jnp.zeros((rows, 2 * D_KV), F32))
    n_s = (q0 + tq - 1) // tk + 1
    j_lo = jnp.maximum(q0 - WINDOW, 0) // tk
    n_w = n_s - j_lo

    def both(i, carry):
        return step_s(i, carry[0]), step_w(j_lo + i, carry[1])

    c_s, c_w = lax.fori_loop(0, n_w, both, (init, init))
    c_s = lax.fori_loop(n_w, n_s, step_s, c_s)
    o_s = c_s[1][:, 0:D_KV] / c_s[1][:, D_KV:2 * D_KV]
    o_w = c_w[1][:, 0:D_KV] / c_w[1][:, D_KV:2 * D_KV]
    o_acc = o_acc + _gate_rows(gate, 1, tq) * o_s + _gate_rows(gate, 2, tq) * o_w
    o_ref[0] = _extract_heads(o_acc, tq)


def _row_consts(tq):
    r = jnp.arange(N_HEADS * tq)
    g, kv, t = r // (N_KV * tq), (r // tq) % N_KV, r % tq
    head = kv * GROUP + g
    slopes = jnp.exp2(-8.0 * (head + 1).astype(F32) / N_HEADS)
    return slopes[:, None], t.astype(jnp.int32)[:, None]


def _key_tiles(a, tk):
    return a.reshape(a.shape[0], -1, tk).transpose(1, 0, 2)


def _attn_prompt(q, gate, kcmp, vcmp, ks, vs, kw, vw):
    b, t, _ = q.shape
    tq = min(ATTN_TQ, t)
    tk = min(ATTN_TK, t)
    n_sb = t // SEL_BLOCK
    assert t // POS_SPLIT < 256
    slopes, tcol = _row_consts(tq)
    qs = jnp.zeros((N_HEADS * tq, LANES), F32).at[:, 0].set(POS_SPLIT * slopes[:, 0]).at[:, 1].set(slopes[:, 0])
    kpos = jnp.arange(t)
    pos = jnp.zeros((LANES, t), F32).at[0].set(kpos // POS_SPLIT).at[1].set(kpos % POS_SPLIT)
    lane = jnp.arange(LANES)
    blk = jnp.where((lane % 2 == 0) & (lane < 2 * n_sb), lane // 2, -1).astype(F32)
    e = (blk[:, None] == (kpos // SEL_BLOCK).astype(F32)[None, :])
    blk_end = (lane + 1) * CMP_BLOCK - 1
    cpos = jnp.zeros((LANES, LANES), F32).at[0].set(blk_end // POS_SPLIT).at[1].set(blk_end % POS_SPLIT)
    tcol2 = tcol[:N_KV * tq]
    nbr = min(LANES, -(-2 * n_sb // SUBLANES) * SUBLANES)
    consts = (tcol2, qs.astype(BF16), blk[:nbr, None], tcol2.reshape(1, -1),
              _key_tiles(e[:nbr].astype(BF16), tk), _key_tiles(pos.astype(BF16), tk), cpos.astype(BF16))
    tile = lambda n: pl.BlockSpec((1, tq, n), lambda i, j: (i, j, 0))
    seq = lambda r, n: pl.BlockSpec((1, r, n), lambda i, j: (i, 0, 0))
    return pl.pallas_call(
        functools.partial(_attn_prompt_body, tq=tq, tk=tk, seq=t, top_k=min(TOP_K, n_sb)),
        grid=(b, t // tq),
        in_specs=[tile(D_ATTN), tile(LANES), seq(t // CMP_BLOCK, D_KV), seq(t // CMP_BLOCK, D_KV),
                  seq(D_KV, t), seq(D_KV, t), seq(D_KV, t), seq(D_KV, t)] + [_full(c.shape) for c in consts],
        out_specs=tile(D_ATTN),
        out_shape=jax.ShapeDtypeStruct((b, t, D_ATTN), F32),
        scratch_shapes=[pltpu.VMEM((t // tk, 2 * D_KV, tk), BF16)] * 4,
        compiler_params=pltpu.CompilerParams(dimension_semantics=("arbitrary", "arbitrary"),
                                             vmem_limit_bytes=VMEM_LIMIT),
        name="attn_prompt",
    )(q, gate, kcmp, vcmp, ks, vs, kw, vw, *consts)


def _attn_sample_body(pt_ref, q_ref, gate_ref, ksn_ref, vsn_ref, kwn_ref, vwn_ref, kwin_ref, vwin_ref,
                      kcp_ref, vcp_ref, cvalid_ref, cend_ref, slope_ref, tcol_ref, blkf_ref, e_ref, *rest,
                      t, n_pages, page, past, w_buf, top_k, nbs):
    ks_pages = rest[:nbs * n_pages]
    vs_pages = rest[nbs * n_pages:2 * nbs * n_pages]
    o_ref, kwo_ref, vwo_ref = rest[2 * nbs * n_pages:]
    b0 = pl.program_id(0) * nbs
    rows = N_HEADS * t
    rows2 = N_KV * t
    seqs = range(nbs)

    def stack(xs):
        return jnp.concatenate(xs, axis=0)

    def part(x, bb, n):
        return x[bb * n:(bb + 1) * n]

    def pad_rows(x):
        if x.shape[0] == LANES:
            return x
        return jnp.concatenate([x, jnp.zeros((LANES - x.shape[0], D_KV), F32)], axis=0)

    slope = stack([slope_ref[...]] * nbs)
    qpos = stack([past + tcol_ref[...]] * nbs)
    qpos2 = stack([past + tcol_ref[0:rows2, :]] * nbs)
    qbd = [_make_qbd(q_ref[bb], t) for bb in seqs]
    gates = [gate_ref[bb] for bb in seqs]

    def gate_rows(c):
        return stack([_gate_rows(gates[bb], c, t) for bb in seqs])

    kc = [pad_rows(stack([kcp_ref[pt_ref[b0 + bb, p] // 2] for p in range(n_pages)])) for bb in seqs]
    vc = [pad_rows(stack([vcp_ref[pt_ref[b0 + bb, p] // 2] for p in range(n_pages)])) for bb in seqs]
    dist_c = qpos - cend_ref[...]
    s_c = stack([_mm_nt(qbd[bb], kc[bb]) for bb in seqs]) - slope * dist_c.astype(F32)
    cvalid = stack([jnp.broadcast_to(cvalid_ref[bb], (rows, LANES)) for bb in seqs])
    p_c = _softmax_rows(s_c, (cvalid > 0.5) & (dist_c >= 0))
    o_acc = gate_rows(0) * stack([_mm(part(p_c, bb, rows), vc[bb]) for bb in seqs])

    p4 = stack([_sum_groups(part(p_c, bb, rows), rows2) for bb in seqs])
    x1 = p4 + pltpu.roll(p4, LANES - 1, 1)
    imp = x1 + pltpu.roll(x1, LANES - 4, 1)
    sel = _select_blocks(imp, qpos2, blkf_ref[...], float(past // SEL_BLOCK), top_k, 1)
    sel4 = stack([part(sel, bb, rows2) for bb in seqs for _ in range(GROUP)])

    s_s = stack([jnp.concatenate([_mm(qbd[bb], ks_pages[bb * n_pages + p][...]) for p in range(n_pages)]
                                 + [_mm_nt(qbd[bb], pad_rows(ksn_ref[bb]))], axis=1) for bb in seqs])
    dist_s = qpos - _iota((1, past + LANES), 1)
    picked = _mm(sel4, e_ref[...])
    p_s = _softmax_rows(s_s - slope * dist_s.astype(F32), (picked > 0.5) & (dist_s >= 0))
    o_s = []
    for bb in seqs:
        pb = part(p_s, bb, rows)
        acc = _mm(pb[:, past:], pad_rows(vsn_ref[bb]))
        for p in range(n_pages):
            acc = acc + _mm_nt(pb[:, p * page:(p + 1) * page], vs_pages[bb * n_pages + p][...])
        o_s.append(acc)
    o_acc = o_acc + gate_rows(1) * stack(o_s)

    kwin = [kwin_ref[bb] for bb in seqs]
    vwin = [vwin_ref[bb] for bb in seqs]
    s_w = stack([jnp.concatenate([_mm(qbd[bb], kwin[bb]), _mm_nt(qbd[bb], pad_rows(kwn_ref[bb]))], axis=1)
                 for bb in seqs])
    dist_w = qpos - (past - w_buf + _iota((1, w_buf + LANES), 1))
    p_w = _softmax_rows(s_w - slope * dist_w.astype(F32), (dist_w >= 0) & (dist_w <= WINDOW))
    o_w = stack([_mm_nt(part(p_w, bb, rows)[:, :w_buf], vwin[bb])
                 + _mm(part(p_w, bb, rows)[:, w_buf:], pad_rows(vwn_ref[bb])) for bb in seqs])
    o_acc = o_acc + gate_rows(2) * o_w

    is_new = _iota((D_KV, LANES), 1) >= LANES - t

    def shift_in(win, new):
        new_t = jnp.concatenate([jnp.zeros((LANES - t, D_KV), F32), new], axis=0).T
        rolled = pltpu.roll(win, w_buf - t, 1)
        last = jnp.where(is_new, new_t, rolled[:, w_buf - LANES:])
        return jnp.concatenate([rolled[:, :w_buf - LANES], last], axis=1)

    for bb in seqs:
        o_ref[bb] = _extract_heads(part(o_acc, bb, rows), t)
        kwo_ref[bb] = shift_in(kwin[bb], kwn_ref[bb])
        vwo_ref[bb] = shift_in(vwin[bb], vwn_ref[bb])


def _page_map(l, nbs, bb, p, i, pt):
    return (l, pt[i * nbs + bb, p], 0, 0)


def _attn_sample(l, page_table, q, gate, ks_new, vs_new, kw_new, vw_new, kwin, vwin, kcp, vcp,
                 cache_ks, cache_vs):
    nb, t, _ = q.shape
    n_pages = page_table.shape[1]
    page = cache_ks.shape[3]
    past = n_pages * page
    w_buf = kwin.shape[3]
    n_sb = -(-(past + t) // SEL_BLOCK)
    assert t == SUBLANES and n_pages * 8 <= LANES and page == LANES and past % SEL_BLOCK == 0
    assert w_buf % LANES == 0
    slopes, tcol = _row_consts(t)
    lane = jnp.arange(LANES)
    pg, r = lane // 8, lane % 8
    in_range = pg < n_pages
    cend = jnp.where(in_range, (4 * pg + r % 4 + 1) * CMP_BLOCK - 1, 1 << 30).astype(jnp.int32)[None, :]
    par = page_table % 2
    par_l = jnp.take(par, jnp.minimum(pg, n_pages - 1), axis=1)
    cvalid = ((r[None, :] // 4 == par_l) & in_range[None, :]).astype(F32)[:, None, :]
    blk = jnp.where(in_range & (r % 2 == 0) & (r < 4), 2 * pg + r // 2, -1)
    blk = jnp.where(lane == 1, n_sb - 1, blk)
    blkf = blk.astype(F32)[None, :]
    e = (blkf.reshape(-1, 1) == (jnp.arange(past + LANES) // SEL_BLOCK).astype(F32)[None, :]).astype(BF16)
    nbs = next(c for c in range(min(DEC_STEP, nb), 0, -1) if nb % c == 0)
    tok = lambda n: pl.BlockSpec((nbs, t, n), lambda i, pt: (i, 0, 0))
    win = pl.BlockSpec((None, nbs, D_KV, w_buf), lambda i, pt: (l, i, 0, 0))
    win_out = pl.BlockSpec((nbs, D_KV, w_buf), lambda i, pt: (i, 0, 0))
    res = pl.BlockSpec((None,) + kcp.shape[1:], lambda i, pt: (l, 0, 0, 0))
    const = lambda a: pl.BlockSpec(a.shape, lambda i, pt: (0,) * a.ndim)
    pages = [pl.BlockSpec((None, None, D_KV, page), functools.partial(_page_map, l, nbs, bb, p))
             for bb in range(nbs) for p in range(n_pages)]
    grid_spec = pltpu.PrefetchScalarGridSpec(
        num_scalar_prefetch=1,
        grid=(nb // nbs,),
        in_specs=[tok(D_ATTN), tok(LANES), tok(D_KV), tok(D_KV), tok(D_KV), tok(D_KV), win, win, res, res,
                  pl.BlockSpec((nbs, 1, LANES), lambda i, pt: (i, 0, 0)),
                  const(cend), const(slopes), const(tcol), const(blkf), const(e)] + pages + pages,
        out_specs=[tok(D_ATTN), win_out, win_out],
    )
    return pl.pallas_call(
        functools.partial(_attn_sample_body, t=t, n_pages=n_pages, page=page, past=past, w_buf=w_buf,
                          top_k=min(TOP_K, n_sb), nbs=nbs),
        grid_spec=grid_spec,
        out_shape=[jax.ShapeDtypeStruct((nb, t, D_ATTN), F32),
                   jax.ShapeDtypeStruct((nb, D_KV, w_buf), F32), jax.ShapeDtypeStruct((nb, D_KV, w_buf), F32)],
        compiler_params=pltpu.CompilerParams(dimension_semantics=("arbitrary",), vmem_limit_bytes=VMEM_LIMIT),
        name="attn_sample",
    )(page_table, q, gate, ks_new, vs_new, kw_new, vw_new, kwin, vwin, kcp, vcp, cvalid,
      cend, slopes, tcol, blkf, e, *([cache_ks] * (nbs * n_pages)), *([cache_vs] * (nbs * n_pages)))


def _back_head(x, o, pab, wg2_ref, woa_ref, wmix_ref, ln1g_ref, ln1b_ref):
    y_c = _mm(o, woa_ref[...])
    g2 = jax.nn.sigmoid(_mm(x, wg2_ref[...]))
    mixed = pab + g2 * y_c
    return _layer_norm(ALPHA * x + _mm(mixed, wmix_ref[...]), ln1g_ref[...], ln1b_ref[...])


def _back_prompt_body(x_ref, o_ref, pab_ref, wg2_ref, woa_ref, wmix_ref, ln1g_ref, ln1b_ref,
                      wup_ref, fcw_ref, fcb_ref, wdn_ref, ln2g_ref, ln2b_ref,
                      y_ref, fst_ref, ext_ref, *, tt):
    j = pl.program_id(1)

    @pl.when(j == 0)
    def _():
        ext_ref[0:8, :] = jnp.zeros((8, D_FF), F32)

    h = _back_head(x_ref[0], o_ref[0], pab_ref[0], wg2_ref, woa_ref, wmix_ref, ln1g_ref, ln1b_ref)
    hb = h.astype(BF16)
    cf = D_FF // FF_CHUNKS
    f = jnp.zeros((tt, D_MODEL), F32)
    for c in range(FF_CHUNKS):
        cs = slice(c * cf, (c + 1) * cf)
        u = _mm(hb, wup_ref[:, c * cf:(c + 1) * cf])
        gt = _mm(hb, wup_ref[:, D_FF + c * cf:D_FF + (c + 1) * cf])
        ext_ref[8:8 + tt, cs] = u
        conv = fcb_ref[:, cs] + fcw_ref[0:1, cs] * ext_ref[6:6 + tt, cs]
        conv = conv + fcw_ref[1:2, cs] * ext_ref[7:7 + tt, cs]
        conv = conv + fcw_ref[2:3, cs] * u
        f = f + _mm(jax.nn.gelu(conv) * gt, wdn_ref[c * cf:(c + 1) * cf, :])
    y_ref[0] = _layer_norm(ALPHA * h + f, ln2g_ref[...], ln2b_ref[...])
    last = ext_ref[tt:tt + 8, :]
    fst_ref[0] = last
    ext_ref[0:8, :] = last


def _back_sample_body(x_ref, o_ref, pab_ref, hf_ref, wg2_ref, woa_ref, wmix_ref, ln1g_ref, ln1b_ref,
                      wup_ref, fcw_ref, fcb_ref, wdn_ref, ln2g_ref, ln2b_ref,
                      y_ref, fst_ref, ext_ref, *, ns, t):
    h = _back_head(x_ref[...], o_ref[...], pab_ref[...], wg2_ref, woa_ref, wmix_ref, ln1g_ref, ln1b_ref)
    hb = h.astype(BF16)
    cf = D_FF // FF_CHUNKS
    ext_ref[:, 6:8, :] = hf_ref[...]
    f = jnp.zeros((ns * t, D_MODEL), F32)
    for c in range(FF_CHUNKS):
        cs = slice(c * cf, (c + 1) * cf)
        u = _mm(hb, wup_ref[:, c * cf:(c + 1) * cf])
        gt = _mm(hb, wup_ref[:, D_FF + c * cf:D_FF + (c + 1) * cf])
        ext_ref[:, 8:8 + t, cs] = u.reshape(ns, t, cf)
        fcw = fcw_ref[:, cs]
        conv = fcb_ref[:, cs] + fcw[0:1] * ext_ref[:, 6:6 + t, cs]
        conv = conv + fcw[1:2] * ext_ref[:, 7:7 + t, cs]
        conv = conv + fcw[2:3] * ext_ref[:, 8:8 + t, cs]
        f = f + _mm(jax.nn.gelu(conv).reshape(ns * t, cf) * gt, wdn_ref[c * cf:(c + 1) * cf, :])
    y_ref[...] = _layer_norm(ALPHA * h + f, ln2g_ref[...], ln2b_ref[...])
    fst_ref[...] = ext_ref[:, t:t + 8, :]


def _back_prompt(x, o, pab, wts):
    b, t, _ = x.shape
    tt = min(FRONT_TILE, t)
    tok = lambda n: pl.BlockSpec((1, tt, n), lambda i, j: (i, j, 0))
    return pl.pallas_call(
        functools.partial(_back_prompt_body, tt=tt),
        grid=(b, t // tt),
        in_specs=[tok(D_MODEL), tok(D_ATTN), tok(D_MODEL)] + [_full(w.shape) for w in wts],
        out_specs=[tok(D_MODEL), pl.BlockSpec((1, 8, D_FF), lambda i, j: (i, 0, 0))],
        out_shape=[jax.ShapeDtypeStruct((b, t, D_MODEL), F32), jax.ShapeDtypeStruct((b, 8, D_FF), F32)],
        scratch_shapes=[pltpu.VMEM((8 + tt, D_FF), F32)],
        compiler_params=pltpu.CompilerParams(dimension_semantics=("arbitrary", "arbitrary"),
                                             vmem_limit_bytes=VMEM_LIMIT),
        name="back_prompt",
    )(x, o, pab, *wts)


def _back_sample(x, o, pab, hist_f, wts, t):
    n = x.shape[0]
    nb = n // t
    ns = min(FRONT_TILE // t, nb)
    rows = ns * t
    tok = lambda c: pl.BlockSpec((rows, c), lambda i: (i, 0))
    return pl.pallas_call(
        functools.partial(_back_sample_body, ns=ns, t=t),
        grid=(nb // ns,),
        in_specs=[tok(D_MODEL), tok(D_ATTN), tok(D_MODEL),
                  pl.BlockSpec((ns, CONV_W - 1, D_FF), lambda i: (i, 0, 0))] + [_full(w.shape) for w in wts],
        out_specs=[tok(D_MODEL), pl.BlockSpec((ns, 8, D_FF), lambda i: (i, 0, 0))],
        out_shape=[jax.ShapeDtypeStruct((n, D_MODEL), F32), jax.ShapeDtypeStruct((nb, 8, D_FF), F32)],
        scratch_shapes=[pltpu.VMEM((ns, 8 + t, D_FF), F32)],
        compiler_params=pltpu.CompilerParams(dimension_semantics=("arbitrary",), vmem_limit_bytes=VMEM_LIMIT),
        name="back_sample",
    )(x, o, pab, hist_f, *wts)


def _block_diag2(w):
    z = jnp.zeros_like(w)
    return jnp.concatenate([jnp.concatenate([w, z], axis=-1), jnp.concatenate([z, w], axis=-1)], axis=-2)


def kernel(x_prompt, x_sample, cache_k_cmp, cache_v_cmp, cache_k_sel, cache_v_sel, state_k_win, state_v_win, state_conv, state_pool, state_ffn_conv, page_table, w_in, conv_w, conv_b, w_out_conv, pool_w, pool_scale, w_cmp_k, w_cmp_v, w_out_attn, w_mix_out, ln1_g, ln1_b, w_ffn_up, ffn_conv_w, ffn_conv_b, w_ffn_down, ln2_g, ln2_b):
    depth = w_in.shape[0]
    bp, seq, _ = x_prompt.shape
    nb, t, _ = x_sample.shape
    n_phys, page = cache_k_cmp.shape[1], cache_k_cmp.shape[2]
    past = page_table.shape[1] * page
    w_buf = state_k_win.shape[2]

    w_front = jnp.concatenate(
        [w_in[:, :, :IN_GATES], jnp.zeros((depth, D_MODEL, C_G01 - IN_GATES), F32),
         w_in[:, :, IN_GATES:IN_GATES + 2 * D_MODEL]], axis=2).astype(BF16)
    w_g2 = w_in[:, :, IN_GATES + 2 * D_MODEL:].astype(BF16)
    ng = pool_w.shape[1]
    pool_bd = jnp.zeros((depth, D_POOL, D_MODEL), F32)
    for g in range(ng):
        pool_bd = pool_bd.at[:, g * POOL_GROUP:(g + 1) * POOL_GROUP,
                             g * (D_MODEL // ng):(g + 1) * (D_MODEL // ng)].set(pool_w[:, g])
    pool_bd = pool_bd.astype(BF16)
    wck = _block_diag2(w_cmp_k).astype(BF16)
    wcv = _block_diag2(w_cmp_v).astype(BF16)
    w_oc, w_oa, w_mix = w_out_conv.astype(BF16), w_out_attn.astype(BF16), w_mix_out.astype(BF16)
    w_up, w_dn = w_ffn_up.astype(BF16), w_ffn_down.astype(BF16)
    row = lambda a: a[:, None, :]

    keys_minor = lambda a: jnp.transpose(a, (0, 1, 3, 4, 2)).reshape(a.shape[0], a.shape[1], D_KV, a.shape[2])
    tokens_major = lambda a: jnp.transpose(a.reshape(a.shape[0], N_KV, HEAD_DIM, a.shape[2]), (0, 3, 1, 2))
    ck_s, cv_s = keys_minor(cache_k_sel), keys_minor(cache_v_sel)
    kwin_all, vwin_all = keys_minor(state_k_win), keys_minor(state_v_win)
    kcp, vcp = _compress(keys_minor(cache_k_cmp), keys_minor(cache_v_cmp), wck, wcv)
    kcp = kcp.reshape(depth, n_phys // 2, 8, D_KV)
    vcp = vcp.reshape(depth, n_phys // 2, 8, D_KV)

    y_p = x_prompt
    y_s = x_sample.reshape(nb * t, D_MODEL)
    st_p, st_s = [], []
    for l in range(depth):
        fw = (w_front[l], conv_w[l], row(conv_b)[l], w_oc[l], pool_bd[l], row(pool_scale)[l])
        bw = (w_g2[l], w_oa[l], w_mix[l], row(ln1_g)[l], row(ln1_b)[l], w_up[l], ffn_conv_w[l],
              row(ffn_conv_b)[l], w_dn[l], row(ln2_g)[l], row(ln2_b)[l])

        pab, q, kc, vc, ks, vs, kw, vw, gate, cst, pst = _front_prompt(y_p, fw)
        kcmp, vcmp = _compress(kc[None], vc[None], wck[l:l + 1], wcv[l:l + 1])
        o = _attn_prompt(q, gate, kcmp.reshape(bp, seq // CMP_BLOCK, D_KV),
                         vcmp.reshape(bp, seq // CMP_BLOCK, D_KV), ks, vs, kw, vw)
        y_p, fst = _back_prompt(y_p, o, pab, bw)
        st_p.append(tuple(tokens_major(a) for a in (kc, vc, ks, vs, kw[:, :, seq - w_buf:], vw[:, :, seq - w_buf:]))
                    + (cst[:, 8 - (CONV_W - 1):], pst[:, 16 - POOL_HIST:], fst[:, 8 - (CONV_W - 1):]))

        pab, q, kc, vc, ks, vs, kw, vw, gate, cst, pst = _front_sample(
            y_s.reshape(nb, t, D_MODEL), state_conv[l], state_pool[l], fw, past)
        r3 = lambda a: a.reshape(nb, t, a.shape[-1])
        o, kwin, vwin = _attn_sample(l, page_table, r3(q), r3(gate), r3(ks), r3(vs), r3(kw), r3(vw),
                                     kwin_all, vwin_all, kcp, vcp, ck_s, cv_s)
        y_s, fst = _back_sample(y_s, o.reshape(nb * t, D_ATTN), pab, state_ffn_conv[l], bw, t)
        kv5 = lambda a: a.reshape(nb, t, N_KV, HEAD_DIM)
        st_s.append((kv5(kc), kv5(vc), kv5(ks), kv5(vs), tokens_major(kwin), tokens_major(vwin),
                     cst[:, 8 - (CONV_W - 1):], pst[:, 16 - POOL_HIST:], fst[:, 8 - (CONV_W - 1):]))

    kc_p, vc_p, ks_p, vs_p, kw_p, vw_p, conv_p, pool_p, ffn_p = [jnp.stack(a) for a in zip(*st_p)]
    kc_s, vc_s, ks_s, vs_s, kw_s, vw_s, conv_s, pool_s, ffn_s = [jnp.stack(a) for a in zip(*st_s)]
    return (y_p, y_s.reshape(nb, t, D_MODEL), kc_p, kc_s, vc_p, vc_s, ks_p, ks_s, vs_p, vs_s,
            kw_p, kw_s, vw_p, vw_s, conv_p, conv_s, pool_p, pool_s, ffn_p, ffn_s)
```

```python
import functools

import jax
import jax.numpy as jnp
from jax import lax
from jax.experimental import pallas as pl
from jax.experimental.pallas import tpu as pltpu

F32 = jnp.float32
BF16 = jnp.bfloat16

D_MODEL = 1024
DEPTH = 4
D_CONV = 256
CONV_W = 3
D_POOL = 256
POOL_WINDOWS = (2, 4, 8, 16)
POOL_GROUP = D_POOL // len(POOL_WINDOWS)
POOL_HIST = 15
N_HEADS = 8
N_KV = 2
GROUP = N_HEADS // N_KV
HEAD_DIM = 64
D_ATTN = N_HEADS * HEAD_DIM
D_KV = N_KV * HEAD_DIM
CMP_BLOCK = 32
SEL_BLOCK = 64
TOP_K = 8
WINDOW = 512
N_BRANCH = 3
D_FF = 2816
ALPHA = (2.0 * DEPTH) ** 0.25
LN_EPS = 1e-5
ATTN_SCALE = HEAD_DIM ** -0.5
NEG_INF = -1e30
POS_SPLIT = 64

LANES = 128
SUBLANES = 8
VMEM_LIMIT = 56 * 1024 * 1024

C_AB, C_AC, C_AH, C_U, C_Q = 0, D_CONV, 2 * D_CONV, 3 * D_CONV, 3 * D_CONV + D_POOL
C_KV = C_Q + D_ATTN
C_NSA = C_KV + 6 * D_KV
C_G01 = C_NSA + LANES
W_FRONT = C_G01 + 2 * D_MODEL
IN_MAIN = C_NSA
IN_GATES = IN_MAIN + N_BRANCH * N_HEADS

FRONT_TILE = 256
ATTN_TQ = 128
ATTN_TK = 256
CMP_ROWS = 8192
CMP_PITCH = LANES + SUBLANES
DEC_STEP = 4
FF_CHUNKS = 2


def _mm(a, b):
    return jnp.dot(a.astype(BF16), b.astype(BF16), preferred_element_type=F32)


def _mm_nt(a, b):
    return lax.dot_general(a.astype(BF16), b.astype(BF16), (((1,), (1,)), ((), ())),
                           preferred_element_type=F32)


def _layer_norm(x, g, b):
    mu = jnp.mean(x, axis=-1, keepdims=True)
    var = jnp.mean(jnp.square(x - mu), axis=-1, keepdims=True)
    return (x - mu) * lax.rsqrt(var + LN_EPS) * g + b


def _iota(shape, axis):
    return lax.broadcasted_iota(jnp.int32, shape, axis)


def _pool_delta(u, shifted, pos):
    nd = u.ndim
    acc = u
    sums = {}
    for j in range(1, POOL_WINDOWS[-1]):
        acc = acc + shifted(j)
        if j + 1 in POOL_WINDOWS:
            sums[j + 1] = acc
    grp = _iota((1,) * (nd - 1) + (D_POOL,), nd - 1) // POOL_GROUP
    s = sums[POOL_WINDOWS[-1]]
    win = jnp.full(grp.shape, POOL_WINDOWS[-1], jnp.int32)
    for g in range(len(POOL_WINDOWS) - 2, -1, -1):
        s = jnp.where(grp == g, sums[POOL_WINDOWS[g]], s)
        win = jnp.where(grp == g, POOL_WINDOWS[g], win)
    cnt = jnp.minimum(win, pos + 1).astype(F32)
    return s / cnt - u


def _front_tail(z, conv, d, woc_ref, pbd_ref, psc_ref, outs, kv_transposed):
    pab_ref, q_ref, kv_refs, gate_ref = outs
    y_a = _mm(z[:, C_AB:C_AB + D_CONV] * conv, woc_ref[...])
    y_b = _mm(d, pbd_ref[...]) * psc_ref[...]
    g0 = jax.nn.sigmoid(z[:, C_G01:C_G01 + D_MODEL])
    g1 = jax.nn.sigmoid(z[:, C_G01 + D_MODEL:C_G01 + 2 * D_MODEL])
    pab_ref[...] = (g0 * y_a + g1 * y_b).reshape(pab_ref.shape)
    q_ref[...] = z[:, C_Q:C_Q + D_ATTN].reshape(q_ref.shape)
    for i, r in enumerate(kv_refs):
        kv = z[:, C_KV + i * D_KV:C_KV + (i + 1) * D_KV]
        r[...] = (kv.T if kv_transposed else kv).reshape(r.shape)
    gate_ref[...] = jax.nn.sigmoid(z[:, C_NSA:C_NSA + LANES]).reshape(gate_ref.shape)


def _front_prompt_body(x_ref, w_ref, cw_ref, cb_ref, woc_ref, pbd_ref, psc_ref,
                       pab_ref, q_ref, kc_ref, vc_ref, ks_ref, vs_ref, kw_ref, vw_ref, gate_ref,
                       cst_ref, pst_ref, extc_ref, extu_ref, *, tt):
    j = pl.program_id(1)

    @pl.when(j == 0)
    def _():
        extc_ref[0:8, :] = jnp.zeros((8, D_CONV), F32)
        extu_ref[0:16, :] = jnp.zeros((16, D_POOL), F32)

    z = _mm(x_ref[0], w_ref[...])
    ch = z[:, C_AC:C_AC + D_CONV] * z[:, C_AH:C_AH + D_CONV]
    extc_ref[8:8 + tt, :] = ch
    conv = cb_ref[...] + cw_ref[0:1, :] * extc_ref[6:6 + tt, :]
    conv = conv + cw_ref[1:2, :] * extc_ref[7:7 + tt, :]
    conv = conv + cw_ref[2:3, :] * ch
    u = z[:, C_U:C_U + D_POOL]
    extu_ref[16:16 + tt, :] = u
    pos = j * tt + _iota((tt, 1), 0)
    d = _pool_delta(u, lambda s: extu_ref[16 - s:16 - s + tt, :], pos)
    _front_tail(z, conv, d, woc_ref, pbd_ref, psc_ref,
                (pab_ref, q_ref, (kc_ref, vc_ref, ks_ref, vs_ref, kw_ref, vw_ref), gate_ref), True)
    last_c = extc_ref[tt:tt + 8, :]
    last_u = extu_ref[tt:tt + 16, :]
    cst_ref[0] = last_c
    pst_ref[0] = last_u
    extc_ref[0:8, :] = last_c
    extu_ref[0:16, :] = last_u


def _front_sample_body(x_ref, hc_ref, hu_ref, w_ref, cw_ref, cb_ref, woc_ref, pbd_ref, psc_ref,
                       pab_ref, q_ref, kc_ref, vc_ref, ks_ref, vs_ref, kw_ref, vw_ref, gate_ref,
                       cst_ref, pst_ref, extc_ref, extu_ref, *, ns, t, pos0):
    z = _mm(x_ref[...], w_ref[...])
    ch = z[:, C_AC:C_AC + D_CONV] * z[:, C_AH:C_AH + D_CONV]
    extc_ref[:, 6:8, :] = hc_ref[...]
    extc_ref[:, 8:8 + t, :] = ch.reshape(ns, t, D_CONV)
    cw = cw_ref[...]
    conv = cb_ref[...] + cw[0:1] * extc_ref[:, 6:6 + t, :]
    conv = conv + cw[1:2] * extc_ref[:, 7:7 + t, :]
    conv = conv + cw[2:3] * extc_ref[:, 8:8 + t, :]
    u = z[:, C_U:C_U + D_POOL]
    extu_ref[:, 1:16, :] = hu_ref[...]
    extu_ref[:, 16:16 + t, :] = u.reshape(ns, t, D_POOL)
    pos = pos0 + _iota((1, t, 1), 1)
    d = _pool_delta(extu_ref[:, 16:16 + t, :], lambda s: extu_ref[:, 16 - s:16 - s + t, :], pos)
    _front_tail(z, conv.reshape(ns * t, D_CONV), d.reshape(ns * t, D_POOL), woc_ref, pbd_ref, psc_ref,
                (pab_ref, q_ref, (kc_ref, vc_ref, ks_ref, vs_ref, kw_ref, vw_ref), gate_ref), False)
    cst_ref[...] = extc_ref[:, t:t + 8, :]
    pst_ref[...] = extu_ref[:, t:t + 16, :]


def _full(shape):
    n = len(shape)
    return pl.BlockSpec(shape, lambda *_: (0,) * n, pipeline_mode=pl.Buffered(1))


def _front_prompt(x, wts):
    b, t, _ = x.shape
    tt = min(FRONT_TILE, t)
    tok = lambda n: pl.BlockSpec((1, tt, n), lambda i, j: (i, j, 0))
    st = lambda r, n: pl.BlockSpec((1, r, n), lambda i, j: (i, 0, 0))
    kvt = pl.BlockSpec((1, D_KV, tt), lambda i, j: (i, 0, j))
    out_shape = ([jax.ShapeDtypeStruct((b, t, D_MODEL), F32), jax.ShapeDtypeStruct((b, t, D_ATTN), F32)]
                 + [jax.ShapeDtypeStruct((b, D_KV, t), F32)] * 6
                 + [jax.ShapeDtypeStruct((b, t, LANES), F32),
                    jax.ShapeDtypeStruct((b, 8, D_CONV), F32), jax.ShapeDtypeStruct((b, 16, D_POOL), F32)])
    out_specs = ([tok(D_MODEL), tok(D_ATTN)] + [kvt] * 6 + [tok(LANES), st(8, D_CONV), st(16, D_POOL)])
    return pl.pallas_call(
        functools.partial(_front_prompt_body, tt=tt),
        grid=(b, t // tt),
        in_specs=[tok(D_MODEL)] + [_full(w.shape) for w in wts],
        out_specs=out_specs,
        out_shape=out_shape,
        scratch_shapes=[pltpu.VMEM((8 + tt, D_CONV), F32), pltpu.VMEM((16 + tt, D_POOL), F32)],
        compiler_params=pltpu.CompilerParams(dimension_semantics=("arbitrary", "arbitrary"),
                                             vmem_limit_bytes=VMEM_LIMIT),
        name="front_prompt",
    )(x, *wts)


def _front_sample(x, hist_c, hist_u, wts, pos0):
    nb, t, _ = x.shape
    ns = min(FRONT_TILE // t, nb)
    rows = ns * t
    xf = x.reshape(nb * t, D_MODEL)
    tok = lambda n: pl.BlockSpec((rows, n), lambda i: (i, 0))
    st = lambda r, n: pl.BlockSpec((ns, r, n), lambda i: (i, 0, 0))
    out_shape = ([jax.ShapeDtypeStruct((nb * t, D_MODEL), F32), jax.ShapeDtypeStruct((nb * t, D_ATTN), F32)]
                 + [jax.ShapeDtypeStruct((nb * t, D_KV), F32)] * 6
                 + [jax.ShapeDtypeStruct((nb * t, LANES), F32),
                    jax.ShapeDtypeStruct((nb, 8, D_CONV), F32), jax.ShapeDtypeStruct((nb, 16, D_POOL), F32)])
    out_specs = ([tok(D_MODEL), tok(D_ATTN)] + [tok(D_KV)] * 6 + [tok(LANES), st(8, D_CONV), st(16, D_POOL)])
    return pl.pallas_call(
        functools.partial(_front_sample_body, ns=ns, t=t, pos0=pos0),
        grid=(nb // ns,),
        in_specs=[tok(D_MODEL), st(CONV_W - 1, D_CONV), st(POOL_HIST, D_POOL)] + [_full(w.shape) for w in wts],
        out_specs=out_specs,
        out_shape=out_shape,
        scratch_shapes=[pltpu.VMEM((ns, 8 + t, D_CONV), F32), pltpu.VMEM((ns, 16 + t, D_POOL), F32)],
        compiler_params=pltpu.CompilerParams(dimension_semantics=("arbitrary",),
                                             vmem_limit_bytes=VMEM_LIMIT),
        name="front_sample",
    )(xf, hist_c, hist_u, *wts)


def _compress_body(k_ref, v_ref, wk_ref, wv_ref, ko_ref, vo_ref, kr_ref, vr_ref, *, npg, r):
    cols = r // LANES
    nvp = npg * cols
    for i in range(npg):
        for c in range(cols):
            row0 = (i * cols + c) * CMP_PITCH
            kr_ref[row0:row0 + LANES, :] = k_ref[i, :, c * LANES:(c + 1) * LANES].astype(BF16).T.astype(F32)
            vr_ref[row0:row0 + LANES, :] = v_ref[i, :, c * LANES:(c + 1) * LANES].astype(BF16).T.astype(F32)
    nj = LANES // CMP_BLOCK
    acc_k = jnp.zeros((nj * nvp, D_KV), F32)
    acc_v = jnp.zeros((nj * nvp, D_KV), F32)
    for l in range(CMP_BLOCK):
        rows_k = [kr_ref[pl.ds(j * CMP_BLOCK + l, nvp, stride=CMP_PITCH), :] for j in range(nj)]
        rows_v = [vr_ref[pl.ds(j * CMP_BLOCK + l, nvp, stride=CMP_PITCH), :] for j in range(nj)]
        acc_k = acc_k + _mm(jnp.concatenate(rows_k, axis=0), wk_ref[l])
        acc_v = acc_v + _mm(jnp.concatenate(rows_v, axis=0), wv_ref[l])
    for j in range(nj):
        ko_ref[pl.ds(j, nvp, stride=nj), :] = acc_k[j * nvp:(j + 1) * nvp]
        vo_ref[pl.ds(j, nvp, stride=nj), :] = acc_v[j * nvp:(j + 1) * nvp]


def _compress(k, v, wk, wv):
    nl, pages, _, r = k.shape
    npg = next(c for c in range(min(max(CMP_ROWS // r, 1), pages), 0, -1) if pages % c == 0)
    nblk = npg * r // CMP_BLOCK
    rspec = pl.BlockSpec((None, npg, D_KV, r), lambda l, i: (l, i, 0, 0))
    wspec = pl.BlockSpec((None, CMP_BLOCK, D_KV, D_KV), lambda l, i: (l, 0, 0, 0))
    ospec = pl.BlockSpec((None, nblk, D_KV), lambda l, i: (l, i, 0))
    return pl.pallas_call(
        functools.partial(_compress_body, npg=npg, r=r),
        grid=(nl, pages // npg),
        in_specs=[rspec, rspec, wspec, wspec],
        out_specs=[ospec, ospec],
        out_shape=[jax.ShapeDtypeStruct((nl, pages * r // CMP_BLOCK, D_KV), F32)] * 2,
        scratch_shapes=[pltpu.VMEM((npg * (r // LANES) * CMP_PITCH, D_KV), F32)] * 2,
        compiler_params=pltpu.CompilerParams(dimension_semantics=("arbitrary", "arbitrary"),
                                             vmem_limit_bytes=VMEM_LIMIT),
        name="compress",
    )(k, v, wk, wv)


def _make_qbd(q, tq):
    lo = _iota((tq, LANES), 1) < HEAD_DIM
    blocks = {}
    for jv in range(D_ATTN // LANES):
        a = q[:, LANES * jv:LANES * (jv + 1)] * ATTN_SCALE
        r = pltpu.roll(a, HEAD_DIM, 1)
        for half in range(2):
            h = 2 * jv + half
            kv, g = h // GROUP, h % GROUP
            if kv == 0:
                blocks[(g, kv)] = jnp.where(lo, a if half == 0 else r, 0.0)
            else:
                blocks[(g, kv)] = jnp.where(lo, 0.0, a if half == 1 else r)
    return jnp.concatenate([blocks[(g, kv)] for g in range(GROUP) for kv in range(N_KV)], axis=0).astype(BF16)


def _extract_heads(o, tq):
    lo = _iota((tq, LANES), 1) < HEAD_DIM
    cols = []
    for jv in range(D_ATTN // LANES):
        parts = []
        for half in range(2):
            h = 2 * jv + half
            kv, g = h // GROUP, h % GROUP
            blk = o[(g * N_KV + kv) * tq:(g * N_KV + kv + 1) * tq]
            if (half == 1) != (kv == 1):
                blk = pltpu.roll(blk, HEAD_DIM, 1)
            parts.append(blk)
        cols.append(jnp.where(lo, parts[0], parts[1]))
    return jnp.concatenate(cols, axis=1)


def _gate_rows(gate, c, tq):
    return jnp.concatenate([gate[:, c * N_HEADS + kv * GROUP + g:c * N_HEADS + kv * GROUP + g + 1]
                            for g in range(GROUP) for kv in range(N_KV)], axis=0)


def _softmax_rows(s, valid):
    s = jnp.where(valid, s, NEG_INF)
    m = jnp.max(s, axis=-1, keepdims=True)
    e = jnp.exp(s - m)
    p = e / jnp.sum(e, axis=-1, keepdims=True)
    return jnp.where(valid, p, 0.0)


def _select_blocks(imp, qpos2, blkf, n_valid_imp, k, axis):
    cur = (qpos2 // SEL_BLOCK).astype(F32)
    imp = jnp.where(blkf < n_valid_imp, imp, 0.0)
    score = jnp.where(blkf == cur, 2.0 * GROUP, jnp.where(blkf < cur, imp, -1.0))
    score = jnp.where(blkf < 0.0, -3.0, score)
    sel = jnp.zeros(score.shape, F32)
    for _ in range(k):
        m = jnp.max(score, axis=axis, keepdims=True)
        idx = jnp.min(jnp.where(score == m, blkf, 1e9), axis=axis, keepdims=True)
        hit = blkf == idx
        sel = jnp.where(hit, 1.0, sel)
        score = jnp.where(hit, -3.0, score)
    return sel


def _sum_groups(p, rows2):
    out = p[0:rows2]
    for g in range(1, GROUP):
        out = out + p[g * rows2:(g + 1) * rows2]
    return out


def _attn_prompt_body(q_ref, gate_ref, kc_ref, vc_ref, ks_ref, vs_ref, kw_ref, vw_ref,
                      tcol_ref, qs_ref, blkc_ref, trow_ref, e_ref, pos_ref, cpos_ref, gexp_ref, o_ref,
                      kas_ref, vas_ref, kaw_ref, vaw_ref, ms_ref, as_ref, mw_ref, aw_ref, *, tq, tk, seq, top_k):
    qt = pl.program_id(1)
    q0 = qt * tq
    rows = N_HEADS * tq
    rows2 = N_KV * tq
    nt = seq // tk

    @pl.when(qt == 0)
    def _():
        ones = jnp.ones((D_KV, tk), BF16)
        for j in range(nt):
            cs = slice(j * tk, (j + 1) * tk)
            for src, dst in ((ks_ref, kas_ref), (kw_ref, kaw_ref)):
                dst[j, 0:D_KV, :] = src[0, :, cs].astype(BF16)
                dst[j, D_KV:2 * D_KV, :] = pos_ref[j]
            for src, dst in ((vs_ref, vas_ref), (vw_ref, vaw_ref)):
                dst[j, 0:D_KV, :] = src[0, :, cs].astype(BF16)
                dst[j, D_KV:2 * D_KV, :] = ones

    qbd = _make_qbd(q_ref[0], tq)
    qaug = jnp.concatenate([qbd, qs_ref[...]], axis=1)
    qpos2 = q0 + tcol_ref[...]

    g_hi = gate_ref[0].astype(BF16)
    g_r1 = gate_ref[0] - g_hi.astype(F32)
    g_mid = g_r1.astype(BF16)
    g_lo = (g_r1 - g_mid.astype(F32)).astype(BF16)
    g_all = (jnp.dot(jnp.concatenate([g_hi, g_mid], axis=1), gexp_ref[...], preferred_element_type=F32)
             + jnp.dot(g_lo, gexp_ref[0:LANES, :], preferred_element_type=F32))

    def gate_rows(c):
        return jnp.concatenate([g_all[:, (c * N_HEADS + bi) * LANES:(c * N_HEADS + bi + 1) * LANES]
                                for bi in range(N_HEADS)], axis=0)

    ncb = seq // CMP_BLOCK
    pad = jnp.zeros((LANES - ncb, D_KV), F32)
    kc = jnp.concatenate([kc_ref[0], pad], axis=0)
    vc = jnp.concatenate([vc_ref[0], pad], axis=0)
    kcaug = jnp.concatenate([kc.T.astype(BF16), cpos_ref[...]], axis=0)
    s_c = jnp.dot(qaug, kcaug, preferred_element_type=F32).reshape(GROUP, rows2, LANES)
    blk_end = (_iota((1, LANES), 1) + 1) * CMP_BLOCK - 1
    p_c = _softmax_rows(s_c, (qpos2 >= blk_end)[None]).reshape(rows, LANES)
    o_acc = gate_rows(0) * _mm(p_c, vc)

    p4 = _sum_groups(p_c, rows2)
    nbr = blkc_ref.shape[0]
    imp_t = (p4 + pltpu.roll(p4, LANES - 1, 1)).T[0:nbr]
    sel = _select_blocks(imp_t, q0 + trow_ref[...], blkc_ref[...], float(seq // SEL_BLOCK), top_k, 0).T
    sel = sel.astype(BF16)

    def make_step(kaug_ref, vaug_ref, mask_fn, m_ref, acc_ref):
        def step(j, carry):
            s = jnp.dot(qaug, kaug_ref[j], preferred_element_type=F32).reshape(GROUP, rows2, tk)
            dist = qpos2 - (j * tk + _iota((1, tk), 1))
            s = jnp.where(mask_fn(j, dist)[None], s, NEG_INF)
            m = m_ref[...]
            m_new = jnp.maximum(m, jnp.max(s, axis=2, keepdims=True))
            alpha = jnp.exp(m - m_new).reshape(rows, LANES)
            p = jnp.exp(s - jnp.concatenate([m_new] * (tk // LANES), axis=2)).astype(BF16).reshape(rows, tk)
            pv = lax.dot_general(p, vaug_ref[j], (((1,), (1,)), ((), ())), preferred_element_type=F32)
            m_ref[...] = m_new
            acc_ref[...] = jnp.concatenate([alpha, alpha], axis=1) * acc_ref[...] + pv
            return carry

        return step

    def sel_mask(j, dist):
        picked = jnp.dot(sel, e_ref[j], preferred_element_type=F32)
        return (picked > 0.5) & (dist >= 0)

    def win_mask(j, dist):
        return (dist >= 0) & (dist <= WINDOW)

    step_s = make_step(kas_ref, vas_ref, sel_mask, ms_ref, as_ref)
    step_w = make_step(kaw_ref, vaw_ref, win_mask, mw_ref, aw_ref)
    for m_ref, acc_ref in ((ms_ref, as_ref), (mw_ref, aw_ref)):
        m_ref[...] = jnp.full((GROUP, rows2, LANES), NEG_INF, F32)
        acc_ref[...] = jnp.zeros((rows, 2 * D_KV), F32)
    n_s = (q0 + tq - 1) // tk + 1
    j_lo = jnp.maximum(q0 - WINDOW, 0) // tk
    n_w = n_s - j_lo

    def both(i, carry):
        step_s(i, carry)
        return step_w(j_lo + i, carry)

    lax.fori_loop(0, n_w, both, 0)
    lax.fori_loop(n_w, n_s, step_s, 0)
    o_s = as_ref[:, 0:D_KV] / as_ref[:, D_KV:2 * D_KV]
    o_w = aw_ref[:, 0:D_KV] / aw_ref[:, D_KV:2 * D_KV]
    o_acc = o_acc + gate_rows(1) * o_s + gate_rows(2) * o_w
    o_ref[0] = _extract_heads(o_acc, tq)


def _row_consts(tq):
    r = jnp.arange(N_HEADS * tq)
    g, kv, t = r // (N_KV * tq), (r // tq) % N_KV, r % tq
    head = kv * GROUP + g
    slopes = jnp.exp2(-8.0 * (head + 1).astype(F32) / N_HEADS)
    return slopes[:, None], t.astype(jnp.int32)[:, None]


def _gate_expander():
    n = jnp.arange(N_BRANCH * N_HEADS)
    c, bi = n // N_HEADS, n % N_HEADS
    src_lane = c * N_HEADS + (bi % N_KV) * GROUP + bi // N_KV
    r = (jnp.arange(LANES)[:, None] == jnp.repeat(src_lane, LANES)[None, :]).astype(BF16)
    return jnp.concatenate([r, r], axis=0)


def _key_tiles(a, tk):
    return a.reshape(a.shape[0], -1, tk).transpose(1, 0, 2)


def _attn_prompt(q, gate, kcmp, vcmp, ks, vs, kw, vw):
    b, t, _ = q.shape
    tq = min(ATTN_TQ, t)
    tk = min(ATTN_TK, t)
    n_sb = t // SEL_BLOCK
    assert t // POS_SPLIT < 256
    slopes, tcol = _row_consts(tq)
    qs = jnp.zeros((N_HEADS * tq, LANES), F32).at[:, 0].set(POS_SPLIT * slopes[:, 0]).at[:, 1].set(slopes[:, 0])
    kpos = jnp.arange(t)
    pos = jnp.zeros((LANES, t), F32).at[0].set(kpos // POS_SPLIT).at[1].set(kpos % POS_SPLIT)
    lane = jnp.arange(LANES)
    blk = jnp.where((lane % 2 == 0) & (lane < 2 * n_sb), lane // 2, -1).astype(F32)
    e = (blk[:, None] == (kpos // SEL_BLOCK).astype(F32)[None, :])
    blk_end = (lane + 1) * CMP_BLOCK - 1
    cpos = jnp.zeros((LANES, LANES), F32).at[0].set(blk_end // POS_SPLIT).at[1].set(blk_end % POS_SPLIT)
    tcol2 = tcol[:N_KV * tq]
    nbr = min(LANES, -(-2 * n_sb // SUBLANES) * SUBLANES)
    consts = (tcol2, qs.astype(BF16), blk[:nbr, None], tcol2.reshape(1, -1),
              _key_tiles(e[:nbr].astype(BF16), tk), _key_tiles(pos.astype(BF16), tk), cpos.astype(BF16),
              _gate_expander())
    tile = lambda n: pl.BlockSpec((1, tq, n), lambda i, j: (i, j, 0))
    seq = lambda r, n: pl.BlockSpec((1, r, n), lambda i, j: (i, 0, 0))
    return pl.pallas_call(
        functools.partial(_attn_prompt_body, tq=tq, tk=tk, seq=t, top_k=min(TOP_K, n_sb)),
        grid=(b, t // tq),
        in_specs=[tile(D_ATTN), tile(LANES), seq(t // CMP_BLOCK, D_KV), seq(t // CMP_BLOCK, D_KV),
                  seq(D_KV, t), seq(D_KV, t), seq(D_KV, t), seq(D_KV, t)] + [_full(c.shape) for c in consts],
        out_specs=tile(D_ATTN),
        out_shape=jax.ShapeDtypeStruct((b, t, D_ATTN), F32),
        scratch_shapes=[pltpu.VMEM((t // tk, 2 * D_KV, tk), BF16)] * 4
        + [pltpu.VMEM((GROUP, N_KV * tq, LANES), F32), pltpu.VMEM((N_HEADS * tq, 2 * D_KV), F32)] * 2,
        compiler_params=pltpu.CompilerParams(dimension_semantics=("arbitrary", "arbitrary"),
                                             vmem_limit_bytes=VMEM_LIMIT),
        name="attn_prompt",
    )(q, gate, kcmp, vcmp, ks, vs, kw, vw, *consts)


def _attn_sample_body(pt_ref, q_ref, gate_ref, ksn_ref, vsn_ref, kwn_ref, vwn_ref, kwin_ref, vwin_ref,
                      kcp_ref, vcp_ref, cvalid_ref, cend_ref, slope_ref, tcol_ref, blkf_ref, e_ref, *rest,
                      t, n_pages, page, past, w_buf, top_k, nbs):
    ks_pages = rest[:nbs * n_pages]
    vs_pages = rest[nbs * n_pages:2 * nbs * n_pages]
    o_ref, kwo_ref, vwo_ref = rest[2 * nbs * n_pages:]
    b0 = pl.program_id(0) * nbs
    rows = N_HEADS * t
    rows2 = N_KV * t
    seqs = range(nbs)

    def stack(xs):
        return jnp.concatenate(xs, axis=0)

    def part(x, bb, n):
        return x[bb * n:(bb + 1) * n]

    def pad_rows(x):
        if x.shape[0] == LANES:
            return x
        return jnp.concatenate([x, jnp.zeros((LANES - x.shape[0], D_KV), F32)], axis=0)

    slope = stack([slope_ref[...]] * nbs)
    qpos = stack([past + tcol_ref[...]] * nbs)
    qpos2 = stack([past + tcol_ref[0:rows2, :]] * nbs)
    qbd = [_make_qbd(q_ref[bb], t) for bb in seqs]
    gates = [gate_ref[bb] for bb in seqs]

    def gate_rows(c):
        return stack([_gate_rows(gates[bb], c, t) for bb in seqs])

    kc = [pad_rows(stack([kcp_ref[pt_ref[b0 + bb, p] // 2] for p in range(n_pages)])) for bb in seqs]
    vc = [pad_rows(stack([vcp_ref[pt_ref[b0 + bb, p] // 2] for p in range(n_pages)])) for bb in seqs]
    dist_c = qpos - cend_ref[...]
    s_c = stack([_mm_nt(qbd[bb], kc[bb]) for bb in seqs]) - slope * dist_c.astype(F32)
    cvalid = stack([jnp.broadcast_to(cvalid_ref[bb], (rows, LANES)) for bb in seqs])
    p_c = _softmax_rows(s_c, (cvalid > 0.5) & (dist_c >= 0))
    o_acc = gate_rows(0) * stack([_mm(part(p_c, bb, rows), vc[bb]) for bb in seqs])

    p4 = stack([_sum_groups(part(p_c, bb, rows), rows2) for bb in seqs])
    x1 = p4 + pltpu.roll(p4, LANES - 1, 1)
    imp = x1 + pltpu.roll(x1, LANES - 4, 1)
    sel = _select_blocks(imp, qpos2, blkf_ref[...], float(past // SEL_BLOCK), top_k, 1)
    sel4 = stack([part(sel, bb, rows2) for bb in seqs for _ in range(GROUP)])

    s_s = stack([jnp.concatenate([_mm(qbd[bb], ks_pages[bb * n_pages + p][...]) for p in range(n_pages)]
                                 + [_mm_nt(qbd[bb], pad_rows(ksn_ref[bb]))], axis=1) for bb in seqs])
    dist_s = qpos - _iota((1, past + LANES), 1)
    picked = _mm(sel4, e_ref[...])
    p_s = _softmax_rows(s_s - slope * dist_s.astype(F32), (picked > 0.5) & (dist_s >= 0))
    o_s = []
    for bb in seqs:
        pb = part(p_s, bb, rows)
        acc = _mm(pb[:, past:], pad_rows(vsn_ref[bb]))
        for p in range(n_pages):
            acc = acc + _mm_nt(pb[:, p * page:(p + 1) * page], vs_pages[bb * n_pages + p][...])
        o_s.append(acc)
    o_acc = o_acc + gate_rows(1) * stack(o_s)

    kwin = [kwin_ref[bb] for bb in seqs]
    vwin = [vwin_ref[bb] for bb in seqs]
    s_w = stack([jnp.concatenate([_mm(qbd[bb], kwin[bb]), _mm_nt(qbd[bb], pad_rows(kwn_ref[bb]))], axis=1)
                 for bb in seqs])
    dist_w = qpos - (past - w_buf + _iota((1, w_buf + LANES), 1))
    p_w = _softmax_rows(s_w - slope * dist_w.astype(F32), (dist_w >= 0) & (dist_w <= WINDOW))
    o_w = stack([_mm_nt(part(p_w, bb, rows)[:, :w_buf], vwin[bb])
                 + _mm(part(p_w, bb, rows)[:, w_buf:], pad_rows(vwn_ref[bb])) for bb in seqs])
    o_acc = o_acc + gate_rows(2) * o_w

    is_new = _iota((D_KV, LANES), 1) >= LANES - t

    def shift_in(win, new):
        new_t = jnp.concatenate([jnp.zeros((LANES - t, D_KV), F32), new], axis=0).T
        rolled = pltpu.roll(win, w_buf - t, 1)
        last = jnp.where(is_new, new_t, rolled[:, w_buf - LANES:])
        return jnp.concatenate([rolled[:, :w_buf - LANES], last], axis=1)

    for bb in seqs:
        o_ref[bb] = _extract_heads(part(o_acc, bb, rows), t)
        kwo_ref[bb] = shift_in(kwin[bb], kwn_ref[bb])
        vwo_ref[bb] = shift_in(vwin[bb], vwn_ref[bb])


def _page_map(l, nbs, bb, p, i, pt):
    return (l, pt[i * nbs + bb, p], 0, 0)


def _attn_sample(l, page_table, q, gate, ks_new, vs_new, kw_new, vw_new, kwin, vwin, kcp, vcp,
                 cache_ks, cache_vs):
    nb, t, _ = q.shape
    n_pages = page_table.shape[1]
    page = cache_ks.shape[3]
    past = n_pages * page
    w_buf = kwin.shape[3]
    n_sb = -(-(past + t) // SEL_BLOCK)
    assert t == SUBLANES and n_pages * 8 <= LANES and page == LANES and past % SEL_BLOCK == 0
    assert w_buf % LANES == 0
    slopes, tcol = _row_consts(t)
    lane = jnp.arange(LANES)
    pg, r = lane // 8, lane % 8
    in_range = pg < n_pages
    cend = jnp.where(in_range, (4 * pg + r % 4 + 1) * CMP_BLOCK - 1, 1 << 30).astype(jnp.int32)[None, :]
    par = page_table % 2
    par_l = jnp.take(par, jnp.minimum(pg, n_pages - 1), axis=1)
    cvalid = ((r[None, :] // 4 == par_l) & in_range[None, :]).astype(F32)[:, None, :]
    blk = jnp.where(in_range & (r % 2 == 0) & (r < 4), 2 * pg + r // 2, -1)
    blk = jnp.where(lane == 1, n_sb - 1, blk)
    blkf = blk.astype(F32)[None, :]
    e = (blkf.reshape(-1, 1) == (jnp.arange(past + LANES) // SEL_BLOCK).astype(F32)[None, :]).astype(BF16)
    nbs = next(c for c in range(min(DEC_STEP, nb), 0, -1) if nb % c == 0)
    tok = lambda n: pl.BlockSpec((nbs, t, n), lambda i, pt: (i, 0, 0))
    win = pl.BlockSpec((None, nbs, D_KV, w_buf), lambda i, pt: (l, i, 0, 0))
    win_out = pl.BlockSpec((nbs, D_KV, w_buf), lambda i, pt: (i, 0, 0))
    res = pl.BlockSpec((None,) + kcp.shape[1:], lambda i, pt: (l, 0, 0, 0))
    const = lambda a: pl.BlockSpec(a.shape, lambda i, pt: (0,) * a.ndim)
    pages = [pl.BlockSpec((None, None, D_KV, page), functools.partial(_page_map, l, nbs, bb, p))
             for bb in range(nbs) for p in range(n_pages)]
    grid_spec = pltpu.PrefetchScalarGridSpec(
        num_scalar_prefetch=1,
        grid=(nb // nbs,),
        in_specs=[tok(D_ATTN), tok(LANES), tok(D_KV), tok(D_KV), tok(D_KV), tok(D_KV), win, win, res, res,
                  pl.BlockSpec((nbs, 1, LANES), lambda i, pt: (i, 0, 0)),
                  const(cend), const(slopes), const(tcol), const(blkf), const(e)] + pages + pages,
        out_specs=[tok(D_ATTN), win_out, win_out],
    )
    return pl.pallas_call(
        functools.partial(_attn_sample_body, t=t, n_pages=n_pages, page=page, past=past, w_buf=w_buf,
                          top_k=min(TOP_K, n_sb), nbs=nbs),
        grid_spec=grid_spec,
        out_shape=[jax.ShapeDtypeStruct((nb, t, D_ATTN), F32),
                   jax.ShapeDtypeStruct((nb, D_KV, w_buf), F32), jax.ShapeDtypeStruct((nb, D_KV, w_buf), F32)],
        compiler_params=pltpu.CompilerParams(dimension_semantics=("arbitrary",), vmem_limit_bytes=VMEM_LIMIT),
        name="attn_sample",
    )(page_table, q, gate, ks_new, vs_new, kw_new, vw_new, kwin, vwin, kcp, vcp, cvalid,
      cend, slopes, tcol, blkf, e, *([cache_ks] * (nbs * n_pages)), *([cache_vs] * (nbs * n_pages)))


def _back_head(x, o, pab, wg2_ref, woa_ref, wmix_ref, ln1g_ref, ln1b_ref):
    y_c = _mm(o, woa_ref[...])
    g2 = jax.nn.sigmoid(_mm(x, wg2_ref[...]))
    mixed = pab + g2 * y_c
    return _layer_norm(ALPHA * x + _mm(mixed, wmix_ref[...]), ln1g_ref[...], ln1b_ref[...])


def _back_prompt_body(x_ref, o_ref, pab_ref, wg2_ref, woa_ref, wmix_ref, ln1g_ref, ln1b_ref,
                      wup_ref, fcw_ref, fcb_ref, wdn_ref, ln2g_ref, ln2b_ref,
                      y_ref, fst_ref, ext_ref, *, tt):
    j = pl.program_id(1)

    @pl.when(j == 0)
    def _():
        ext_ref[0:8, :] = jnp.zeros((8, D_FF), F32)

    h = _back_head(x_ref[0], o_ref[0], pab_ref[0], wg2_ref, woa_ref, wmix_ref, ln1g_ref, ln1b_ref)
    hb = h.astype(BF16)
    cf = D_FF // FF_CHUNKS
    f = jnp.zeros((tt, D_MODEL), F32)
    for c in range(FF_CHUNKS):
        cs = slice(c * cf, (c + 1) * cf)
        u = _mm(hb, wup_ref[:, c * cf:(c + 1) * cf])
        gt = _mm(hb, wup_ref[:, D_FF + c * cf:D_FF + (c + 1) * cf])
        ext_ref[8:8 + tt, cs] = u
        conv = fcb_ref[:, cs] + fcw_ref[0:1, cs] * ext_ref[6:6 + tt, cs]
        conv = conv + fcw_ref[1:2, cs] * ext_ref[7:7 + tt, cs]
        conv = conv + fcw_ref[2:3, cs] * u
        f = f + _mm(jax.nn.gelu(conv) * gt, wdn_ref[c * cf:(c + 1) * cf, :])
    y_ref[0] = _layer_norm(ALPHA * h + f, ln2g_ref[...], ln2b_ref[...])
    last = ext_ref[tt:tt + 8, :]
    fst_ref[0] = last
    ext_ref[0:8, :] = last


def _back_sample_body(x_ref, o_ref, pab_ref, hf_ref, wg2_ref, woa_ref, wmix_ref, ln1g_ref, ln1b_ref,
                      wup_ref, fcw_ref, fcb_ref, wdn_ref, ln2g_ref, ln2b_ref,
                      y_ref, fst_ref, ext_ref, *, ns, t):
    h = _back_head(x_ref[...], o_ref[...], pab_ref[...], wg2_ref, woa_ref, wmix_ref, ln1g_ref, ln1b_ref)
    hb = h.astype(BF16)
    cf = D_FF // FF_CHUNKS
    ext_ref[:, 6:8, :] = hf_ref[...]
    f = jnp.zeros((ns * t, D_MODEL), F32)
    for c in range(FF_CHUNKS):
        cs = slice(c * cf, (c + 1) * cf)
        u = _mm(hb, wup_ref[:, c * cf:(c + 1) * cf])
        gt = _mm(hb, wup_ref[:, D_FF + c * cf:D_FF + (c + 1) * cf])
        ext_ref[:, 8:8 + t, cs] = u.reshape(ns, t, cf)
        fcw = fcw_ref[:, cs]
        conv = fcb_ref[:, cs] + fcw[0:1] * ext_ref[:, 6:6 + t, cs]
        conv = conv + fcw[1:2] * ext_ref[:, 7:7 + t, cs]
        conv = conv + fcw[2:3] * ext_ref[:, 8:8 + t, cs]
        f = f + _mm(jax.nn.gelu(conv).reshape(ns * t, cf) * gt, wdn_ref[c * cf:(c + 1) * cf, :])
    y_ref[...] = _layer_norm(ALPHA * h + f, ln2g_ref[...], ln2b_ref[...])
    fst_ref[...] = ext_ref[:, t:t + 8, :]


def _back_prompt(x, o, pab, wts):
    b, t, _ = x.shape
    tt = min(FRONT_TILE, t)
    tok = lambda n: pl.BlockSpec((1, tt, n), lambda i, j: (i, j, 0))
    return pl.pallas_call(
        functools.partial(_back_prompt_body, tt=tt),
        grid=(b, t // tt),
        in_specs=[tok(D_MODEL), tok(D_ATTN), tok(D_MODEL)] + [_full(w.shape) for w in wts],
        out_specs=[tok(D_MODEL), pl.BlockSpec((1, 8, D_FF), lambda i, j: (i, 0, 0))],
        out_shape=[jax.ShapeDtypeStruct((b, t, D_MODEL), F32), jax.ShapeDtypeStruct((b, 8, D_FF), F32)],
        scratch_shapes=[pltpu.VMEM((8 + tt, D_FF), F32)],
        compiler_params=pltpu.CompilerParams(dimension_semantics=("arbitrary", "arbitrary"),
                                             vmem_limit_bytes=VMEM_LIMIT),
        name="back_prompt",
    )(x, o, pab, *wts)


def _back_sample(x, o, pab, hist_f, wts, t):
    n = x.shape[0]
    nb = n // t
    ns = min(FRONT_TILE // t, nb)
    rows = ns * t
    tok = lambda c: pl.BlockSpec((rows, c), lambda i: (i, 0))
    return pl.pallas_call(
        functools.partial(_back_sample_body, ns=ns, t=t),
        grid=(nb // ns,),
        in_specs=[tok(D_MODEL), tok(D_ATTN), tok(D_MODEL),
                  pl.BlockSpec((ns, CONV_W - 1, D_FF), lambda i: (i, 0, 0))] + [_full(w.shape) for w in wts],
        out_specs=[tok(D_MODEL), pl.BlockSpec((ns, 8, D_FF), lambda i: (i, 0, 0))],
        out_shape=[jax.ShapeDtypeStruct((n, D_MODEL), F32), jax.ShapeDtypeStruct((nb, 8, D_FF), F32)],
        scratch_shapes=[pltpu.VMEM((ns, 8 + t, D_FF), F32)],
        compiler_params=pltpu.CompilerParams(dimension_semantics=("arbitrary",), vmem_limit_bytes=VMEM_LIMIT),
        name="back_sample",
    )(x, o, pab, hist_f, *wts)


def _block_diag2(w):
    z = jnp.zeros_like(w)
    return jnp.concatenate([jnp.concatenate([w, z], axis=-1), jnp.concatenate([z, w], axis=-1)], axis=-2)


def kernel(x_prompt, x_sample, cache_k_cmp, cache_v_cmp, cache_k_sel, cache_v_sel, state_k_win, state_v_win, state_conv, state_pool, state_ffn_conv, page_table, w_in, conv_w, conv_b, w_out_conv, pool_w, pool_scale, w_cmp_k, w_cmp_v, w_out_attn, w_mix_out, ln1_g, ln1_b, w_ffn_up, ffn_conv_w, ffn_conv_b, w_ffn_down, ln2_g, ln2_b):
    depth = w_in.shape[0]
    bp, seq, _ = x_prompt.shape
    nb, t, _ = x_sample.shape
    n_phys, page = cache_k_cmp.shape[1], cache_k_cmp.shape[2]
    past = page_table.shape[1] * page
    w_buf = state_k_win.shape[2]

    w_front = jnp.concatenate(
        [w_in[:, :, :IN_GATES], jnp.zeros((depth, D_MODEL, C_G01 - IN_GATES), F32),
         w_in[:, :, IN_GATES:IN_GATES + 2 * D_MODEL]], axis=2).astype(BF16)
    w_g2 = w_in[:, :, IN_GATES + 2 * D_MODEL:].astype(BF16)
    ng = pool_w.shape[1]
    pool_bd = jnp.zeros((depth, D_POOL, D_MODEL), F32)
    for g in range(ng):
        pool_bd = pool_bd.at[:, g * POOL_GROUP:(g + 1) * POOL_GROUP,
                             g * (D_MODEL // ng):(g + 1) * (D_MODEL // ng)].set(pool_w[:, g])
    pool_bd = pool_bd.astype(BF16)
    wck = _block_diag2(w_cmp_k).astype(BF16)
    wcv = _block_diag2(w_cmp_v).astype(BF16)
    w_oc, w_oa, w_mix = w_out_conv.astype(BF16), w_out_attn.astype(BF16), w_mix_out.astype(BF16)
    w_up, w_dn = w_ffn_up.astype(BF16), w_ffn_down.astype(BF16)
    row = lambda a: a[:, None, :]

    keys_minor = lambda a: jnp.transpose(a, (0, 1, 3, 4, 2)).reshape(a.shape[0], a.shape[1], D_KV, a.shape[2])
    tokens_major = lambda a: jnp.transpose(a.reshape(a.shape[0], N_KV, HEAD_DIM, a.shape[2]), (0, 3, 1, 2))
    ck_s, cv_s = keys_minor(cache_k_sel), keys_minor(cache_v_sel)
    kwin_all, vwin_all = keys_minor(state_k_win), keys_minor(state_v_win)
    kcp, vcp = _compress(keys_minor(cache_k_cmp), keys_minor(cache_v_cmp), wck, wcv)
    kcp = kcp.reshape(depth, n_phys // 2, 8, D_KV)
    vcp = vcp.reshape(depth, n_phys // 2, 8, D_KV)

    y_p = x_prompt
    y_s = x_sample.reshape(nb * t, D_MODEL)
    st_p, st_s = [], []
    for l in range(depth):
        fw = (w_front[l], conv_w[l], row(conv_b)[l], w_oc[l], pool_bd[l], row(pool_scale)[l])
        bw = (w_g2[l], w_oa[l], w_mix[l], row(ln1_g)[l], row(ln1_b)[l], w_up[l], ffn_conv_w[l],
              row(ffn_conv_b)[l], w_dn[l], row(ln2_g)[l], row(ln2_b)[l])

        pab, q, kc, vc, ks, vs, kw, vw, gate, cst, pst = _front_prompt(y_p, fw)
        kcmp, vcmp = _compress(kc[None], vc[None], wck[l:l + 1], wcv[l:l + 1])
        o = _attn_prompt(q, gate, kcmp.reshape(bp, seq // CMP_BLOCK, D_KV),
                         vcmp.reshape(bp, seq // CMP_BLOCK, D_KV), ks, vs, kw, vw)
        y_p, fst = _back_prompt(y_p, o, pab, bw)
        st_p.append(tuple(tokens_major(a) for a in (kc, vc, ks, vs, kw[:, :, seq - w_buf:], vw[:, :, seq - w_buf:]))
                    + (cst[:, 8 - (CONV_W - 1):], pst[:, 16 - POOL_HIST:], fst[:, 8 - (CONV_W - 1):]))

        pab, q, kc, vc, ks, vs, kw, vw, gate, cst, pst = _front_sample(
            y_s.reshape(nb, t, D_MODEL), state_conv[l], state_pool[l], fw, past)
        r3 = lambda a: a.reshape(nb, t, a.shape[-1])
        o, kwin, vwin = _attn_sample(l, page_table, r3(q), r3(gate), r3(ks), r3(vs), r3(kw), r3(vw),
                                     kwin_all, vwin_all, kcp, vcp, ck_s, cv_s)
        y_s, fst = _back_sample(y_s, o.reshape(nb * t, D_ATTN), pab, state_ffn_conv[l], bw, t)
        kv5 = lambda a: a.reshape(nb, t, N_KV, HEAD_DIM)
        st_s.append((kv5(kc), kv5(vc), kv5(ks), kv5(vs), tokens_major(kwin), tokens_major(vwin),
                     cst[:, 8 - (CONV_W - 1):], pst[:, 16 - POOL_HIST:], fst[:, 8 - (CONV_W - 1):]))

    kc_p, vc_p, ks_p, vs_p, kw_p, vw_p, conv_p, pool_p, ffn_p = [jnp.stack(a) for a in zip(*st_p)]
    kc_s, vc_s, ks_s, vs_s, kw_s, vw_s, conv_s, pool_s, ffn_s = [jnp.stack(a) for a in zip(*st_s)]
    return (y_p, y_s.reshape(nb, t, D_MODEL), kc_p, kc_s, vc_p, vc_s, ks_p, ks_s, vs_p, vs_s,
            kw_p, kw_s, vw_p, vw_s, conv_p, conv_s, pool_p, pool_s, ffn_p, ffn_s)
```

```python
import functools

import jax
import jax.numpy as jnp
from jax import lax
from jax.experimental import pallas as pl
from jax.experimental.pallas import tpu as pltpu

F32 = jnp.float32
BF16 = jnp.bfloat16

D_MODEL = 1024
DEPTH = 4
D_CONV = 256
CONV_W = 3
D_POOL = 256
POOL_WINDOWS = (2, 4, 8, 16)
POOL_GROUP = D_POOL // len(POOL_WINDOWS)
POOL_HIST = 15
N_HEADS = 8
N_KV = 2
GROUP = N_HEADS // N_KV
HEAD_DIM = 64
D_ATTN = N_HEADS * HEAD_DIM
D_KV = N_KV * HEAD_DIM
CMP_BLOCK = 32
SEL_BLOCK = 64
TOP_K = 8
WINDOW = 512
N_BRANCH = 3
D_FF = 2816
ALPHA = (2.0 * DEPTH) ** 0.25
LN_EPS = 1e-5
ATTN_SCALE = HEAD_DIM ** -0.5
NEG_INF = -1e30
POS_SPLIT = 64

LANES = 128
SUBLANES = 8
VMEM_LIMIT = 56 * 1024 * 1024

C_AB, C_AC, C_AH, C_U, C_Q = 0, D_CONV, 2 * D_CONV, 3 * D_CONV, 3 * D_CONV + D_POOL
C_KV = C_Q + D_ATTN
C_NSA = C_KV + 6 * D_KV
C_G01 = C_NSA + LANES
IN_MAIN = C_NSA
IN_GATES = IN_MAIN + N_BRANCH * N_HEADS

FRONT_TILE = 256
ATTN_TQ = 128
ATTN_TK = 256
CMP_ROWS = 8192
CMP_PITCH = LANES + SUBLANES
DEC_STEP = 4
FF_CHUNKS = 2


def _mm(a, b):
    return jnp.dot(a.astype(BF16), b.astype(BF16), preferred_element_type=F32)


def _mm_nt(a, b):
    return lax.dot_general(a.astype(BF16), b.astype(BF16), (((1,), (1,)), ((), ())),
                           preferred_element_type=F32)


def _layer_norm(x, g, b):
    mu = jnp.mean(x, axis=-1, keepdims=True)
    var = jnp.mean(jnp.square(x - mu), axis=-1, keepdims=True)
    return (x - mu) * lax.rsqrt(var + LN_EPS) * g + b


def _iota(shape, axis):
    return lax.broadcasted_iota(jnp.int32, shape, axis)


def _pool_delta(u, shifted, pos):
    nd = u.ndim
    acc = u
    sums = {}
    for j in range(1, POOL_WINDOWS[-1]):
        acc = acc + shifted(j)
        if j + 1 in POOL_WINDOWS:
            sums[j + 1] = acc
    grp = _iota((1,) * (nd - 1) + (D_POOL,), nd - 1) // POOL_GROUP
    s = sums[POOL_WINDOWS[-1]]
    win = jnp.full(grp.shape, POOL_WINDOWS[-1], jnp.int32)
    for g in range(len(POOL_WINDOWS) - 2, -1, -1):
        s = jnp.where(grp == g, sums[POOL_WINDOWS[g]], s)
        win = jnp.where(grp == g, POOL_WINDOWS[g], win)
    cnt = jnp.minimum(win, pos + 1).astype(F32)
    return s / cnt - u


def _front_tail(z, conv, d, woc_ref, pbd_ref, psc_ref, outs, kv_transposed):
    pab_ref, q_ref, kv_refs, gate_ref = outs
    y_a = _mm(z[:, C_AB:C_AB + D_CONV] * conv, woc_ref[...])
    y_b = _mm(d, pbd_ref[...]) * psc_ref[...]
    g0 = jax.nn.sigmoid(z[:, C_G01:C_G01 + D_MODEL])
    g1 = jax.nn.sigmoid(z[:, C_G01 + D_MODEL:C_G01 + 2 * D_MODEL])
    pab_ref[...] = (g0 * y_a + g1 * y_b).reshape(pab_ref.shape)
    q_ref[...] = z[:, C_Q:C_Q + D_ATTN].reshape(q_ref.shape)
    for i, r in enumerate(kv_refs):
        kv = z[:, C_KV + i * D_KV:C_KV + (i + 1) * D_KV]
        r[...] = (kv.T if kv_transposed else kv).reshape(r.shape)
    gate_ref[...] = jax.nn.sigmoid(z[:, C_NSA:C_NSA + LANES]).reshape(gate_ref.shape)


def _front_prompt_body(x_ref, w_ref, cw_ref, cb_ref, woc_ref, pbd_ref, psc_ref,
                       pab_ref, q_ref, kc_ref, vc_ref, ks_ref, vs_ref, kw_ref, vw_ref, gate_ref,
                       cst_ref, pst_ref, extc_ref, extu_ref, *, tt):
    j = pl.program_id(1)

    @pl.when(j == 0)
    def _():
        extc_ref[0:8, :] = jnp.zeros((8, D_CONV), F32)
        extu_ref[0:16, :] = jnp.zeros((16, D_POOL), F32)

    z = _mm(x_ref[0], w_ref[...])
    ch = z[:, C_AC:C_AC + D_CONV] * z[:, C_AH:C_AH + D_CONV]
    extc_ref[8:8 + tt, :] = ch
    conv = cb_ref[...] + cw_ref[0:1, :] * extc_ref[6:6 + tt, :]
    conv = conv + cw_ref[1:2, :] * extc_ref[7:7 + tt, :]
    conv = conv + cw_ref[2:3, :] * ch
    u = z[:, C_U:C_U + D_POOL]
    extu_ref[16:16 + tt, :] = u
    pos = j * tt + _iota((tt, 1), 0)
    d = _pool_delta(u, lambda s: extu_ref[16 - s:16 - s + tt, :], pos)
    _front_tail(z, conv, d, woc_ref, pbd_ref, psc_ref,
                (pab_ref, q_ref, (kc_ref, vc_ref, ks_ref, vs_ref, kw_ref, vw_ref), gate_ref), True)
    last_c = extc_ref[tt:tt + 8, :]
    last_u = extu_ref[tt:tt + 16, :]
    cst_ref[0] = last_c
    pst_ref[0] = last_u
    extc_ref[0:8, :] = last_c
    extu_ref[0:16, :] = last_u


def _front_sample_body(x_ref, hc_ref, hu_ref, w_ref, cw_ref, cb_ref, woc_ref, pbd_ref, psc_ref,
                       pab_ref, q_ref, kc_ref, vc_ref, ks_ref, vs_ref, kw_ref, vw_ref, gate_ref,
                       cst_ref, pst_ref, extc_ref, extu_ref, *, ns, t, pos0):
    z = _mm(x_ref[...], w_ref[...])
    ch = z[:, C_AC:C_AC + D_CONV] * z[:, C_AH:C_AH + D_CONV]
    extc_ref[:, 6:8, :] = hc_ref[...]
    extc_ref[:, 8:8 + t, :] = ch.reshape(ns, t, D_CONV)
    cw = cw_ref[...]
    conv = cb_ref[...] + cw[0:1] * extc_ref[:, 6:6 + t, :]
    conv = conv + cw[1:2] * extc_ref[:, 7:7 + t, :]
    conv = conv + cw[2:3] * extc_ref[:, 8:8 + t, :]
    u = z[:, C_U:C_U + D_POOL]
    extu_ref[:, 1:16, :] = hu_ref[...]
    extu_ref[:, 16:16 + t, :] = u.reshape(ns, t, D_POOL)
    pos = pos0 + _iota((1, t, 1), 1)
    d = _pool_delta(extu_ref[:, 16:16 + t, :], lambda s: extu_ref[:, 16 - s:16 - s + t, :], pos)
    _front_tail(z, conv.reshape(ns * t, D_CONV), d.reshape(ns * t, D_POOL), woc_ref, pbd_ref, psc_ref,
                (pab_ref, q_ref, (kc_ref, vc_ref, ks_ref, vs_ref, kw_ref, vw_ref), gate_ref), False)
    cst_ref[...] = extc_ref[:, t:t + 8, :]
    pst_ref[...] = extu_ref[:, t:t + 16, :]


def _full(shape):
    n = len(shape)
    return pl.BlockSpec(shape, lambda *_: (0,) * n, pipeline_mode=pl.Buffered(1))


def _front_prompt(x, wts):
    b, t, _ = x.shape
    tt = min(FRONT_TILE, t)
    tok = lambda n: pl.BlockSpec((1, tt, n), lambda i, j: (i, j, 0))
    st = lambda r, n: pl.BlockSpec((1, r, n), lambda i, j: (i, 0, 0))
    kvt = pl.BlockSpec((1, D_KV, tt), lambda i, j: (i, 0, j))
    out_shape = ([jax.ShapeDtypeStruct((b, t, D_MODEL), F32), jax.ShapeDtypeStruct((b, t, D_ATTN), F32)]
                 + [jax.ShapeDtypeStruct((b, D_KV, t), F32)] * 6
                 + [jax.ShapeDtypeStruct((b, t, LANES), F32),
                    jax.ShapeDtypeStruct((b, 8, D_CONV), F32), jax.ShapeDtypeStruct((b, 16, D_POOL), F32)])
    out_specs = ([tok(D_MODEL), tok(D_ATTN)] + [kvt] * 6 + [tok(LANES), st(8, D_CONV), st(16, D_POOL)])
    return pl.pallas_call(
        functools.partial(_front_prompt_body, tt=tt),
        grid=(b, t // tt),
        in_specs=[tok(D_MODEL)] + [_full(w.shape) for w in wts],
        out_specs=out_specs,
        out_shape=out_shape,
        scratch_shapes=[pltpu.VMEM((8 + tt, D_CONV), F32), pltpu.VMEM((16 + tt, D_POOL), F32)],
        compiler_params=pltpu.CompilerParams(dimension_semantics=("arbitrary", "arbitrary"),
                                             vmem_limit_bytes=VMEM_LIMIT),
        name="front_prompt",
    )(x, *wts)


def _front_sample(x, hist_c, hist_u, wts, pos0):
    nb, t, _ = x.shape
    ns = min(FRONT_TILE // t, nb)
    rows = ns * t
    xf = x.reshape(nb * t, D_MODEL)
    tok = lambda n: pl.BlockSpec((rows, n), lambda i: (i, 0))
    st = lambda r, n: pl.BlockSpec((ns, r, n), lambda i: (i, 0, 0))
    out_shape = ([jax.ShapeDtypeStruct((nb * t, D_MODEL), F32), jax.ShapeDtypeStruct((nb * t, D_ATTN), F32)]
                 + [jax.ShapeDtypeStruct((nb * t, D_KV), F32)] * 6
                 + [jax.ShapeDtypeStruct((nb * t, LANES), F32),
                    jax.ShapeDtypeStruct((nb, 8, D_CONV), F32), jax.ShapeDtypeStruct((nb, 16, D_POOL), F32)])
    out_specs = ([tok(D_MODEL), tok(D_ATTN)] + [tok(D_KV)] * 6 + [tok(LANES), st(8, D_CONV), st(16, D_POOL)])
    return pl.pallas_call(
        functools.partial(_front_sample_body, ns=ns, t=t, pos0=pos0),
        grid=(nb // ns,),
        in_specs=[tok(D_MODEL), st(CONV_W - 1, D_CONV), st(POOL_HIST, D_POOL)] + [_full(w.shape) for w in wts],
        out_specs=out_specs,
        out_shape=out_shape,
        scratch_shapes=[pltpu.VMEM((ns, 8 + t, D_CONV), F32), pltpu.VMEM((ns, 16 + t, D_POOL), F32)],
        compiler_params=pltpu.CompilerParams(dimension_semantics=("arbitrary",),
                                             vmem_limit_bytes=VMEM_LIMIT),
        name="front_sample",
    )(xf, hist_c, hist_u, *wts)


def _compress_body(k_ref, v_ref, wk_ref, wv_ref, ko_ref, vo_ref, kr_ref, vr_ref, *, npg, r):
    cols = r // LANES
    nvp = npg * cols
    for i in range(npg):
        for c in range(cols):
            row0 = (i * cols + c) * CMP_PITCH
            kr_ref[row0:row0 + LANES, :] = k_ref[i, :, c * LANES:(c + 1) * LANES].astype(BF16).T.astype(F32)
            vr_ref[row0:row0 + LANES, :] = v_ref[i, :, c * LANES:(c + 1) * LANES].astype(BF16).T.astype(F32)
    nj = LANES // CMP_BLOCK
    acc_k = jnp.zeros((nj * nvp, D_KV), F32)
    acc_v = jnp.zeros((nj * nvp, D_KV), F32)
    for l in range(CMP_BLOCK):
        rows_k = [kr_ref[pl.ds(j * CMP_BLOCK + l, nvp, stride=CMP_PITCH), :] for j in range(nj)]
        rows_v = [vr_ref[pl.ds(j * CMP_BLOCK + l, nvp, stride=CMP_PITCH), :] for j in range(nj)]
        acc_k = acc_k + _mm(jnp.concatenate(rows_k, axis=0), wk_ref[l])
        acc_v = acc_v + _mm(jnp.concatenate(rows_v, axis=0), wv_ref[l])
    for j in range(nj):
        ko_ref[pl.ds(j, nvp, stride=nj), :] = acc_k[j * nvp:(j + 1) * nvp]
        vo_ref[pl.ds(j, nvp, stride=nj), :] = acc_v[j * nvp:(j + 1) * nvp]


def _compress(k, v, wk, wv):
    nl, pages, _, r = k.shape
    npg = next(c for c in range(min(max(CMP_ROWS // r, 1), pages), 0, -1) if pages % c == 0)
    nblk = npg * r // CMP_BLOCK
    rspec = pl.BlockSpec((None, npg, D_KV, r), lambda l, i: (l, i, 0, 0))
    wspec = pl.BlockSpec((None, CMP_BLOCK, D_KV, D_KV), lambda l, i: (l, 0, 0, 0))
    ospec = pl.BlockSpec((None, nblk, D_KV), lambda l, i: (l, i, 0))
    return pl.pallas_call(
        functools.partial(_compress_body, npg=npg, r=r),
        grid=(nl, pages // npg),
        in_specs=[rspec, rspec, wspec, wspec],
        out_specs=[ospec, ospec],
        out_shape=[jax.ShapeDtypeStruct((nl, pages * r // CMP_BLOCK, D_KV), F32)] * 2,
        scratch_shapes=[pltpu.VMEM((npg * (r // LANES) * CMP_PITCH, D_KV), F32)] * 2,
        compiler_params=pltpu.CompilerParams(dimension_semantics=("arbitrary", "arbitrary"),
                                             vmem_limit_bytes=VMEM_LIMIT),
        name="compress",
    )(k, v, wk, wv)


def _make_qbd(q, tq):
    lo = _iota((tq, LANES), 1) < HEAD_DIM
    blocks = {}
    for jv in range(D_ATTN // LANES):
        a = q[:, LANES * jv:LANES * (jv + 1)] * ATTN_SCALE
        r = pltpu.roll(a, HEAD_DIM, 1)
        for half in range(2):
            h = 2 * jv + half
            kv, g = h // GROUP, h % GROUP
            if kv == 0:
                blocks[(g, kv)] = jnp.where(lo, a if half == 0 else r, 0.0)
            else:
                blocks[(g, kv)] = jnp.where(lo, 0.0, a if half == 1 else r)
    return jnp.concatenate([blocks[(g, kv)] for g in range(GROUP) for kv in range(N_KV)], axis=0).astype(BF16)


def _extract_heads(o, tq):
    lo = _iota((tq, LANES), 1) < HEAD_DIM
    cols = []
    for jv in range(D_ATTN // LANES):
        parts = []
        for half in range(2):
            h = 2 * jv + half
            kv, g = h // GROUP, h % GROUP
            blk = o[(g * N_KV + kv) * tq:(g * N_KV + kv + 1) * tq]
            if (half == 1) != (kv == 1):
                blk = pltpu.roll(blk, HEAD_DIM, 1)
            parts.append(blk)
        cols.append(jnp.where(lo, parts[0], parts[1]))
    return jnp.concatenate(cols, axis=1)


def _gate_rows(gate, c, tq):
    return jnp.concatenate([gate[:, c * N_HEADS + kv * GROUP + g:c * N_HEADS + kv * GROUP + g + 1]
                            for g in range(GROUP) for kv in range(N_KV)], axis=0)


def _softmax_rows(s, valid):
    s = jnp.where(valid, s, NEG_INF)
    m = jnp.max(s, axis=-1, keepdims=True)
    e = jnp.exp(s - m)
    p = e / jnp.sum(e, axis=-1, keepdims=True)
    return jnp.where(valid, p, 0.0)


def _select_blocks(imp, qpos2, blkf, n_valid_imp, k, axis):
    cur = (qpos2 // SEL_BLOCK).astype(F32)
    imp = jnp.where(blkf < n_valid_imp, imp, 0.0)
    score = jnp.where(blkf == cur, 2.0 * GROUP, jnp.where(blkf < cur, imp, -1.0))
    score = jnp.where(blkf < 0.0, -3.0, score)
    sel = jnp.zeros(score.shape, F32)
    for _ in range(k):
        m = jnp.max(score, axis=axis, keepdims=True)
        idx = jnp.min(jnp.where(score == m, blkf, 1e9), axis=axis, keepdims=True)
        hit = blkf == idx
        sel = jnp.where(hit, 1.0, sel)
        score = jnp.where(hit, -3.0, score)
    return sel


def _sum_groups(p, rows2):
    out = p[0:rows2]
    for g in range(1, GROUP):
        out = out + p[g * rows2:(g + 1) * rows2]
    return out


def _attn_prompt_body(q_ref, gate_ref, kc_ref, vc_ref, ks_ref, vs_ref, kw_ref, vw_ref,
                      tcol_ref, qs_ref, blkc_ref, trow_ref, e_ref, pos_ref, cpos_ref, gexp_ref, o_ref,
                      kas_ref, vas_ref, kaw_ref, vaw_ref, ms_ref, as_ref, mw_ref, aw_ref, *, tq, tk, seq, top_k):
    qt = pl.program_id(1)
    q0 = qt * tq
    rows = N_HEADS * tq
    rows2 = N_KV * tq
    nt = seq // tk

    @pl.when(qt == 0)
    def _():
        ones = jnp.ones((D_KV, tk), BF16)
        for j in range(nt):
            cs = slice(j * tk, (j + 1) * tk)
            for src, dst in ((ks_ref, kas_ref), (kw_ref, kaw_ref)):
                dst[j, 0:D_KV, :] = src[0, :, cs].astype(BF16)
                dst[j, D_KV:2 * D_KV, :] = pos_ref[j]
            for src, dst in ((vs_ref, vas_ref), (vw_ref, vaw_ref)):
                dst[j, 0:D_KV, :] = src[0, :, cs].astype(BF16)
                dst[j, D_KV:2 * D_KV, :] = ones

    qbd = _make_qbd(q_ref[0], tq)
    qaug = jnp.concatenate([qbd, qs_ref[...]], axis=1)
    qpos2 = q0 + tcol_ref[...]

    g_hi = gate_ref[0].astype(BF16)
    g_r1 = gate_ref[0] - g_hi.astype(F32)
    g_mid = g_r1.astype(BF16)
    g_lo = (g_r1 - g_mid.astype(F32)).astype(BF16)
    g_all = (jnp.dot(jnp.concatenate([g_hi, g_mid], axis=1), gexp_ref[...], preferred_element_type=F32)
             + jnp.dot(g_lo, gexp_ref[0:LANES, :], preferred_element_type=F32))

    def gate_rows(c):
        return jnp.concatenate([g_all[:, (c * N_HEADS + bi) * LANES:(c * N_HEADS + bi + 1) * LANES]
                                for bi in range(N_HEADS)], axis=0)

    ncb = seq // CMP_BLOCK
    pad = jnp.zeros((LANES - ncb, D_KV), F32)
    kc = jnp.concatenate([kc_ref[0], pad], axis=0)
    vc = jnp.concatenate([vc_ref[0], pad], axis=0)
    kcaug = jnp.concatenate([kc.T.astype(BF16), cpos_ref[...]], axis=0)
    s_c = jnp.dot(qaug, kcaug, preferred_element_type=F32).reshape(GROUP, rows2, LANES)
    blk_end = (_iota((1, LANES), 1) + 1) * CMP_BLOCK - 1
    p_c = _softmax_rows(s_c, (qpos2 >= blk_end)[None]).reshape(rows, LANES)
    o_acc = gate_rows(0) * _mm(p_c, vc)

    p4 = _sum_groups(p_c, rows2)
    nbr = blkc_ref.shape[0]
    imp_t = (p4 + pltpu.roll(p4, LANES - 1, 1)).T[0:nbr]
    sel = _select_blocks(imp_t, q0 + trow_ref[...], blkc_ref[...], float(seq // SEL_BLOCK), top_k, 0).T
    sel = sel.astype(BF16)

    def make_step(kaug_ref, vaug_ref, mask_fn, m_ref, acc_ref):
        def step(j, carry):
            s = jnp.dot(qaug, kaug_ref[j], preferred_element_type=F32).reshape(GROUP, rows2, tk)
            dist = qpos2 - (j * tk + _iota((1, tk), 1))
            s = jnp.where(mask_fn(j, dist)[None], s, NEG_INF)
            m = m_ref[...]
            m_new = jnp.maximum(m, jnp.max(s, axis=2, keepdims=True))
            alpha = jnp.exp(m - m_new).reshape(rows, LANES)
            p = jnp.exp(s - jnp.concatenate([m_new] * (tk // LANES), axis=2)).astype(BF16).reshape(rows, tk)
            pv = lax.dot_general(p, vaug_ref[j], (((1,), (1,)), ((), ())), preferred_element_type=F32)
            m_ref[...] = m_new
            acc_ref[...] = jnp.concatenate([alpha, alpha], axis=1) * acc_ref[...] + pv
            return carry

        return step

    def sel_mask(j, dist):
        picked = jnp.dot(sel, e_ref[j], preferred_element_type=F32)
        return (picked > 0.5) & (dist >= 0)

    def win_mask(j, dist):
        return (dist >= 0) & (dist <= WINDOW)

    step_s = make_step(kas_ref, vas_ref, sel_mask, ms_ref, as_ref)
    step_w = make_step(kaw_ref, vaw_ref, win_mask, mw_ref, aw_ref)
    for m_ref, acc_ref in ((ms_ref, as_ref), (mw_ref, aw_ref)):
        m_ref[...] = jnp.full((GROUP, rows2, LANES), NEG_INF, F32)
        acc_ref[...] = jnp.zeros((rows, 2 * D_KV), F32)
    n_s = (q0 + tq - 1) // tk + 1
    j_lo = jnp.maximum(q0 - WINDOW, 0) // tk
    n_w = n_s - j_lo

    def both(i, carry):
        step_s(i, carry)
        return step_w(j_lo + i, carry)

    lax.fori_loop(0, n_w, both, 0)
    lax.fori_loop(n_w, n_s, step_s, 0)
    o_s = as_ref[:, 0:D_KV] / as_ref[:, D_KV:2 * D_KV]
    o_w = aw_ref[:, 0:D_KV] / aw_ref[:, D_KV:2 * D_KV]
    o_acc = o_acc + gate_rows(1) * o_s + gate_rows(2) * o_w
    o_ref[0] = _extract_heads(o_acc, tq)


def _row_consts(tq):
    r = jnp.arange(N_HEADS * tq)
    g, kv, t = r // (N_KV * tq), (r // tq) % N_KV, r % tq
    head = kv * GROUP + g
    slopes = jnp.exp2(-8.0 * (head + 1).astype(F32) / N_HEADS)
    return slopes[:, None], t.astype(jnp.int32)[:, None]


def _gate_expander():
    n = jnp.arange(N_BRANCH * N_HEADS)
    c, bi = n // N_HEADS, n % N_HEADS
    src_lane = c * N_HEADS + (bi % N_KV) * GROUP + bi // N_KV
    r = (jnp.arange(LANES)[:, None] == jnp.repeat(src_lane, LANES)[None, :]).astype(BF16)
    return jnp.concatenate([r, r], axis=0)


def _key_tiles(a, tk):
    return a.reshape(a.shape[0], -1, tk).transpose(1, 0, 2)


def _attn_prompt(q, gate, kcmp, vcmp, ks, vs, kw, vw):
    b, t, _ = q.shape
    tq = min(ATTN_TQ, t)
    tk = min(ATTN_TK, t)
    n_sb = t // SEL_BLOCK
    assert t // POS_SPLIT < 256
    slopes, tcol = _row_consts(tq)
    qs = jnp.zeros((N_HEADS * tq, LANES), F32).at[:, 0].set(POS_SPLIT * slopes[:, 0]).at[:, 1].set(slopes[:, 0])
    kpos = jnp.arange(t)
    pos = jnp.zeros((LANES, t), F32).at[0].set(kpos // POS_SPLIT).at[1].set(kpos % POS_SPLIT)
    lane = jnp.arange(LANES)
    blk = jnp.where((lane % 2 == 0) & (lane < 2 * n_sb), lane // 2, -1).astype(F32)
    e = (blk[:, None] == (kpos // SEL_BLOCK).astype(F32)[None, :])
    blk_end = (lane + 1) * CMP_BLOCK - 1
    cpos = jnp.zeros((LANES, LANES), F32).at[0].set(blk_end // POS_SPLIT).at[1].set(blk_end % POS_SPLIT)
    tcol2 = tcol[:N_KV * tq]
    nbr = min(LANES, -(-2 * n_sb // SUBLANES) * SUBLANES)
    consts = (tcol2, qs.astype(BF16), blk[:nbr, None], tcol2.reshape(1, -1),
              _key_tiles(e[:nbr].astype(BF16), tk), _key_tiles(pos.astype(BF16), tk), cpos.astype(BF16),
              _gate_expander())
    tile = lambda n: pl.BlockSpec((1, tq, n), lambda i, j: (i, j, 0))
    seq = lambda r, n: pl.BlockSpec((1, r, n), lambda i, j: (i, 0, 0))
    return pl.pallas_call(
        functools.partial(_attn_prompt_body, tq=tq, tk=tk, seq=t, top_k=min(TOP_K, n_sb)),
        grid=(b, t // tq),
        in_specs=[tile(D_ATTN), tile(LANES), seq(t // CMP_BLOCK, D_KV), seq(t // CMP_BLOCK, D_KV),
                  seq(D_KV, t), seq(D_KV, t), seq(D_KV, t), seq(D_KV, t)] + [_full(c.shape) for c in consts],
        out_specs=tile(D_ATTN),
        out_shape=jax.ShapeDtypeStruct((b, t, D_ATTN), F32),
        scratch_shapes=[pltpu.VMEM((t // tk, 2 * D_KV, tk), BF16)] * 4
        + [pltpu.VMEM((GROUP, N_KV * tq, LANES), F32), pltpu.VMEM((N_HEADS * tq, 2 * D_KV), F32)] * 2,
        compiler_params=pltpu.CompilerParams(dimension_semantics=("arbitrary", "arbitrary"),
                                             vmem_limit_bytes=VMEM_LIMIT),
        name="attn_prompt",
    )(q, gate, kcmp, vcmp, ks, vs, kw, vw, *consts)


def _attn_sample_body(pt_ref, q_ref, gate_ref, ksn_ref, vsn_ref, kwn_ref, vwn_ref, kwin_ref, vwin_ref,
                      kcp_ref, vcp_ref, cvalid_ref, bias_c_ref, bias_s_ref, bias_w_ref, qrow_ref, blkc_ref, e_ref, *rest,
                      t, n_pages, page, past, w_buf, top_k, nbs):
    ks_pages = rest[:nbs * n_pages]
    vs_pages = rest[nbs * n_pages:2 * nbs * n_pages]
    o_ref, kwo_ref, vwo_ref = rest[2 * nbs * n_pages:]
    b0 = pl.program_id(0) * nbs
    rows = N_HEADS * t
    rows2 = N_KV * t
    seqs = range(nbs)

    def stack(xs):
        return jnp.concatenate(xs, axis=0)

    def part(x, bb, n):
        return x[bb * n:(bb + 1) * n]

    def pad_rows(x):
        if x.shape[0] == LANES:
            return x
        return jnp.concatenate([x, jnp.zeros((LANES - x.shape[0], D_KV), F32)], axis=0)

    live = NEG_INF / 2
    qbd = [_make_qbd(q_ref[bb], t) for bb in seqs]
    gates = [gate_ref[bb] for bb in seqs]

    def gate_rows(c):
        return stack([_gate_rows(gates[bb], c, t) for bb in seqs])

    kc = [pad_rows(stack([kcp_ref[pt_ref[(b0 + bb) * n_pages + p] // 2] for p in range(n_pages)])) for bb in seqs]
    vc = [pad_rows(stack([vcp_ref[pt_ref[(b0 + bb) * n_pages + p] // 2] for p in range(n_pages)])) for bb in seqs]
    bias_c = bias_c_ref[...]
    s_c = stack([_mm_nt(qbd[bb], kc[bb]) for bb in seqs]) + bias_c
    cvalid = stack([jnp.broadcast_to(cvalid_ref[bb], (rows, LANES)) for bb in seqs])
    p_c = _softmax_rows(s_c, (cvalid > 0.5) & (bias_c > live))
    o_acc = gate_rows(0) * stack([_mm(part(p_c, bb, rows), vc[bb]) for bb in seqs])

    p4 = stack([_sum_groups(part(p_c, bb, rows), rows2) for bb in seqs])
    x1 = p4 + pltpu.roll(p4, LANES - 1, 1)
    imp = x1 + pltpu.roll(x1, LANES - 4, 1)
    n2 = nbs * rows2
    imp_t = jnp.concatenate([imp, jnp.zeros((LANES - n2, LANES), F32)], axis=0).T
    sel = _select_blocks(imp_t, qrow_ref[...], blkc_ref[...], float(past // SEL_BLOCK), top_k, 0).T[0:n2]
    sel4 = stack([part(sel, bb, rows2) for bb in seqs for _ in range(GROUP)])

    s_s = stack([jnp.concatenate([_mm(qbd[bb], ks_pages[bb * n_pages + p][...]) for p in range(n_pages)]
                                 + [_mm_nt(qbd[bb], pad_rows(ksn_ref[bb]))], axis=1) for bb in seqs])
    bias_s = bias_s_ref[...]
    picked = _mm(sel4, e_ref[...])
    p_s = _softmax_rows(s_s + bias_s, (picked > 0.5) & (bias_s > live))
    o_s = []
    for bb in seqs:
        pb = part(p_s, bb, rows)
        acc = _mm(pb[:, past:], pad_rows(vsn_ref[bb]))
        for p in range(n_pages):
            acc = acc + _mm_nt(pb[:, p * page:(p + 1) * page], vs_pages[bb * n_pages + p][...])
        o_s.append(acc)
    o_acc = o_acc + gate_rows(1) * stack(o_s)

    kwin = [kwin_ref[bb] for bb in seqs]
    vwin = [vwin_ref[bb] for bb in seqs]
    s_w = stack([jnp.concatenate([_mm(qbd[bb], kwin[bb]), _mm_nt(qbd[bb], pad_rows(kwn_ref[bb]))], axis=1)
                 for bb in seqs])
    bias_w = bias_w_ref[...]
    p_w = _softmax_rows(s_w + bias_w, bias_w > live)
    o_w = stack([_mm_nt(part(p_w, bb, rows)[:, :w_buf], vwin[bb])
                 + _mm(part(p_w, bb, rows)[:, w_buf:], pad_rows(vwn_ref[bb])) for bb in seqs])
    o_acc = o_acc + gate_rows(2) * o_w

    is_new = _iota((D_KV, LANES), 1) >= LANES - t

    def shift_in(win, new):
        new_t = jnp.concatenate([jnp.zeros((LANES - t, D_KV), F32), new], axis=0).T
        rolled = pltpu.roll(win, w_buf - t, 1)
        last = jnp.where(is_new, new_t, rolled[:, w_buf - LANES:])
        return jnp.concatenate([rolled[:, :w_buf - LANES], last], axis=1)

    for bb in seqs:
        o_ref[bb] = _extract_heads(part(o_acc, bb, rows), t)
        kwo_ref[bb] = shift_in(kwin[bb], kwn_ref[bb])
        vwo_ref[bb] = shift_in(vwin[bb], vwn_ref[bb])


def _page_map(l, row_stride, offset, i, pt):
    return (l, pt[i * row_stride + offset], 0, 0)


def _attn_sample(l, page_table, q, gate, ks_new, vs_new, kw_new, vw_new, kwin, vwin, kcp, vcp,
                 cache_ks, cache_vs):
    nb, t, _ = q.shape
    n_pages = page_table.shape[1]
    page = cache_ks.shape[3]
    past = n_pages * page
    w_buf = kwin.shape[3]
    n_sb = -(-(past + t) // SEL_BLOCK)
    assert t == SUBLANES and n_pages * 8 <= LANES and page == LANES and past % SEL_BLOCK == 0
    assert w_buf % LANES == 0
    slopes, tcol = _row_consts(t)
    lane = jnp.arange(LANES)
    pg, r = lane // 8, lane % 8
    in_range = pg < n_pages
    cend = jnp.where(in_range, (4 * pg + r % 4 + 1) * CMP_BLOCK - 1, 1 << 30).astype(jnp.int32)[None, :]
    par = page_table % 2
    par_l = jnp.take(par, jnp.minimum(pg, n_pages - 1), axis=1)
    cvalid = ((r[None, :] // 4 == par_l) & in_range[None, :]).astype(F32)[:, None, :]
    blk = jnp.where(in_range & (r % 2 == 0) & (r < 4), 2 * pg + r // 2, -1)
    blk = jnp.where(lane == 1, n_sb - 1, blk)
    blkf = blk.astype(F32)[None, :]
    e = (blkf.reshape(-1, 1) == (jnp.arange(past + LANES) // SEL_BLOCK).astype(F32)[None, :]).astype(BF16)
    nbs = next(c for c in range(min(DEC_STEP, nb), 0, -1) if nb % c == 0)
    qpos = past + tcol
    assert nbs * N_KV * t <= LANES
    qrow = jnp.pad(jnp.tile(qpos[:N_KV * t, 0], nbs), (0, LANES - nbs * N_KV * t), constant_values=past)[None, :]
    blkc = blkf.reshape(-1, 1)

    def bias(dist, ok):
        return jnp.tile(jnp.where(ok, -(slopes * dist.astype(F32)), NEG_INF), (nbs, 1))

    dist_c = qpos - cend
    dist_s = qpos - jnp.arange(past + LANES)[None, :]
    dist_w = qpos - (past - w_buf + jnp.arange(w_buf + LANES))[None, :]
    bias_c = bias(dist_c, dist_c >= 0)
    bias_s = bias(dist_s, dist_s >= 0)
    bias_w = bias(dist_w, (dist_w >= 0) & (dist_w <= WINDOW))
    tok = lambda n: pl.BlockSpec((nbs, t, n), lambda i, pt: (i, 0, 0))
    win = pl.BlockSpec((None, nbs, D_KV, w_buf), lambda i, pt: (l, i, 0, 0))
    win_out = pl.BlockSpec((nbs, D_KV, w_buf), lambda i, pt: (i, 0, 0))
    res = pl.BlockSpec((None,) + kcp.shape[1:], lambda i, pt: (l, 0, 0, 0))
    const = lambda a: pl.BlockSpec(a.shape, lambda i, pt: (0,) * a.ndim)
    pages = [pl.BlockSpec((None, None, D_KV, page), functools.partial(_page_map, l, nbs * n_pages, bb * n_pages + p))
             for bb in range(nbs) for p in range(n_pages)]
    grid_spec = pltpu.PrefetchScalarGridSpec(
        num_scalar_prefetch=1,
        grid=(nb // nbs,),
        in_specs=[tok(D_ATTN), tok(LANES), tok(D_KV), tok(D_KV), tok(D_KV), tok(D_KV), win, win, res, res,
                  pl.BlockSpec((nbs, 1, LANES), lambda i, pt: (i, 0, 0)),
                  const(bias_c), const(bias_s), const(bias_w), const(qrow), const(blkc), const(e)] + pages + pages,
        out_specs=[tok(D_ATTN), win_out, win_out],
    )
    return pl.pallas_call(
        functools.partial(_attn_sample_body, t=t, n_pages=n_pages, page=page, past=past, w_buf=w_buf,
                          top_k=min(TOP_K, n_sb), nbs=nbs),
        grid_spec=grid_spec,
        out_shape=[jax.ShapeDtypeStruct((nb, t, D_ATTN), F32),
                   jax.ShapeDtypeStruct((nb, D_KV, w_buf), F32), jax.ShapeDtypeStruct((nb, D_KV, w_buf), F32)],
        compiler_params=pltpu.CompilerParams(dimension_semantics=("arbitrary",), vmem_limit_bytes=VMEM_LIMIT),
        name="attn_sample",
    )(page_table.reshape(-1), q, gate, ks_new, vs_new, kw_new, vw_new, kwin, vwin, kcp, vcp, cvalid,
      bias_c, bias_s, bias_w, qrow, blkc, e, *([cache_ks] * (nbs * n_pages)), *([cache_vs] * (nbs * n_pages)))


def _back_head(x, o, pab, wg2_ref, woa_ref, wmix_ref, ln1g_ref, ln1b_ref):
    y_c = _mm(o, woa_ref[...])
    g2 = jax.nn.sigmoid(_mm(x, wg2_ref[...]))
    mixed = pab + g2 * y_c
    return _layer_norm(ALPHA * x + _mm(mixed, wmix_ref[...]), ln1g_ref[...], ln1b_ref[...])


def _back_prompt_body(x_ref, o_ref, pab_ref, wg2_ref, woa_ref, wmix_ref, ln1g_ref, ln1b_ref,
                      wup_ref, fcw_ref, fcb_ref, wdn_ref, ln2g_ref, ln2b_ref,
                      y_ref, fst_ref, ext_ref, *, tt):
    j = pl.program_id(1)

    @pl.when(j == 0)
    def _():
        ext_ref[0:8, :] = jnp.zeros((8, D_FF), F32)

    h = _back_head(x_ref[0], o_ref[0], pab_ref[0], wg2_ref, woa_ref, wmix_ref, ln1g_ref, ln1b_ref)
    hb = h.astype(BF16)
    cf = D_FF // FF_CHUNKS
    f = jnp.zeros((tt, D_MODEL), F32)
    for c in range(FF_CHUNKS):
        cs = slice(c * cf, (c + 1) * cf)
        u = _mm(hb, wup_ref[:, c * cf:(c + 1) * cf])
        gt = _mm(hb, wup_ref[:, D_FF + c * cf:D_FF + (c + 1) * cf])
        ext_ref[8:8 + tt, cs] = u
        conv = fcb_ref[:, cs] + fcw_ref[0:1, cs] * ext_ref[6:6 + tt, cs]
        conv = conv + fcw_ref[1:2, cs] * ext_ref[7:7 + tt, cs]
        conv = conv + fcw_ref[2:3, cs] * u
        f = f + _mm(jax.nn.gelu(conv) * gt, wdn_ref[c * cf:(c + 1) * cf, :])
    y_ref[0] = _layer_norm(ALPHA * h + f, ln2g_ref[...], ln2b_ref[...])
    last = ext_ref[tt:tt + 8, :]
    fst_ref[0] = last
    ext_ref[0:8, :] = last


def _back_sample_body(x_ref, o_ref, pab_ref, hf_ref, wg2_ref, woa_ref, wmix_ref, ln1g_ref, ln1b_ref,
                      wup_ref, fcw_ref, fcb_ref, wdn_ref, ln2g_ref, ln2b_ref,
                      y_ref, fst_ref, ext_ref, *, ns, t):
    h = _back_head(x_ref[...], o_ref[...], pab_ref[...], wg2_ref, woa_ref, wmix_ref, ln1g_ref, ln1b_ref)
    hb = h.astype(BF16)
    cf = D_FF // FF_CHUNKS
    ext_ref[:, 6:8, :] = hf_ref[...]
    f = jnp.zeros((ns * t, D_MODEL), F32)
    for c in range(FF_CHUNKS):
        cs = slice(c * cf, (c + 1) * cf)
        u = _mm(hb, wup_ref[:, c * cf:(c + 1) * cf])
        gt = _mm(hb, wup_ref[:, D_FF + c * cf:D_FF + (c + 1) * cf])
        ext_ref[:, 8:8 + t, cs] = u.reshape(ns, t, cf)
        fcw = fcw_ref[:, cs]
        conv = fcb_ref[:, cs] + fcw[0:1] * ext_ref[:, 6:6 + t, cs]
        conv = conv + fcw[1:2] * ext_ref[:, 7:7 + t, cs]
        conv = conv + fcw[2:3] * ext_ref[:, 8:8 + t, cs]
        f = f + _mm(jax.nn.gelu(conv).reshape(ns * t, cf) * gt, wdn_ref[c * cf:(c + 1) * cf, :])
    y_ref[...] = _layer_norm(ALPHA * h + f, ln2g_ref[...], ln2b_ref[...])
    fst_ref[...] = ext_ref[:, t:t + 8, :]


def _back_prompt(x, o, pab, wts):
    b, t, _ = x.shape
    tt = min(FRONT_TILE, t)
    tok = lambda n: pl.BlockSpec((1, tt, n), lambda i, j: (i, j, 0))
    return pl.pallas_call(
        functools.partial(_back_prompt_body, tt=tt),
        grid=(b, t // tt),
        in_specs=[tok(D_MODEL), tok(D_ATTN), tok(D_MODEL)] + [_full(w.shape) for w in wts],
        out_specs=[tok(D_MODEL), pl.BlockSpec((1, 8, D_FF), lambda i, j: (i, 0, 0))],
        out_shape=[jax.ShapeDtypeStruct((b, t, D_MODEL), F32), jax.ShapeDtypeStruct((b, 8, D_FF), F32)],
        scratch_shapes=[pltpu.VMEM((8 + tt, D_FF), F32)],
        compiler_params=pltpu.CompilerParams(dimension_semantics=("arbitrary", "arbitrary"),
                                             vmem_limit_bytes=VMEM_LIMIT),
        name="back_prompt",
    )(x, o, pab, *wts)


def _back_sample(x, o, pab, hist_f, wts, t):
    n = x.shape[0]
    nb = n // t
    ns = min(FRONT_TILE // t, nb)
    rows = ns * t
    tok = lambda c: pl.BlockSpec((rows, c), lambda i: (i, 0))
    return pl.pallas_call(
        functools.partial(_back_sample_body, ns=ns, t=t),
        grid=(nb // ns,),
        in_specs=[tok(D_MODEL), tok(D_ATTN), tok(D_MODEL),
                  pl.BlockSpec((ns, CONV_W - 1, D_FF), lambda i: (i, 0, 0))] + [_full(w.shape) for w in wts],
        out_specs=[tok(D_MODEL), pl.BlockSpec((ns, 8, D_FF), lambda i: (i, 0, 0))],
        out_shape=[jax.ShapeDtypeStruct((n, D_MODEL), F32), jax.ShapeDtypeStruct((nb, 8, D_FF), F32)],
        scratch_shapes=[pltpu.VMEM((ns, 8 + t, D_FF), F32)],
        compiler_params=pltpu.CompilerParams(dimension_semantics=("arbitrary",), vmem_limit_bytes=VMEM_LIMIT),
        name="back_sample",
    )(x, o, pab, hist_f, *wts)


def _block_diag2(w):
    z = jnp.zeros_like(w)
    return jnp.concatenate([jnp.concatenate([w, z], axis=-1), jnp.concatenate([z, w], axis=-1)], axis=-2)


def kernel(x_prompt, x_sample, cache_k_cmp, cache_v_cmp, cache_k_sel, cache_v_sel, state_k_win, state_v_win, state_conv, state_pool, state_ffn_conv, page_table, w_in, conv_w, conv_b, w_out_conv, pool_w, pool_scale, w_cmp_k, w_cmp_v, w_out_attn, w_mix_out, ln1_g, ln1_b, w_ffn_up, ffn_conv_w, ffn_conv_b, w_ffn_down, ln2_g, ln2_b):
    depth = w_in.shape[0]
    bp, seq, _ = x_prompt.shape
    nb, t, _ = x_sample.shape
    n_phys, page = cache_k_cmp.shape[1], cache_k_cmp.shape[2]
    past = page_table.shape[1] * page
    w_buf = state_k_win.shape[2]

    w_front = jnp.concatenate(
        [w_in[:, :, :IN_GATES], jnp.zeros((depth, D_MODEL, C_G01 - IN_GATES), F32),
         w_in[:, :, IN_GATES:IN_GATES + 2 * D_MODEL]], axis=2).astype(BF16)
    w_g2 = w_in[:, :, IN_GATES + 2 * D_MODEL:].astype(BF16)
    ng = pool_w.shape[1]
    pool_bd = jnp.zeros((depth, D_POOL, D_MODEL), F32)
    for g in range(ng):
        pool_bd = pool_bd.at[:, g * POOL_GROUP:(g + 1) * POOL_GROUP,
                             g * (D_MODEL // ng):(g + 1) * (D_MODEL // ng)].set(pool_w[:, g])
    pool_bd = pool_bd.astype(BF16)
    wck = _block_diag2(w_cmp_k).astype(BF16)
    wcv = _block_diag2(w_cmp_v).astype(BF16)
    w_oc, w_oa, w_mix = w_out_conv.astype(BF16), w_out_attn.astype(BF16), w_mix_out.astype(BF16)
    w_up, w_dn = w_ffn_up.astype(BF16), w_ffn_down.astype(BF16)
    row = lambda a: a[:, None, :]

    keys_minor = lambda a: jnp.transpose(a, (0, 1, 3, 4, 2)).reshape(a.shape[0], a.shape[1], D_KV, a.shape[2])
    tokens_major = lambda a: jnp.transpose(a.reshape(a.shape[0], N_KV, HEAD_DIM, a.shape[2]), (0, 3, 1, 2))
    ck_s, cv_s = keys_minor(cache_k_sel), keys_minor(cache_v_sel)
    kwin_all, vwin_all = keys_minor(state_k_win), keys_minor(state_v_win)
    kcp, vcp = _compress(keys_minor(cache_k_cmp), keys_minor(cache_v_cmp), wck, wcv)
    kcp = kcp.reshape(depth, n_phys // 2, 8, D_KV)
    vcp = vcp.reshape(depth, n_phys // 2, 8, D_KV)

    y_p = x_prompt
    y_s = x_sample.reshape(nb * t, D_MODEL)
    st_p, st_s = [], []
    for l in range(depth):
        fw = (w_front[l], conv_w[l], row(conv_b)[l], w_oc[l], pool_bd[l], row(pool_scale)[l])
        bw = (w_g2[l], w_oa[l], w_mix[l], row(ln1_g)[l], row(ln1_b)[l], w_up[l], ffn_conv_w[l],
              row(ffn_conv_b)[l], w_dn[l], row(ln2_g)[l], row(ln2_b)[l])

        pab, q, kc, vc, ks, vs, kw, vw, gate, cst, pst = _front_prompt(y_p, fw)
        kcmp, vcmp = _compress(kc[None], vc[None], wck[l:l + 1], wcv[l:l + 1])
        o = _attn_prompt(q, gate, kcmp.reshape(bp, seq // CMP_BLOCK, D_KV),
                         vcmp.reshape(bp, seq // CMP_BLOCK, D_KV), ks, vs, kw, vw)
        y_p, fst = _back_prompt(y_p, o, pab, bw)
        st_p.append(tuple(tokens_major(a) for a in (kc, vc, ks, vs, kw[:, :, seq - w_buf:], vw[:, :, seq - w_buf:]))
                    + (cst[:, 8 - (CONV_W - 1):], pst[:, 16 - POOL_HIST:], fst[:, 8 - (CONV_W - 1):]))

        pab, q, kc, vc, ks, vs, kw, vw, gate, cst, pst = _front_sample(
            y_s.reshape(nb, t, D_MODEL), state_conv[l], state_pool[l], fw, past)
        r3 = lambda a: a.reshape(nb, t, a.shape[-1])
        o, kwin, vwin = _attn_sample(l, page_table, r3(q), r3(gate), r3(ks), r3(vs), r3(kw), r3(vw),
                                     kwin_all, vwin_all, kcp, vcp, ck_s, cv_s)
        y_s, fst = _back_sample(y_s, o.reshape(nb * t, D_ATTN), pab, state_ffn_conv[l], bw, t)
        st_s.append((kc, vc, ks, vs, tokens_major(kwin), tokens_major(vwin),
                     cst[:, 8 - (CONV_W - 1):], pst[:, 16 - POOL_HIST:], fst[:, 8 - (CONV_W - 1):]))

    kc_p, vc_p, ks_p, vs_p, kw_p, vw_p, conv_p, pool_p, ffn_p = [jnp.stack(a) for a in zip(*st_p)]
    kc_s, vc_s, ks_s, vs_s, kw_s, vw_s, conv_s, pool_s, ffn_s = [jnp.stack(a) for a in zip(*st_s)]
    kc_s, vc_s, ks_s, vs_s = [a.reshape(depth, nb, t, N_KV, HEAD_DIM) for a in (kc_s, vc_s, ks_s, vs_s)]
    return (y_p, y_s.reshape(nb, t, D_MODEL), kc_p, kc_s, vc_p, vc_s, ks_p, ks_s, vs_p, vs_s,
            kw_p, kw_s, vw_p, vw_s, conv_p, conv_s, pool_p, pool_s, ffn_p, ffn_s)
```

```python
import functools

import jax
import jax.numpy as jnp
from jax import lax
from jax.experimental import pallas as pl
from jax.experimental.pallas import tpu as pltpu

F32 = jnp.float32
BF16 = jnp.bfloat16

D_MODEL = 1024
DEPTH = 4
D_CONV = 256
CONV_W = 3
D_POOL = 256
POOL_WINDOWS = (2, 4, 8, 16)
POOL_GROUP = D_POOL // len(POOL_WINDOWS)
POOL_HIST = 15
N_HEADS = 8
N_KV = 2
GROUP = N_HEADS // N_KV
HEAD_DIM = 64
D_ATTN = N_HEADS * HEAD_DIM
D_KV = N_KV * HEAD_DIM
CMP_BLOCK = 32
SEL_BLOCK = 64
TOP_K = 8
WINDOW = 512
N_BRANCH = 3
D_FF = 2816
ALPHA = (2.0 * DEPTH) ** 0.25
LN_EPS = 1e-5
ATTN_SCALE = HEAD_DIM ** -0.5
NEG_INF = -1e30
POS_SPLIT = 64

LANES = 128
SUBLANES = 8
VMEM_LIMIT = 56 * 1024 * 1024

C_AB, C_AC, C_AH, C_U, C_Q = 0, D_CONV, 2 * D_CONV, 3 * D_CONV, 3 * D_CONV + D_POOL
C_KV = C_Q + D_ATTN
C_NSA = C_KV + 6 * D_KV
C_G01 = C_NSA + LANES
IN_MAIN = C_NSA
IN_GATES = IN_MAIN + N_BRANCH * N_HEADS

FRONT_TILE = 256
ATTN_TQ = 128
ATTN_TK = 256
CMP_ROWS = 8192
CMP_PITCH = LANES + SUBLANES
DEC_STEP = 4
WIN_STEP = 8
FF_CHUNKS = 2


def _mm(a, b):
    return jnp.dot(a.astype(BF16), b.astype(BF16), preferred_element_type=F32)


def _mm_nt(a, b):
    return lax.dot_general(a.astype(BF16), b.astype(BF16), (((1,), (1,)), ((), ())),
                           preferred_element_type=F32)


def _layer_norm(x, g, b):
    mu = jnp.mean(x, axis=-1, keepdims=True)
    var = jnp.mean(jnp.square(x - mu), axis=-1, keepdims=True)
    return (x - mu) * lax.rsqrt(var + LN_EPS) * g + b


def _iota(shape, axis):
    return lax.broadcasted_iota(jnp.int32, shape, axis)


def _pool_delta(u, shifted, pos):
    nd = u.ndim
    acc = u
    sums = {}
    for j in range(1, POOL_WINDOWS[-1]):
        acc = acc + shifted(j)
        if j + 1 in POOL_WINDOWS:
            sums[j + 1] = acc
    grp = _iota((1,) * (nd - 1) + (D_POOL,), nd - 1) // POOL_GROUP
    s = sums[POOL_WINDOWS[-1]]
    win = jnp.full(grp.shape, POOL_WINDOWS[-1], jnp.int32)
    for g in range(len(POOL_WINDOWS) - 2, -1, -1):
        s = jnp.where(grp == g, sums[POOL_WINDOWS[g]], s)
        win = jnp.where(grp == g, POOL_WINDOWS[g], win)
    cnt = jnp.minimum(win, pos + 1).astype(F32)
    return s / cnt - u


def _front_tail(z, conv, d, woc_ref, pbd_ref, psc_ref, outs, prompt):
    pab_ref, q_ref, kv_refs, gate_ref = outs
    y_a = _mm(z[:, C_AB:C_AB + D_CONV] * conv, woc_ref[...])
    y_b = _mm(d, pbd_ref[...]) * psc_ref[...]
    g0 = jax.nn.sigmoid(z[:, C_G01:C_G01 + D_MODEL])
    g1 = jax.nn.sigmoid(z[:, C_G01 + D_MODEL:C_G01 + 2 * D_MODEL])
    pab_ref[...] = (g0 * y_a + g1 * y_b).reshape(pab_ref.shape)
    q_ref[...] = z[:, C_Q:C_Q + D_ATTN].reshape(q_ref.shape)
    for i, r in enumerate(kv_refs):
        kv = z[:, C_KV + i * D_KV:C_KV + (i + 1) * D_KV]
        r[...] = (kv.T if prompt else kv).reshape(r.shape)
    gate_ref[...] = jax.nn.sigmoid(z[:, C_NSA:C_NSA + LANES]).reshape(gate_ref.shape)


def _front_prompt_body(x_ref, w_ref, cw_ref, cb_ref, woc_ref, pbd_ref, psc_ref,
                       pab_ref, q_ref, kc_ref, vc_ref, ks_ref, vs_ref, kw_ref, vw_ref, gate_ref,
                       cst_ref, pst_ref, extc_ref, extu_ref, *, tt):
    j = pl.program_id(1)

    @pl.when(j == 0)
    def _():
        extc_ref[0:8, :] = jnp.zeros((8, D_CONV), F32)
        extu_ref[0:16, :] = jnp.zeros((16, D_POOL), F32)

    z = _mm(x_ref[0], w_ref[...])
    ch = z[:, C_AC:C_AC + D_CONV] * z[:, C_AH:C_AH + D_CONV]
    extc_ref[8:8 + tt, :] = ch
    conv = cb_ref[...] + cw_ref[0:1, :] * extc_ref[6:6 + tt, :]
    conv = conv + cw_ref[1:2, :] * extc_ref[7:7 + tt, :]
    conv = conv + cw_ref[2:3, :] * ch
    u = z[:, C_U:C_U + D_POOL]
    extu_ref[16:16 + tt, :] = u
    pos = j * tt + _iota((tt, 1), 0)
    d = _pool_delta(u, lambda s: extu_ref[16 - s:16 - s + tt, :], pos)
    _front_tail(z, conv, d, woc_ref, pbd_ref, psc_ref,
                (pab_ref, q_ref, (kc_ref, vc_ref, ks_ref, vs_ref, kw_ref, vw_ref), gate_ref), True)
    last_c = extc_ref[tt:tt + 8, :]
    last_u = extu_ref[tt:tt + 16, :]
    cst_ref[0] = last_c
    pst_ref[0] = last_u
    extc_ref[0:8, :] = last_c
    extu_ref[0:16, :] = last_u


def _front_sample_body(x_ref, hc_ref, hu_ref, w_ref, cw_ref, cb_ref, woc_ref, pbd_ref, psc_ref,
                       pab_ref, q_ref, kc_ref, vc_ref, ks_ref, vs_ref, kw_ref, vw_ref, gate_ref,
                       cst_ref, pst_ref, extc_ref, extu_ref, *, ns, t, pos0):
    z = _mm(x_ref[...], w_ref[...])
    ch = z[:, C_AC:C_AC + D_CONV] * z[:, C_AH:C_AH + D_CONV]
    extc_ref[:, 6:8, :] = hc_ref[...]
    extc_ref[:, 8:8 + t, :] = ch.reshape(ns, t, D_CONV)
    cw = cw_ref[...]
    conv = cb_ref[...] + cw[0:1] * extc_ref[:, 6:6 + t, :]
    conv = conv + cw[1:2] * extc_ref[:, 7:7 + t, :]
    conv = conv + cw[2:3] * extc_ref[:, 8:8 + t, :]
    u = z[:, C_U:C_U + D_POOL]
    extu_ref[:, 1:16, :] = hu_ref[...]
    extu_ref[:, 16:16 + t, :] = u.reshape(ns, t, D_POOL)
    pos = pos0 + _iota((1, t, 1), 1)
    d = _pool_delta(extu_ref[:, 16:16 + t, :], lambda s: extu_ref[:, 16 - s:16 - s + t, :], pos)
    _front_tail(z, conv.reshape(ns * t, D_CONV), d.reshape(ns * t, D_POOL), woc_ref, pbd_ref, psc_ref,
                (pab_ref, q_ref, (kc_ref, vc_ref, ks_ref, vs_ref, kw_ref, vw_ref), gate_ref), False)
    cst_ref[...] = extc_ref[:, t:t + 8, :]
    pst_ref[...] = extu_ref[:, t:t + 16, :]


def _full(shape):
    n = len(shape)
    return pl.BlockSpec(shape, lambda *_: (0,) * n, pipeline_mode=pl.Buffered(1))


def _front_prompt(x, wts):
    b, t, _ = x.shape
    tt = min(FRONT_TILE, t)
    tok = lambda n: pl.BlockSpec((1, tt, n), lambda i, j: (i, j, 0))
    st = lambda r, n: pl.BlockSpec((1, r, n), lambda i, j: (i, 0, 0))
    kvt = pl.BlockSpec((1, D_KV, tt), lambda i, j: (i, 0, j))
    out_shape = ([jax.ShapeDtypeStruct((b, t, D_MODEL), F32), jax.ShapeDtypeStruct((b, t, D_ATTN), F32)]
                 + [jax.ShapeDtypeStruct((b, D_KV, t), F32)] * 6
                 + [jax.ShapeDtypeStruct((b, t, LANES), F32),
                    jax.ShapeDtypeStruct((b, 8, D_CONV), F32), jax.ShapeDtypeStruct((b, 16, D_POOL), F32)])
    out_specs = ([tok(D_MODEL), tok(D_ATTN)] + [kvt] * 6 + [tok(LANES), st(8, D_CONV), st(16, D_POOL)])
    return pl.pallas_call(
        functools.partial(_front_prompt_body, tt=tt),
        grid=(b, t // tt),
        in_specs=[tok(D_MODEL)] + [_full(w.shape) for w in wts],
        out_specs=out_specs,
        out_shape=out_shape,
        scratch_shapes=[pltpu.VMEM((8 + tt, D_CONV), F32), pltpu.VMEM((16 + tt, D_POOL), F32)],
        compiler_params=pltpu.CompilerParams(dimension_semantics=("arbitrary", "arbitrary"),
                                             vmem_limit_bytes=VMEM_LIMIT),
        name="front_prompt",
    )(x, *wts)


def _front_sample(x, hist_c, hist_u, wts, pos0):
    nb, t, _ = x.shape
    ns = min(FRONT_TILE // t, nb)
    rows = ns * t
    xf = x.reshape(nb * t, D_MODEL)
    tok = lambda n: pl.BlockSpec((rows, n), lambda i: (i, 0))
    st = lambda r, n: pl.BlockSpec((ns, r, n), lambda i: (i, 0, 0))
    out_shape = ([jax.ShapeDtypeStruct((nb * t, D_MODEL), F32), jax.ShapeDtypeStruct((nb * t, D_ATTN), F32)]
                 + [jax.ShapeDtypeStruct((nb * t, D_KV), F32)] * 6
                 + [jax.ShapeDtypeStruct((nb * t, LANES), F32),
                    jax.ShapeDtypeStruct((nb, 8, D_CONV), F32), jax.ShapeDtypeStruct((nb, 16, D_POOL), F32)])
    out_specs = ([tok(D_MODEL), tok(D_ATTN)] + [tok(D_KV)] * 6 + [tok(LANES), st(8, D_CONV), st(16, D_POOL)])
    return pl.pallas_call(
        functools.partial(_front_sample_body, ns=ns, t=t, pos0=pos0),
        grid=(nb // ns,),
        in_specs=[tok(D_MODEL), st(CONV_W - 1, D_CONV), st(POOL_HIST, D_POOL)] + [_full(w.shape) for w in wts],
        out_specs=out_specs,
        out_shape=out_shape,
        scratch_shapes=[pltpu.VMEM((ns, 8 + t, D_CONV), F32), pltpu.VMEM((ns, 16 + t, D_POOL), F32)],
        compiler_params=pltpu.CompilerParams(dimension_semantics=("arbitrary",),
                                             vmem_limit_bytes=VMEM_LIMIT),
        name="front_sample",
    )(xf, hist_c, hist_u, *wts)


def _compress_body(k_ref, v_ref, wk_ref, wv_ref, ko_ref, vo_ref, kr_ref, vr_ref, *, npg, r):
    cols = r // LANES
    nvp = npg * cols
    for i in range(npg):
        for c in range(cols):
            row0 = (i * cols + c) * CMP_PITCH
            kr_ref[row0:row0 + LANES, :] = k_ref[i, :, c * LANES:(c + 1) * LANES].astype(BF16).T.astype(F32)
            vr_ref[row0:row0 + LANES, :] = v_ref[i, :, c * LANES:(c + 1) * LANES].astype(BF16).T.astype(F32)
    nj = LANES // CMP_BLOCK
    acc_k = jnp.zeros((nj * nvp, D_KV), F32)
    acc_v = jnp.zeros((nj * nvp, D_KV), F32)
    for l in range(CMP_BLOCK):
        rows_k = [kr_ref[pl.ds(j * CMP_BLOCK + l, nvp, stride=CMP_PITCH), :] for j in range(nj)]
        rows_v = [vr_ref[pl.ds(j * CMP_BLOCK + l, nvp, stride=CMP_PITCH), :] for j in range(nj)]
        acc_k = acc_k + _mm(jnp.concatenate(rows_k, axis=0), wk_ref[l])
        acc_v = acc_v + _mm(jnp.concatenate(rows_v, axis=0), wv_ref[l])
    for j in range(nj):
        ko_ref[pl.ds(j, nvp, stride=nj), :] = acc_k[j * nvp:(j + 1) * nvp]
        vo_ref[pl.ds(j, nvp, stride=nj), :] = acc_v[j * nvp:(j + 1) * nvp]


def _compress(k, v, wk, wv):
    nl, pages, _, r = k.shape
    npg = next(c for c in range(min(max(CMP_ROWS // r, 1), pages), 0, -1) if pages % c == 0)
    nblk = npg * r // CMP_BLOCK
    rspec = pl.BlockSpec((None, npg, D_KV, r), lambda l, i: (l, i, 0, 0))
    wspec = pl.BlockSpec((None, CMP_BLOCK, D_KV, D_KV), lambda l, i: (l, 0, 0, 0))
    ospec = pl.BlockSpec((None, nblk, D_KV), lambda l, i: (l, i, 0))
    return pl.pallas_call(
        functools.partial(_compress_body, npg=npg, r=r),
        grid=(nl, pages // npg),
        in_specs=[rspec, rspec, wspec, wspec],
        out_specs=[ospec, ospec],
        out_shape=[jax.ShapeDtypeStruct((nl, pages * r // CMP_BLOCK, D_KV), F32)] * 2,
        scratch_shapes=[pltpu.VMEM((npg * (r // LANES) * CMP_PITCH, D_KV), F32)] * 2,
        compiler_params=pltpu.CompilerParams(dimension_semantics=("arbitrary", "arbitrary"),
                                             vmem_limit_bytes=VMEM_LIMIT),
        name="compress",
    )(k, v, wk, wv)


def _make_qbd(q, tq):
    lo = _iota((tq, LANES), 1) < HEAD_DIM
    blocks = {}
    for jv in range(D_ATTN // LANES):
        a = q[:, LANES * jv:LANES * (jv + 1)] * ATTN_SCALE
        r = pltpu.roll(a, HEAD_DIM, 1)
        for half in range(2):
            h = 2 * jv + half
            kv, g = h // GROUP, h % GROUP
            if kv == 0:
                blocks[(g, kv)] = jnp.where(lo, a if half == 0 else r, 0.0)
            else:
                blocks[(g, kv)] = jnp.where(lo, 0.0, a if half == 1 else r)
    return jnp.concatenate([blocks[(g, kv)] for g in range(GROUP) for kv in range(N_KV)], axis=0).astype(BF16)


def _extract_heads(o, tq):
    lo = _iota((tq, LANES), 1) < HEAD_DIM
    cols = []
    for jv in range(D_ATTN // LANES):
        parts = []
        for half in range(2):
            h = 2 * jv + half
            kv, g = h // GROUP, h % GROUP
            blk = o[(g * N_KV + kv) * tq:(g * N_KV + kv + 1) * tq]
            if (half == 1) != (kv == 1):
                blk = pltpu.roll(blk, HEAD_DIM, 1)
            parts.append(blk)
        cols.append(jnp.where(lo, parts[0], parts[1]))
    return jnp.concatenate(cols, axis=1)


def _gate_rows(gate, c, tq):
    return jnp.concatenate([gate[:, c * N_HEADS + kv * GROUP + g:c * N_HEADS + kv * GROUP + g + 1]
                            for g in range(GROUP) for kv in range(N_KV)], axis=0)


def _softmax_rows(s, valid):
    s = jnp.where(valid, s, NEG_INF)
    m = jnp.max(s, axis=-1, keepdims=True)
    e = jnp.exp(s - m)
    p = e / jnp.sum(e, axis=-1, keepdims=True)
    return jnp.where(valid, p, 0.0)


def _select_blocks(imp, qpos2, blkf, n_valid_imp, k, axis):
    cur = (qpos2 // SEL_BLOCK).astype(F32)
    imp = jnp.where(blkf < n_valid_imp, imp, 0.0)
    score = jnp.where(blkf == cur, 2.0 * GROUP, jnp.where(blkf < cur, imp, -1.0))
    score = jnp.where(blkf < 0.0, -3.0, score)
    sel = jnp.zeros(score.shape, F32)
    for _ in range(k):
        m = jnp.max(score, axis=axis, keepdims=True)
        idx = jnp.min(jnp.where(score == m, blkf, 1e9), axis=axis, keepdims=True)
        hit = blkf == idx
        sel = jnp.where(hit, 1.0, sel)
        score = jnp.where(hit, -3.0, score)
    return sel


def _sum_groups(p, rows2):
    out = p[0:rows2]
    for g in range(1, GROUP):
        out = out + p[g * rows2:(g + 1) * rows2]
    return out


def _attn_prompt_body(q_ref, gate_ref, kc_ref, vc_ref, ks_ref, vs_ref, kw_ref, vw_ref,
                      tcol_ref, qs_ref, blkc_ref, trow_ref, e_ref, pos_ref, cpos_ref, gexp_ref, o_ref,
                      kas_ref, vas_ref, kaw_ref, vaw_ref, ms_ref, as_ref, mw_ref, aw_ref, *, tq, tk, seq, top_k):
    qt = pl.program_id(1)
    q0 = qt * tq
    rows = N_HEADS * tq
    rows2 = N_KV * tq
    nt = seq // tk

    @pl.when(qt == 0)
    def _():
        ones = jnp.ones((D_KV, tk), BF16)
        for j in range(nt):
            cs = slice(j * tk, (j + 1) * tk)
            for src, dst in ((ks_ref, kas_ref), (kw_ref, kaw_ref)):
                dst[j, 0:D_KV, :] = src[0, :, cs].astype(BF16)
                dst[j, D_KV:2 * D_KV, :] = pos_ref[j]
            for src, dst in ((vs_ref, vas_ref), (vw_ref, vaw_ref)):
                dst[j, 0:D_KV, :] = src[0, :, cs].astype(BF16)
                dst[j, D_KV:2 * D_KV, :] = ones

    qbd = _make_qbd(q_ref[0], tq)
    qaug = jnp.concatenate([qbd, qs_ref[...]], axis=1)
    qpos2 = q0 + tcol_ref[...]

    g_hi = gate_ref[0].astype(BF16)
    g_r1 = gate_ref[0] - g_hi.astype(F32)
    g_mid = g_r1.astype(BF16)
    g_lo = (g_r1 - g_mid.astype(F32)).astype(BF16)
    g_all = (jnp.dot(jnp.concatenate([g_hi, g_mid], axis=1), gexp_ref[...], preferred_element_type=F32)
             + jnp.dot(g_lo, gexp_ref[0:LANES, :], preferred_element_type=F32))

    def gate_rows(c):
        return jnp.concatenate([g_all[:, (c * N_HEADS + bi) * LANES:(c * N_HEADS + bi + 1) * LANES]
                                for bi in range(N_HEADS)], axis=0)

    ncb = seq // CMP_BLOCK
    pad = jnp.zeros((LANES - ncb, D_KV), F32)
    kc = jnp.concatenate([kc_ref[0], pad], axis=0)
    vc = jnp.concatenate([vc_ref[0], pad], axis=0)
    kcaug = jnp.concatenate([kc.T.astype(BF16), cpos_ref[...]], axis=0)
    s_c = jnp.dot(qaug, kcaug, preferred_element_type=F32).reshape(GROUP, rows2, LANES)
    blk_end = (_iota((1, LANES), 1) + 1) * CMP_BLOCK - 1
    p_c = _softmax_rows(s_c, (qpos2 >= blk_end)[None]).reshape(rows, LANES)
    o_acc = gate_rows(0) * _mm(p_c, vc)

    p4 = _sum_groups(p_c, rows2)
    nbr = blkc_ref.shape[0]
    imp_t = (p4 + pltpu.roll(p4, LANES - 1, 1)).T[0:nbr]
    sel = _select_blocks(imp_t, q0 + trow_ref[...], blkc_ref[...], float(seq // SEL_BLOCK), top_k, 0).T
    sel = sel.astype(BF16)

    def make_step(kaug_ref, vaug_ref, mask_fn, m_ref, acc_ref):
        def step(j, carry):
            s = jnp.dot(qaug, kaug_ref[j], preferred_element_type=F32).reshape(GROUP, rows2, tk)
            dist = qpos2 - (j * tk + _iota((1, tk), 1))
            s = jnp.where(mask_fn(j, dist)[None], s, NEG_INF)
            m = m_ref[...]
            m_new = jnp.maximum(m, jnp.max(s, axis=2, keepdims=True))
            alpha = jnp.exp(m - m_new).reshape(rows, LANES)
            p = jnp.exp(s - jnp.concatenate([m_new] * (tk // LANES), axis=2)).astype(BF16).reshape(rows, tk)
            pv = lax.dot_general(p, vaug_ref[j], (((1,), (1,)), ((), ())), preferred_element_type=F32)
            m_ref[...] = m_new
            acc_ref[...] = jnp.concatenate([alpha, alpha], axis=1) * acc_ref[...] + pv
            return carry

        return step

    def sel_mask(j, dist):
        picked = jnp.dot(sel, e_ref[j], preferred_element_type=F32)
        return (picked > 0.5) & (dist >= 0)

    def win_mask(j, dist):
        return (dist >= 0) & (dist <= WINDOW)

    step_s = make_step(kas_ref, vas_ref, sel_mask, ms_ref, as_ref)
    step_w = make_step(kaw_ref, vaw_ref, win_mask, mw_ref, aw_ref)
    for m_ref, acc_ref in ((ms_ref, as_ref), (mw_ref, aw_ref)):
        m_ref[...] = jnp.full((GROUP, rows2, LANES), NEG_INF, F32)
        acc_ref[...] = jnp.zeros((rows, 2 * D_KV), F32)
    n_s = (q0 + tq - 1) // tk + 1
    j_lo = jnp.maximum(q0 - WINDOW, 0) // tk
    n_w = n_s - j_lo

    def both(i, carry):
        step_s(i, carry)
        return step_w(j_lo + i, carry)

    lax.fori_loop(0, n_w, both, 0)
    lax.fori_loop(n_w, n_s, step_s, 0)
    o_s = as_ref[:, 0:D_KV] / as_ref[:, D_KV:2 * D_KV]
    o_w = aw_ref[:, 0:D_KV] / aw_ref[:, D_KV:2 * D_KV]
    o_acc = o_acc + gate_rows(1) * o_s + gate_rows(2) * o_w
    o_ref[0] = _extract_heads(o_acc, tq)


def _row_consts(tq):
    r = jnp.arange(N_HEADS * tq)
    g, kv, t = r // (N_KV * tq), (r // tq) % N_KV, r % tq
    head = kv * GROUP + g
    slopes = jnp.exp2(-8.0 * (head + 1).astype(F32) / N_HEADS)
    return slopes[:, None], t.astype(jnp.int32)[:, None]


def _gate_expander():
    n = jnp.arange(N_BRANCH * N_HEADS)
    c, bi = n // N_HEADS, n % N_HEADS
    src_lane = c * N_HEADS + (bi % N_KV) * GROUP + bi // N_KV
    r = (jnp.arange(LANES)[:, None] == jnp.repeat(src_lane, LANES)[None, :]).astype(BF16)
    return jnp.concatenate([r, r], axis=0)


def _key_tiles(a, tk):
    return a.reshape(a.shape[0], -1, tk).transpose(1, 0, 2)


def _attn_prompt(q, gate, kcmp, vcmp, ks, vs, kw, vw):
    b, t, _ = q.shape
    tq = min(ATTN_TQ, t)
    tk = min(ATTN_TK, t)
    n_sb = t // SEL_BLOCK
    assert t // POS_SPLIT < 256
    slopes, tcol = _row_consts(tq)
    qs = jnp.zeros((N_HEADS * tq, LANES), F32).at[:, 0].set(POS_SPLIT * slopes[:, 0]).at[:, 1].set(slopes[:, 0])
    kpos = jnp.arange(t)
    pos = jnp.zeros((LANES, t), F32).at[0].set(kpos // POS_SPLIT).at[1].set(kpos % POS_SPLIT)
    lane = jnp.arange(LANES)
    blk = jnp.where((lane % 2 == 0) & (lane < 2 * n_sb), lane // 2, -1).astype(F32)
    e = (blk[:, None] == (kpos // SEL_BLOCK).astype(F32)[None, :])
    blk_end = (lane + 1) * CMP_BLOCK - 1
    cpos = jnp.zeros((LANES, LANES), F32).at[0].set(blk_end // POS_SPLIT).at[1].set(blk_end % POS_SPLIT)
    tcol2 = tcol[:N_KV * tq]
    nbr = min(LANES, -(-2 * n_sb // SUBLANES) * SUBLANES)
    consts = (tcol2, qs.astype(BF16), blk[:nbr, None], tcol2.reshape(1, -1),
              _key_tiles(e[:nbr].astype(BF16), tk), _key_tiles(pos.astype(BF16), tk), cpos.astype(BF16),
              _gate_expander())
    tile = lambda n: pl.BlockSpec((1, tq, n), lambda i, j: (i, j, 0))
    seq = lambda r, n: pl.BlockSpec((1, r, n), lambda i, j: (i, 0, 0))
    return pl.pallas_call(
        functools.partial(_attn_prompt_body, tq=tq, tk=tk, seq=t, top_k=min(TOP_K, n_sb)),
        grid=(b, t // tq),
        in_specs=[tile(D_ATTN), tile(LANES), seq(t // CMP_BLOCK, D_KV), seq(t // CMP_BLOCK, D_KV),
                  seq(D_KV, t), seq(D_KV, t), seq(D_KV, t), seq(D_KV, t)] + [_full(c.shape) for c in consts],
        out_specs=tile(D_ATTN),
        out_shape=jax.ShapeDtypeStruct((b, t, D_ATTN), F32),
        scratch_shapes=[pltpu.VMEM((t // tk, 2 * D_KV, tk), BF16)] * 4
        + [pltpu.VMEM((GROUP, N_KV * tq, LANES), F32), pltpu.VMEM((N_HEADS * tq, 2 * D_KV), F32)] * 2,
        compiler_params=pltpu.CompilerParams(dimension_semantics=("arbitrary", "arbitrary"),
                                             vmem_limit_bytes=VMEM_LIMIT),
        name="attn_prompt",
    )(q, gate, kcmp, vcmp, ks, vs, kw, vw, *consts)


def _attn_sample_body(pt_ref, q_ref, gate_ref, ksn_ref, vsn_ref, kwn_ref, vwn_ref, kwin_ref, vwin_ref,
                      kcp_ref, vcp_ref, cvalid_ref, bias_c_ref, bias_s_ref, bias_w_ref, qrow_ref, blkc_ref, e_ref, *rest,
                      t, n_pages, page, past, w_buf, top_k, nbs):
    ks_pages = rest[:nbs * n_pages]
    vs_pages = rest[nbs * n_pages:2 * nbs * n_pages]
    (o_ref,) = rest[2 * nbs * n_pages:]
    b0 = pl.program_id(0) * nbs
    rows = N_HEADS * t
    rows2 = N_KV * t
    seqs = range(nbs)

    def stack(xs):
        return jnp.concatenate(xs, axis=0)

    def part(x, bb, n):
        return x[bb * n:(bb + 1) * n]

    def pad_rows(x):
        if x.shape[0] == LANES:
            return x
        return jnp.concatenate([x, jnp.zeros((LANES - x.shape[0], D_KV), F32)], axis=0)

    live = NEG_INF / 2
    qbd = [_make_qbd(q_ref[bb], t) for bb in seqs]
    gates = [gate_ref[bb] for bb in seqs]

    def gate_rows(c):
        return stack([_gate_rows(gates[bb], c, t) for bb in seqs])

    kc = [pad_rows(stack([kcp_ref[pt_ref[(b0 + bb) * n_pages + p] // 2] for p in range(n_pages)])) for bb in seqs]
    vc = [pad_rows(stack([vcp_ref[pt_ref[(b0 + bb) * n_pages + p] // 2] for p in range(n_pages)])) for bb in seqs]
    bias_c = bias_c_ref[...]
    s_c = stack([_mm_nt(qbd[bb], kc[bb]) for bb in seqs]) + bias_c
    cvalid = stack([jnp.broadcast_to(cvalid_ref[bb], (rows, LANES)) for bb in seqs])
    p_c = _softmax_rows(s_c, (cvalid > 0.5) & (bias_c > live))
    o_acc = gate_rows(0) * stack([_mm(part(p_c, bb, rows), vc[bb]) for bb in seqs])

    p4 = stack([_sum_groups(part(p_c, bb, rows), rows2) for bb in seqs])
    x1 = p4 + pltpu.roll(p4, LANES - 1, 1)
    imp = x1 + pltpu.roll(x1, LANES - 4, 1)
    n2 = nbs * rows2
    imp_t = jnp.concatenate([imp, jnp.zeros((LANES - n2, LANES), F32)], axis=0).T
    sel = _select_blocks(imp_t, qrow_ref[...], blkc_ref[...], float(past // SEL_BLOCK), top_k, 0).T[0:n2]
    sel4 = stack([part(sel, bb, rows2) for bb in seqs for _ in range(GROUP)])

    s_s = stack([jnp.concatenate([_mm(qbd[bb], ks_pages[bb * n_pages + p][...]) for p in range(n_pages)]
                                 + [_mm_nt(qbd[bb], pad_rows(ksn_ref[bb]))], axis=1) for bb in seqs])
    bias_s = bias_s_ref[...]
    picked = _mm(sel4, e_ref[...])
    p_s = _softmax_rows(s_s + bias_s, (picked > 0.5) & (bias_s > live))
    o_s = []
    for bb in seqs:
        pb = part(p_s, bb, rows)
        acc = _mm(pb[:, past:], pad_rows(vsn_ref[bb]))
        for p in range(n_pages):
            acc = acc + _mm_nt(pb[:, p * page:(p + 1) * page], vs_pages[bb * n_pages + p][...])
        o_s.append(acc)
    o_acc = o_acc + gate_rows(1) * stack(o_s)

    kwin = [kwin_ref[bb] for bb in seqs]
    vwin = [vwin_ref[bb] for bb in seqs]
    s_w = stack([jnp.concatenate([_mm(qbd[bb], kwin[bb]), _mm_nt(qbd[bb], pad_rows(kwn_ref[bb]))], axis=1)
                 for bb in seqs])
    bias_w = bias_w_ref[...]
    p_w = _softmax_rows(s_w + bias_w, bias_w > live)
    o_w = stack([_mm_nt(part(p_w, bb, rows)[:, :w_buf], vwin[bb])
                 + _mm(part(p_w, bb, rows)[:, w_buf:], pad_rows(vwn_ref[bb])) for bb in seqs])
    o_acc = o_acc + gate_rows(2) * o_w
    for bb in seqs:
        o_ref[bb] = _extract_heads(part(o_acc, bb, rows), t)


def _window_update_body(kwin_ref, vwin_ref, kn_ref, vn_ref, kwo_ref, vwo_ref, *, t, w_buf, nbs):
    is_new = _iota((D_KV, LANES), 1) >= LANES - t

    def shift_in(win, new):
        new_t = jnp.concatenate([jnp.zeros((LANES - t, D_KV), F32), new], axis=0).T
        rolled = pltpu.roll(win, w_buf - t, 1)
        last = jnp.where(is_new, new_t, rolled[:, w_buf - LANES:])
        return jnp.concatenate([rolled[:, :w_buf - LANES], last], axis=1)

    for bb in range(nbs):
        kwo_ref[bb] = shift_in(kwin_ref[bb], kn_ref[bb])
        vwo_ref[bb] = shift_in(vwin_ref[bb], vn_ref[bb])


def _window_update(kwin, vwin, kw_new, vw_new):
    nl, nb, _, w_buf = kwin.shape
    t = kw_new.shape[2]
    assert w_buf % LANES == 0 and t <= LANES
    nbs = next(c for c in range(min(WIN_STEP, nb), 0, -1) if nb % c == 0)
    win = pl.BlockSpec((None, nbs, D_KV, w_buf), lambda l, i: (l, i, 0, 0))
    new = pl.BlockSpec((None, nbs, t, D_KV), lambda l, i: (l, i, 0, 0))
    return pl.pallas_call(
        functools.partial(_window_update_body, t=t, w_buf=w_buf, nbs=nbs),
        grid=(nl, nb // nbs),
        in_specs=[win, win, new, new],
        out_specs=[win, win],
        out_shape=[jax.ShapeDtypeStruct(kwin.shape, F32)] * 2,
        compiler_params=pltpu.CompilerParams(dimension_semantics=("arbitrary", "arbitrary"),
                                             vmem_limit_bytes=VMEM_LIMIT),
        name="window_update",
    )(kwin, vwin, kw_new, vw_new)


def _page_map(l, row_stride, offset, i, pt):
    return (l, pt[i * row_stride + offset], 0, 0)


def _attn_sample(l, page_table, q, gate, ks_new, vs_new, kw_new, vw_new, kwin, vwin, kcp, vcp,
                 cache_ks, cache_vs):
    nb, t, _ = q.shape
    n_pages = page_table.shape[1]
    page = cache_ks.shape[3]
    past = n_pages * page
    w_buf = kwin.shape[3]
    n_sb = -(-(past + t) // SEL_BLOCK)
    assert t == SUBLANES and n_pages * 8 <= LANES and page == LANES and past % SEL_BLOCK == 0
    assert w_buf % LANES == 0
    slopes, tcol = _row_consts(t)
    lane = jnp.arange(LANES)
    pg, r = lane // 8, lane % 8
    in_range = pg < n_pages
    cend = jnp.where(in_range, (4 * pg + r % 4 + 1) * CMP_BLOCK - 1, 1 << 30).astype(jnp.int32)[None, :]
    par = page_table % 2
    par_l = jnp.take(par, jnp.minimum(pg, n_pages - 1), axis=1)
    cvalid = ((r[None, :] // 4 == par_l) & in_range[None, :]).astype(F32)[:, None, :]
    blk = jnp.where(in_range & (r % 2 == 0) & (r < 4), 2 * pg + r // 2, -1)
    blk = jnp.where(lane == 1, n_sb - 1, blk)
    blkf = blk.astype(F32)[None, :]
    e = (blkf.reshape(-1, 1) == (jnp.arange(past + LANES) // SEL_BLOCK).astype(F32)[None, :]).astype(BF16)
    nbs = next(c for c in range(min(DEC_STEP, nb), 0, -1) if nb % c == 0)
    qpos = past + tcol
    assert nbs * N_KV * t <= LANES
    qrow = jnp.pad(jnp.tile(qpos[:N_KV * t, 0], nbs), (0, LANES - nbs * N_KV * t), constant_values=past)[None, :]
    blkc = blkf.reshape(-1, 1)

    def bias(dist, ok):
        return jnp.tile(jnp.where(ok, -(slopes * dist.astype(F32)), NEG_INF), (nbs, 1))

    dist_c = qpos - cend
    dist_s = qpos - jnp.arange(past + LANES)[None, :]
    dist_w = qpos - (past - w_buf + jnp.arange(w_buf + LANES))[None, :]
    bias_c = bias(dist_c, dist_c >= 0)
    bias_s = bias(dist_s, dist_s >= 0)
    bias_w = bias(dist_w, (dist_w >= 0) & (dist_w <= WINDOW))
    tok = lambda n: pl.BlockSpec((nbs, t, n), lambda i, pt: (i, 0, 0))
    win = pl.BlockSpec((None, nbs, D_KV, w_buf), lambda i, pt: (l, i, 0, 0))
    res = pl.BlockSpec((None,) + kcp.shape[1:], lambda i, pt: (l, 0, 0, 0))
    const = lambda a: pl.BlockSpec(a.shape, lambda i, pt: (0,) * a.ndim)
    pages = [pl.BlockSpec((None, None, D_KV, page), functools.partial(_page_map, l, nbs * n_pages, bb * n_pages + p))
             for bb in range(nbs) for p in range(n_pages)]
    grid_spec = pltpu.PrefetchScalarGridSpec(
        num_scalar_prefetch=1,
        grid=(nb // nbs,),
        in_specs=[tok(D_ATTN), tok(LANES), tok(D_KV), tok(D_KV), tok(D_KV), tok(D_KV), win, win, res, res,
                  pl.BlockSpec((nbs, 1, LANES), lambda i, pt: (i, 0, 0)),
                  const(bias_c), const(bias_s), const(bias_w), const(qrow), const(blkc), const(e)] + pages + pages,
        out_specs=[tok(D_ATTN)],
    )
    return pl.pallas_call(
        functools.partial(_attn_sample_body, t=t, n_pages=n_pages, page=page, past=past, w_buf=w_buf,
                          top_k=min(TOP_K, n_sb), nbs=nbs),
        grid_spec=grid_spec,
        out_shape=[jax.ShapeDtypeStruct((nb, t, D_ATTN), F32)],
        compiler_params=pltpu.CompilerParams(dimension_semantics=("arbitrary",), vmem_limit_bytes=VMEM_LIMIT),
        name="attn_sample",
    )(page_table.reshape(-1), q, gate, ks_new, vs_new, kw_new, vw_new, kwin, vwin, kcp, vcp, cvalid,
      bias_c, bias_s, bias_w, qrow, blkc, e, *([cache_ks] * (nbs * n_pages)), *([cache_vs] * (nbs * n_pages)))


def _back_head(x, o, pab, wg2_ref, woa_ref, wmix_ref, ln1g_ref, ln1b_ref):
    y_c = _mm(o, woa_ref[...])
    g2 = jax.nn.sigmoid(_mm(x, wg2_ref[...]))
    mixed = pab + g2 * y_c
    return _layer_norm(ALPHA * x + _mm(mixed, wmix_ref[...]), ln1g_ref[...], ln1b_ref[...])


def _back_prompt_body(x_ref, o_ref, pab_ref, wg2_ref, woa_ref, wmix_ref, ln1g_ref, ln1b_ref,
                      wup_ref, fcw_ref, fcb_ref, wdn_ref, ln2g_ref, ln2b_ref,
                      y_ref, fst_ref, ext_ref, *, tt):
    j = pl.program_id(1)

    @pl.when(j == 0)
    def _():
        ext_ref[0:8, :] = jnp.zeros((8, D_FF), F32)

    h = _back_head(x_ref[0], o_ref[0], pab_ref[0], wg2_ref, woa_ref, wmix_ref, ln1g_ref, ln1b_ref)
    hb = h.astype(BF16)
    cf = D_FF // FF_CHUNKS
    f = jnp.zeros((tt, D_MODEL), F32)
    for c in range(FF_CHUNKS):
        cs = slice(c * cf, (c + 1) * cf)
        u = _mm(hb, wup_ref[:, c * cf:(c + 1) * cf])
        gt = _mm(hb, wup_ref[:, D_FF + c * cf:D_FF + (c + 1) * cf])
        ext_ref[8:8 + tt, cs] = u
        conv = fcb_ref[:, cs] + fcw_ref[0:1, cs] * ext_ref[6:6 + tt, cs]
        conv = conv + fcw_ref[1:2, cs] * ext_ref[7:7 + tt, cs]
        conv = conv + fcw_ref[2:3, cs] * u
        f = f + _mm(jax.nn.gelu(conv) * gt, wdn_ref[c * cf:(c + 1) * cf, :])
    y_ref[0] = _layer_norm(ALPHA * h + f, ln2g_ref[...], ln2b_ref[...])
    last = ext_ref[tt:tt + 8, :]
    fst_ref[0] = last
    ext_ref[0:8, :] = last


def _back_sample_body(x_ref, o_ref, pab_ref, hf_ref, wg2_ref, woa_ref, wmix_ref, ln1g_ref, ln1b_ref,
                      wup_ref, fcw_ref, fcb_ref, wdn_ref, ln2g_ref, ln2b_ref,
                      y_ref, fst_ref, ext_ref, *, ns, t):
    h = _back_head(x_ref[...], o_ref[...], pab_ref[...], wg2_ref, woa_ref, wmix_ref, ln1g_ref, ln1b_ref)
    hb = h.astype(BF16)
    cf = D_FF // FF_CHUNKS
    ext_ref[:, 6:8, :] = hf_ref[...]
    f = jnp.zeros((ns * t, D_MODEL), F32)
    for c in range(FF_CHUNKS):
        cs = slice(c * cf, (c + 1) * cf)
        u = _mm(hb, wup_ref[:, c * cf:(c + 1) * cf])
        gt = _mm(hb, wup_ref[:, D_FF + c * cf:D_FF + (c + 1) * cf])
        ext_ref[:, 8:8 + t, cs] = u.reshape(ns, t, cf)
        fcw = fcw_ref[:, cs]
        conv = fcb_ref[:, cs] + fcw[0:1] * ext_ref[:, 6:6 + t, cs]
        conv = conv + fcw[1:2] * ext_ref[:, 7:7 + t, cs]
        conv = conv + fcw[2:3] * ext_ref[:, 8:8 + t, cs]
        f = f + _mm(jax.nn.gelu(conv).reshape(ns * t, cf) * gt, wdn_ref[c * cf:(c + 1) * cf, :])
    y_ref[...] = _layer_norm(ALPHA * h + f, ln2g_ref[...], ln2b_ref[...])
    fst_ref[...] = ext_ref[:, t:t + 8, :]


def _back_prompt(x, o, pab, wts):
    b, t, _ = x.shape
    tt = min(FRONT_TILE, t)
    tok = lambda n: pl.BlockSpec((1, tt, n), lambda i, j: (i, j, 0))
    return pl.pallas_call(
        functools.partial(_back_prompt_body, tt=tt),
        grid=(b, t // tt),
        in_specs=[tok(D_MODEL), tok(D_ATTN), tok(D_MODEL)] + [_full(w.shape) for w in wts],
        out_specs=[tok(D_MODEL), pl.BlockSpec((1, 8, D_FF), lambda i, j: (i, 0, 0))],
        out_shape=[jax.ShapeDtypeStruct((b, t, D_MODEL), F32), jax.ShapeDtypeStruct((b, 8, D_FF), F32)],
        scratch_shapes=[pltpu.VMEM((8 + tt, D_FF), F32)],
        compiler_params=pltpu.CompilerParams(dimension_semantics=("arbitrary", "arbitrary"),
                                             vmem_limit_bytes=VMEM_LIMIT),
        name="back_prompt",
    )(x, o, pab, *wts)


def _back_sample(x, o, pab, hist_f, wts, t):
    n = x.shape[0]
    nb = n // t
    ns = min(FRONT_TILE // t, nb)
    rows = ns * t
    tok = lambda c: pl.BlockSpec((rows, c), lambda i: (i, 0))
    return pl.pallas_call(
        functools.partial(_back_sample_body, ns=ns, t=t),
        grid=(nb // ns,),
        in_specs=[tok(D_MODEL), tok(D_ATTN), tok(D_MODEL),
                  pl.BlockSpec((ns, CONV_W - 1, D_FF), lambda i: (i, 0, 0))] + [_full(w.shape) for w in wts],
        out_specs=[tok(D_MODEL), pl.BlockSpec((ns, 8, D_FF), lambda i: (i, 0, 0))],
        out_shape=[jax.ShapeDtypeStruct((n, D_MODEL), F32), jax.ShapeDtypeStruct((nb, 8, D_FF), F32)],
        scratch_shapes=[pltpu.VMEM((ns, 8 + t, D_FF), F32)],
        compiler_params=pltpu.CompilerParams(dimension_semantics=("arbitrary",), vmem_limit_bytes=VMEM_LIMIT),
        name="back_sample",
    )(x, o, pab, hist_f, *wts)


def _block_diag2(w):
    z = jnp.zeros_like(w)
    return jnp.concatenate([jnp.concatenate([w, z], axis=-1), jnp.concatenate([z, w], axis=-1)], axis=-2)


def kernel(x_prompt, x_sample, cache_k_cmp, cache_v_cmp, cache_k_sel, cache_v_sel, state_k_win, state_v_win, state_conv, state_pool, state_ffn_conv, page_table, w_in, conv_w, conv_b, w_out_conv, pool_w, pool_scale, w_cmp_k, w_cmp_v, w_out_attn, w_mix_out, ln1_g, ln1_b, w_ffn_up, ffn_conv_w, ffn_conv_b, w_ffn_down, ln2_g, ln2_b):
    depth = w_in.shape[0]
    bp, seq, _ = x_prompt.shape
    nb, t, _ = x_sample.shape
    n_phys, page = cache_k_cmp.shape[1], cache_k_cmp.shape[2]
    past = page_table.shape[1] * page
    w_buf = state_k_win.shape[2]

    w_front = jnp.concatenate(
        [w_in[:, :, :IN_GATES], jnp.zeros((depth, D_MODEL, C_G01 - IN_GATES), F32),
         w_in[:, :, IN_GATES:IN_GATES + 2 * D_MODEL]], axis=2).astype(BF16)
    w_g2 = w_in[:, :, IN_GATES + 2 * D_MODEL:].astype(BF16)
    ng = pool_w.shape[1]
    pool_bd = jnp.zeros((depth, D_POOL, D_MODEL), F32)
    for g in range(ng):
        pool_bd = pool_bd.at[:, g * POOL_GROUP:(g + 1) * POOL_GROUP,
                             g * (D_MODEL // ng):(g + 1) * (D_MODEL // ng)].set(pool_w[:, g])
    pool_bd = pool_bd.astype(BF16)
    wck = _block_diag2(w_cmp_k).astype(BF16)
    wcv = _block_diag2(w_cmp_v).astype(BF16)
    w_oc, w_oa, w_mix = w_out_conv.astype(BF16), w_out_attn.astype(BF16), w_mix_out.astype(BF16)
    w_up, w_dn = w_ffn_up.astype(BF16), w_ffn_down.astype(BF16)
    row = lambda a: a[:, None, :]

    keys_minor = lambda a: jnp.transpose(a, (0, 1, 3, 4, 2)).reshape(a.shape[0], a.shape[1], D_KV, a.shape[2])
    tokens_major = lambda a: jnp.transpose(a.reshape(a.shape[0], N_KV, HEAD_DIM, a.shape[2]), (0, 3, 1, 2))
    ck_s, cv_s = keys_minor(cache_k_sel), keys_minor(cache_v_sel)
    kwin_all, vwin_all = keys_minor(state_k_win), keys_minor(state_v_win)
    kcp, vcp = _compress(keys_minor(cache_k_cmp), keys_minor(cache_v_cmp), wck, wcv)
    kcp = kcp.reshape(depth, n_phys // 2, 8, D_KV)
    vcp = vcp.reshape(depth, n_phys // 2, 8, D_KV)

    y_p = x_prompt
    y_s = x_sample.reshape(nb * t, D_MODEL)
    st_p, st_s = [], []
    for l in range(depth):
        fw = (w_front[l], conv_w[l], row(conv_b)[l], w_oc[l], pool_bd[l], row(pool_scale)[l])
        bw = (w_g2[l], w_oa[l], w_mix[l], row(ln1_g)[l], row(ln1_b)[l], w_up[l], ffn_conv_w[l],
              row(ffn_conv_b)[l], w_dn[l], row(ln2_g)[l], row(ln2_b)[l])

        pab, q, kc, vc, ks, vs, kw, vw, gate, cst, pst = _front_prompt(y_p, fw)
        kcmp, vcmp = _compress(kc[None], vc[None], wck[l:l + 1], wcv[l:l + 1])
        o = _attn_prompt(q, gate, kcmp.reshape(bp, seq // CMP_BLOCK, D_KV),
                         vcmp.reshape(bp, seq // CMP_BLOCK, D_KV), ks, vs, kw, vw)
        y_p, fst = _back_prompt(y_p, o, pab, bw)
        st_p.append(tuple(tokens_major(a) for a in (kc, vc, ks, vs, kw[:, :, seq - w_buf:], vw[:, :, seq - w_buf:]))
                    + (cst[:, 8 - (CONV_W - 1):], pst[:, 16 - POOL_HIST:], fst[:, 8 - (CONV_W - 1):]))

        pab, q, kc, vc, ks, vs, kw, vw, gate, cst, pst = _front_sample(
            y_s.reshape(nb, t, D_MODEL), state_conv[l], state_pool[l], fw, past)
        r3 = lambda a: a.reshape(nb, t, a.shape[-1])
        (o,) = _attn_sample(l, page_table, r3(q), r3(gate), r3(ks), r3(vs), r3(kw), r3(vw),
                            kwin_all, vwin_all, kcp, vcp, ck_s, cv_s)
        y_s, fst = _back_sample(y_s, o.reshape(nb * t, D_ATTN), pab, state_ffn_conv[l], bw, t)
        st_s.append((kc, vc, ks, vs, kw, vw,
                     cst[:, 8 - (CONV_W - 1):], pst[:, 16 - POOL_HIST:], fst[:, 8 - (CONV_W - 1):]))

    kc_p, vc_p, ks_p, vs_p, kw_p, vw_p, conv_p, pool_p, ffn_p = [jnp.stack(a) for a in zip(*st_p)]
    kc_s, vc_s, ks_s, vs_s, kwn_s, vwn_s, conv_s, pool_s, ffn_s = [jnp.stack(a) for a in zip(*st_s)]
    kc_s, vc_s, ks_s, vs_s = [a.reshape(depth, nb, t, N_KV, HEAD_DIM) for a in (kc_s, vc_s, ks_s, vs_s)]
    kw_s, vw_s = _window_update(kwin_all, vwin_all, kwn_s.reshape(depth, nb, t, D_KV),
                                vwn_s.reshape(depth, nb, t, D_KV))
    kw_s, vw_s = [jnp.transpose(a.reshape(depth, nb, N_KV, HEAD_DIM, w_buf), (0, 1, 4, 2, 3)) for a in (kw_s, vw_s)]
    return (y_p, y_s.reshape(nb, t, D_MODEL), kc_p, kc_s, vc_p, vc_s, ks_p, ks_s, vs_p, vs_s,
            kw_p, kw_s, vw_p, vw_s, conv_p, conv_s, pool_p, pool_s, ffn_p, ffn_s)
```

```python
import functools

import jax
import jax.numpy as jnp
from jax import lax
from jax.experimental import pallas as pl
from jax.experimental.pallas import tpu as pltpu

F32 = jnp.float32
BF16 = jnp.bfloat16

D_MODEL = 1024
DEPTH = 4
D_CONV = 256
CONV_W = 3
D_POOL = 256
POOL_WINDOWS = (2, 4, 8, 16)
POOL_GROUP = D_POOL // len(POOL_WINDOWS)
POOL_HIST = 15
N_HEADS = 8
N_KV = 2
GROUP = N_HEADS // N_KV
HEAD_DIM = 64
D_ATTN = N_HEADS * HEAD_DIM
D_KV = N_KV * HEAD_DIM
CMP_BLOCK = 32
SEL_BLOCK = 64
TOP_K = 8
WINDOW = 512
N_BRANCH = 3
D_FF = 2816
ALPHA = (2.0 * DEPTH) ** 0.25
LN_EPS = 1e-5
ATTN_SCALE = HEAD_DIM ** -0.5
NEG_INF = -1e30
POS_SPLIT = 64

LANES = 128
SUBLANES = 8
VMEM_LIMIT = 56 * 1024 * 1024

C_AB, C_AC, C_AH, C_U, C_Q = 0, D_CONV, 2 * D_CONV, 3 * D_CONV, 3 * D_CONV + D_POOL
C_KV = C_Q + D_ATTN
C_NSA = C_KV + 6 * D_KV
C_G01 = C_NSA + LANES
IN_MAIN = C_NSA
IN_GATES = IN_MAIN + N_BRANCH * N_HEADS

FRONT_TILE = 256
ATTN_TQ = 128
ATTN_TK = 256
CMP_ROWS = 8192
CMP_PITCH = LANES + SUBLANES
DEC_STEP = 4
WIN_STEP = 8
FF_CHUNKS = 1


def _mm(a, b):
    return jnp.dot(a.astype(BF16), b.astype(BF16), preferred_element_type=F32)


def _mm_nt(a, b):
    return lax.dot_general(a.astype(BF16), b.astype(BF16), (((1,), (1,)), ((), ())),
                           preferred_element_type=F32)


def _layer_norm(x, g, b):
    mu = jnp.mean(x, axis=-1, keepdims=True)
    var = jnp.mean(jnp.square(x - mu), axis=-1, keepdims=True)
    return (x - mu) * lax.rsqrt(var + LN_EPS) * g + b


def _iota(shape, axis):
    return lax.broadcasted_iota(jnp.int32, shape, axis)


def _pool_delta(u, shifted, pos):
    nd = u.ndim
    acc = u
    sums = {}
    for j in range(1, POOL_WINDOWS[-1]):
        acc = acc + shifted(j)
        if j + 1 in POOL_WINDOWS:
            sums[j + 1] = acc
    grp = _iota((1,) * (nd - 1) + (D_POOL,), nd - 1) // POOL_GROUP
    s = sums[POOL_WINDOWS[-1]]
    win = jnp.full(grp.shape, POOL_WINDOWS[-1], jnp.int32)
    for g in range(len(POOL_WINDOWS) - 2, -1, -1):
        s = jnp.where(grp == g, sums[POOL_WINDOWS[g]], s)
        win = jnp.where(grp == g, POOL_WINDOWS[g], win)
    cnt = jnp.minimum(win, pos + 1).astype(F32)
    return s / cnt - u


def _front_tail(z, conv, d, woc_ref, pbd_ref, psc_ref, outs, prompt):
    pab_ref, q_ref, kv_refs, gate_ref = outs
    y_a = _mm(z[:, C_AB:C_AB + D_CONV] * conv, woc_ref[...])
    y_b = _mm(d, pbd_ref[...]) * psc_ref[...]
    g0 = jax.nn.sigmoid(z[:, C_G01:C_G01 + D_MODEL])
    g1 = jax.nn.sigmoid(z[:, C_G01 + D_MODEL:C_G01 + 2 * D_MODEL])
    pab_ref[...] = (g0 * y_a + g1 * y_b).reshape(pab_ref.shape)
    q_ref[...] = z[:, C_Q:C_Q + D_ATTN].reshape(q_ref.shape)
    for i, r in enumerate(kv_refs):
        kv = z[:, C_KV + i * D_KV:C_KV + (i + 1) * D_KV]
        r[...] = (kv.T if prompt else kv).reshape(r.shape)
    gate_ref[...] = jax.nn.sigmoid(z[:, C_NSA:C_NSA + LANES]).reshape(gate_ref.shape)


def _front_prompt_body(x_ref, w_ref, cw_ref, cb_ref, woc_ref, pbd_ref, psc_ref,
                       pab_ref, q_ref, kc_ref, vc_ref, ks_ref, vs_ref, kw_ref, vw_ref, gate_ref,
                       cst_ref, pst_ref, extc_ref, extu_ref, *, tt):
    j = pl.program_id(1)

    @pl.when(j == 0)
    def _():
        extc_ref[0:8, :] = jnp.zeros((8, D_CONV), F32)
        extu_ref[0:16, :] = jnp.zeros((16, D_POOL), F32)

    z = _mm(x_ref[0], w_ref[...])
    ch = z[:, C_AC:C_AC + D_CONV] * z[:, C_AH:C_AH + D_CONV]
    extc_ref[8:8 + tt, :] = ch
    conv = cb_ref[...] + cw_ref[0:1, :] * extc_ref[6:6 + tt, :]
    conv = conv + cw_ref[1:2, :] * extc_ref[7:7 + tt, :]
    conv = conv + cw_ref[2:3, :] * ch
    u = z[:, C_U:C_U + D_POOL]
    extu_ref[16:16 + tt, :] = u
    pos = j * tt + _iota((tt, 1), 0)
    d = _pool_delta(u, lambda s: extu_ref[16 - s:16 - s + tt, :], pos)
    _front_tail(z, conv, d, woc_ref, pbd_ref, psc_ref,
                (pab_ref, q_ref, (kc_ref, vc_ref, ks_ref, vs_ref, kw_ref, vw_ref), gate_ref), True)
    last_c = extc_ref[tt:tt + 8, :]
    last_u = extu_ref[tt:tt + 16, :]
    cst_ref[0] = last_c
    pst_ref[0] = last_u
    extc_ref[0:8, :] = last_c
    extu_ref[0:16, :] = last_u


def _front_sample_body(x_ref, hc_ref, hu_ref, w_ref, cw_ref, cb_ref, woc_ref, pbd_ref, psc_ref,
                       pab_ref, q_ref, kc_ref, vc_ref, ks_ref, vs_ref, kw_ref, vw_ref, gate_ref,
                       cst_ref, pst_ref, extc_ref, extu_ref, *, ns, t, pos0):
    z = _mm(x_ref[...], w_ref[...])
    ch = z[:, C_AC:C_AC + D_CONV] * z[:, C_AH:C_AH + D_CONV]
    extc_ref[:, 6:8, :] = hc_ref[...]
    extc_ref[:, 8:8 + t, :] = ch.reshape(ns, t, D_CONV)
    cw = cw_ref[...]
    conv = cb_ref[...] + cw[0:1] * extc_ref[:, 6:6 + t, :]
    conv = conv + cw[1:2] * extc_ref[:, 7:7 + t, :]
    conv = conv + cw[2:3] * extc_ref[:, 8:8 + t, :]
    u = z[:, C_U:C_U + D_POOL]
    extu_ref[:, 1:16, :] = hu_ref[...]
    extu_ref[:, 16:16 + t, :] = u.reshape(ns, t, D_POOL)
    pos = pos0 + _iota((1, t, 1), 1)
    d = _pool_delta(extu_ref[:, 16:16 + t, :], lambda s: extu_ref[:, 16 - s:16 - s + t, :], pos)
    _front_tail(z, conv.reshape(ns * t, D_CONV), d.reshape(ns * t, D_POOL), woc_ref, pbd_ref, psc_ref,
                (pab_ref, q_ref, (kc_ref, vc_ref, ks_ref, vs_ref, kw_ref, vw_ref), gate_ref), False)
    cst_ref[...] = extc_ref[:, t:t + 8, :]
    pst_ref[...] = extu_ref[:, t:t + 16, :]


def _full(shape):
    n = len(shape)
    return pl.BlockSpec(shape, lambda *_: (0,) * n, pipeline_mode=pl.Buffered(1))


def _front_prompt(x, wts):
    b, t, _ = x.shape
    tt = min(FRONT_TILE, t)
    tok = lambda n: pl.BlockSpec((1, tt, n), lambda i, j: (i, j, 0))
    st = lambda r, n: pl.BlockSpec((1, r, n), lambda i, j: (i, 0, 0))
    kvt = pl.BlockSpec((1, D_KV, tt), lambda i, j: (i, 0, j))
    out_shape = ([jax.ShapeDtypeStruct((b, t, D_MODEL), F32), jax.ShapeDtypeStruct((b, t, D_ATTN), F32)]
                 + [jax.ShapeDtypeStruct((b, D_KV, t), F32)] * 6
                 + [jax.ShapeDtypeStruct((b, t, LANES), F32),
                    jax.ShapeDtypeStruct((b, 8, D_CONV), F32), jax.ShapeDtypeStruct((b, 16, D_POOL), F32)])
    out_specs = ([tok(D_MODEL), tok(D_ATTN)] + [kvt] * 6 + [tok(LANES), st(8, D_CONV), st(16, D_POOL)])
    return pl.pallas_call(
        functools.partial(_front_prompt_body, tt=tt),
        grid=(b, t // tt),
        in_specs=[tok(D_MODEL)] + [_full(w.shape) for w in wts],
        out_specs=out_specs,
        out_shape=out_shape,
        scratch_shapes=[pltpu.VMEM((8 + tt, D_CONV), F32), pltpu.VMEM((16 + tt, D_POOL), F32)],
        compiler_params=pltpu.CompilerParams(dimension_semantics=("arbitrary", "arbitrary"),
                                             vmem_limit_bytes=VMEM_LIMIT),
        name="front_prompt",
    )(x, *wts)


def _front_sample(x, hist_c, hist_u, wts, pos0):
    nb, t, _ = x.shape
    ns = min(FRONT_TILE // t, nb)
    rows = ns * t
    xf = x.reshape(nb * t, D_MODEL)
    tok = lambda n: pl.BlockSpec((rows, n), lambda i: (i, 0))
    st = lambda r, n: pl.BlockSpec((ns, r, n), lambda i: (i, 0, 0))
    out_shape = ([jax.ShapeDtypeStruct((nb * t, D_MODEL), F32), jax.ShapeDtypeStruct((nb * t, D_ATTN), F32)]
                 + [jax.ShapeDtypeStruct((nb * t, D_KV), F32)] * 6
                 + [jax.ShapeDtypeStruct((nb * t, LANES), F32),
                    jax.ShapeDtypeStruct((nb, 8, D_CONV), F32), jax.ShapeDtypeStruct((nb, 16, D_POOL), F32)])
    out_specs = ([tok(D_MODEL), tok(D_ATTN)] + [tok(D_KV)] * 6 + [tok(LANES), st(8, D_CONV), st(16, D_POOL)])
    return pl.pallas_call(
        functools.partial(_front_sample_body, ns=ns, t=t, pos0=pos0),
        grid=(nb // ns,),
        in_specs=[tok(D_MODEL), st(CONV_W - 1, D_CONV), st(POOL_HIST, D_POOL)] + [_full(w.shape) for w in wts],
        out_specs=out_specs,
        out_shape=out_shape,
        scratch_shapes=[pltpu.VMEM((ns, 8 + t, D_CONV), F32), pltpu.VMEM((ns, 16 + t, D_POOL), F32)],
        compiler_params=pltpu.CompilerParams(dimension_semantics=("arbitrary",),
                                             vmem_limit_bytes=VMEM_LIMIT),
        name="front_sample",
    )(xf, hist_c, hist_u, *wts)


def _compress_body(k_ref, v_ref, wk_ref, wv_ref, ko_ref, vo_ref, kr_ref, vr_ref, *, npg, r):
    cols = r // LANES
    nvp = npg * cols
    for i in range(npg):
        for c in range(cols):
            row0 = (i * cols + c) * CMP_PITCH
            kr_ref[row0:row0 + LANES, :] = k_ref[i, :, c * LANES:(c + 1) * LANES].astype(BF16).T.astype(F32)
            vr_ref[row0:row0 + LANES, :] = v_ref[i, :, c * LANES:(c + 1) * LANES].astype(BF16).T.astype(F32)
    nj = LANES // CMP_BLOCK
    acc_k = jnp.zeros((nj * nvp, D_KV), F32)
    acc_v = jnp.zeros((nj * nvp, D_KV), F32)
    for l in range(CMP_BLOCK):
        rows_k = [kr_ref[pl.ds(j * CMP_BLOCK + l, nvp, stride=CMP_PITCH), :] for j in range(nj)]
        rows_v = [vr_ref[pl.ds(j * CMP_BLOCK + l, nvp, stride=CMP_PITCH), :] for j in range(nj)]
        acc_k = acc_k + _mm(jnp.concatenate(rows_k, axis=0), wk_ref[l])
        acc_v = acc_v + _mm(jnp.concatenate(rows_v, axis=0), wv_ref[l])
    for j in range(nj):
        ko_ref[pl.ds(j, nvp, stride=nj), :] = acc_k[j * nvp:(j + 1) * nvp]
        vo_ref[pl.ds(j, nvp, stride=nj), :] = acc_v[j * nvp:(j + 1) * nvp]


def _compress(k, v, wk, wv):
    nl, pages, _, r = k.shape
    npg = next(c for c in range(min(max(CMP_ROWS // r, 1), pages), 0, -1) if pages % c == 0)
    nblk = npg * r // CMP_BLOCK
    rspec = pl.BlockSpec((None, npg, D_KV, r), lambda l, i: (l, i, 0, 0))
    wspec = pl.BlockSpec((None, CMP_BLOCK, D_KV, D_KV), lambda l, i: (l, 0, 0, 0))
    ospec = pl.BlockSpec((None, nblk, D_KV), lambda l, i: (l, i, 0))
    return pl.pallas_call(
        functools.partial(_compress_body, npg=npg, r=r),
        grid=(nl, pages // npg),
        in_specs=[rspec, rspec, wspec, wspec],
        out_specs=[ospec, ospec],
        out_shape=[jax.ShapeDtypeStruct((nl, pages * r // CMP_BLOCK, D_KV), F32)] * 2,
        scratch_shapes=[pltpu.VMEM((npg * (r // LANES) * CMP_PITCH, D_KV), F32)] * 2,
        compiler_params=pltpu.CompilerParams(dimension_semantics=("arbitrary", "arbitrary"),
                                             vmem_limit_bytes=VMEM_LIMIT),
        name="compress",
    )(k, v, wk, wv)


def _make_qbd(q, tq):
    lo = _iota((tq, LANES), 1) < HEAD_DIM
    blocks = {}
    for jv in range(D_ATTN // LANES):
        a = q[:, LANES * jv:LANES * (jv + 1)] * ATTN_SCALE
        r = pltpu.roll(a, HEAD_DIM, 1)
        for half in range(2):
            h = 2 * jv + half
            kv, g = h // GROUP, h % GROUP
            if kv == 0:
                blocks[(g, kv)] = jnp.where(lo, a if half == 0 else r, 0.0)
            else:
                blocks[(g, kv)] = jnp.where(lo, 0.0, a if half == 1 else r)
    return jnp.concatenate([blocks[(g, kv)] for g in range(GROUP) for kv in range(N_KV)], axis=0).astype(BF16)


def _extract_heads(o, tq):
    lo = _iota((tq, LANES), 1) < HEAD_DIM
    cols = []
    for jv in range(D_ATTN // LANES):
        parts = []
        for half in range(2):
            h = 2 * jv + half
            kv, g = h // GROUP, h % GROUP
            blk = o[(g * N_KV + kv) * tq:(g * N_KV + kv + 1) * tq]
            if (half == 1) != (kv == 1):
                blk = pltpu.roll(blk, HEAD_DIM, 1)
            parts.append(blk)
        cols.append(jnp.where(lo, parts[0], parts[1]))
    return jnp.concatenate(cols, axis=1)


def _gate_rows(gate, c, tq):
    return jnp.concatenate([gate[:, c * N_HEADS + kv * GROUP + g:c * N_HEADS + kv * GROUP + g + 1]
                            for g in range(GROUP) for kv in range(N_KV)], axis=0)


def _softmax_rows(s, valid):
    s = jnp.where(valid, s, NEG_INF)
    m = jnp.max(s, axis=-1, keepdims=True)
    e = jnp.exp(s - m)
    p = e / jnp.sum(e, axis=-1, keepdims=True)
    return jnp.where(valid, p, 0.0)


def _select_blocks(imp, qpos2, blkf, n_valid_imp, k, axis):
    cur = (qpos2 // SEL_BLOCK).astype(F32)
    imp = jnp.where(blkf < n_valid_imp, imp, 0.0)
    score = jnp.where(blkf == cur, 2.0 * GROUP, jnp.where(blkf < cur, imp, -1.0))
    score = jnp.where(blkf < 0.0, -3.0, score)
    sel = jnp.zeros(score.shape, F32)
    for _ in range(k):
        m = jnp.max(score, axis=axis, keepdims=True)
        idx = jnp.min(jnp.where(score == m, blkf, 1e9), axis=axis, keepdims=True)
        hit = blkf == idx
        sel = jnp.where(hit, 1.0, sel)
        score = jnp.where(hit, -3.0, score)
    return sel


def _sum_groups(p, rows2):
    out = p[0:rows2]
    for g in range(1, GROUP):
        out = out + p[g * rows2:(g + 1) * rows2]
    return out


def _attn_prompt_body(q_ref, gate_ref, kc_ref, vc_ref, ks_ref, vs_ref, kw_ref, vw_ref,
                      tcol_ref, qs_ref, blkc_ref, trow_ref, e_ref, pos_ref, cpos_ref, gexp_ref, o_ref,
                      kas_ref, vas_ref, kaw_ref, vaw_ref, ms_ref, as_ref, mw_ref, aw_ref, *, tq, tk, seq, top_k):
    qt = pl.program_id(1)
    q0 = qt * tq
    rows = N_HEADS * tq
    rows2 = N_KV * tq
    nt = seq // tk

    @pl.when(qt == 0)
    def _():
        ones = jnp.ones((D_KV, tk), BF16)
        for j in range(nt):
            cs = slice(j * tk, (j + 1) * tk)
            for src, dst in ((ks_ref, kas_ref), (kw_ref, kaw_ref)):
                dst[j, 0:D_KV, :] = src[0, :, cs].astype(BF16)
                dst[j, D_KV:2 * D_KV, :] = pos_ref[j]
            for src, dst in ((vs_ref, vas_ref), (vw_ref, vaw_ref)):
                dst[j, 0:D_KV, :] = src[0, :, cs].astype(BF16)
                dst[j, D_KV:2 * D_KV, :] = ones

    qbd = _make_qbd(q_ref[0], tq)
    qaug = jnp.concatenate([qbd, qs_ref[...]], axis=1)
    qpos2 = q0 + tcol_ref[...]

    g_hi = gate_ref[0].astype(BF16)
    g_r1 = gate_ref[0] - g_hi.astype(F32)
    g_mid = g_r1.astype(BF16)
    g_lo = (g_r1 - g_mid.astype(F32)).astype(BF16)
    g_all = (jnp.dot(jnp.concatenate([g_hi, g_mid], axis=1), gexp_ref[...], preferred_element_type=F32)
             + jnp.dot(g_lo, gexp_ref[0:LANES, :], preferred_element_type=F32))

    def gate_rows(c):
        return jnp.concatenate([g_all[:, (c * N_HEADS + bi) * LANES:(c * N_HEADS + bi + 1) * LANES]
                                for bi in range(N_HEADS)], axis=0)

    ncb = seq // CMP_BLOCK
    pad = jnp.zeros((LANES - ncb, D_KV), F32)
    kc = jnp.concatenate([kc_ref[0], pad], axis=0)
    vc = jnp.concatenate([vc_ref[0], pad], axis=0)
    kcaug = jnp.concatenate([kc.T.astype(BF16), cpos_ref[...]], axis=0)
    s_c = jnp.dot(qaug, kcaug, preferred_element_type=F32).reshape(GROUP, rows2, LANES)
    blk_end = (_iota((1, LANES), 1) + 1) * CMP_BLOCK - 1
    p_c = _softmax_rows(s_c, (qpos2 >= blk_end)[None]).reshape(rows, LANES)
    o_acc = gate_rows(0) * _mm(p_c, vc)

    p4 = _sum_groups(p_c, rows2)
    nbr = blkc_ref.shape[0]
    imp_t = (p4 + pltpu.roll(p4, LANES - 1, 1)).T[0:nbr]
    sel = _select_blocks(imp_t, q0 + trow_ref[...], blkc_ref[...], float(seq // SEL_BLOCK), top_k, 0).T
    sel = sel.astype(BF16)

    def make_step(kaug_ref, vaug_ref, mask_fn, m_ref, acc_ref):
        def step(j, carry):
            s = jnp.dot(qaug, kaug_ref[j], preferred_element_type=F32).reshape(GROUP, rows2, tk)
            dist = qpos2 - (j * tk + _iota((1, tk), 1))
            s = jnp.where(mask_fn(j, dist)[None], s, NEG_INF)
            m = m_ref[...]
            m_new = jnp.maximum(m, jnp.max(s, axis=2, keepdims=True))
            alpha = jnp.exp(m - m_new).reshape(rows, LANES)
            p = jnp.exp(s - jnp.concatenate([m_new] * (tk // LANES), axis=2)).astype(BF16).reshape(rows, tk)
            pv = lax.dot_general(p, vaug_ref[j], (((1,), (1,)), ((), ())), preferred_element_type=F32)
            m_ref[...] = m_new
            acc_ref[...] = jnp.concatenate([alpha, alpha], axis=1) * acc_ref[...] + pv
            return carry

        return step

    def sel_mask(j, dist):
        picked = jnp.dot(sel, e_ref[j], preferred_element_type=F32)
        return (picked > 0.5) & (dist >= 0)

    def win_mask(j, dist):
        return (dist >= 0) & (dist <= WINDOW)

    step_s = make_step(kas_ref, vas_ref, sel_mask, ms_ref, as_ref)
    step_w = make_step(kaw_ref, vaw_ref, win_mask, mw_ref, aw_ref)
    for m_ref, acc_ref in ((ms_ref, as_ref), (mw_ref, aw_ref)):
        m_ref[...] = jnp.full((GROUP, rows2, LANES), NEG_INF, F32)
        acc_ref[...] = jnp.zeros((rows, 2 * D_KV), F32)
    n_s = (q0 + tq - 1) // tk + 1
    j_lo = jnp.maximum(q0 - WINDOW, 0) // tk
    n_w = n_s - j_lo

    def both(i, carry):
        step_s(i, carry)
        return step_w(j_lo + i, carry)

    lax.fori_loop(0, n_w, both, 0)
    lax.fori_loop(n_w, n_s, step_s, 0)
    o_s = as_ref[:, 0:D_KV] / as_ref[:, D_KV:2 * D_KV]
    o_w = aw_ref[:, 0:D_KV] / aw_ref[:, D_KV:2 * D_KV]
    o_acc = o_acc + gate_rows(1) * o_s + gate_rows(2) * o_w
    o_ref[0] = _extract_heads(o_acc, tq)


def _row_consts(tq):
    r = jnp.arange(N_HEADS * tq)
    g, kv, t = r // (N_KV * tq), (r // tq) % N_KV, r % tq
    head = kv * GROUP + g
    slopes = jnp.exp2(-8.0 * (head + 1).astype(F32) / N_HEADS)
    return slopes[:, None], t.astype(jnp.int32)[:, None]


def _gate_expander():
    n = jnp.arange(N_BRANCH * N_HEADS)
    c, bi = n // N_HEADS, n % N_HEADS
    src_lane = c * N_HEADS + (bi % N_KV) * GROUP + bi // N_KV
    r = (jnp.arange(LANES)[:, None] == jnp.repeat(src_lane, LANES)[None, :]).astype(BF16)
    return jnp.concatenate([r, r], axis=0)


def _key_tiles(a, tk):
    return a.reshape(a.shape[0], -1, tk).transpose(1, 0, 2)


def _attn_prompt(q, gate, kcmp, vcmp, ks, vs, kw, vw):
    b, t, _ = q.shape
    tq = min(ATTN_TQ, t)
    tk = min(ATTN_TK, t)
    n_sb = t // SEL_BLOCK
    assert t // POS_SPLIT < 256
    slopes, tcol = _row_consts(tq)
    qs = jnp.zeros((N_HEADS * tq, LANES), F32).at[:, 0].set(POS_SPLIT * slopes[:, 0]).at[:, 1].set(slopes[:, 0])
    kpos = jnp.arange(t)
    pos = jnp.zeros((LANES, t), F32).at[0].set(kpos // POS_SPLIT).at[1].set(kpos % POS_SPLIT)
    lane = jnp.arange(LANES)
    blk = jnp.where((lane % 2 == 0) & (lane < 2 * n_sb), lane // 2, -1).astype(F32)
    e = (blk[:, None] == (kpos // SEL_BLOCK).astype(F32)[None, :])
    blk_end = (lane + 1) * CMP_BLOCK - 1
    cpos = jnp.zeros((LANES, LANES), F32).at[0].set(blk_end // POS_SPLIT).at[1].set(blk_end % POS_SPLIT)
    tcol2 = tcol[:N_KV * tq]
    nbr = min(LANES, -(-2 * n_sb // SUBLANES) * SUBLANES)
    consts = (tcol2, qs.astype(BF16), blk[:nbr, None], tcol2.reshape(1, -1),
              _key_tiles(e[:nbr].astype(BF16), tk), _key_tiles(pos.astype(BF16), tk), cpos.astype(BF16),
              _gate_expander())
    tile = lambda n: pl.BlockSpec((1, tq, n), lambda i, j: (i, j, 0))
    seq = lambda r, n: pl.BlockSpec((1, r, n), lambda i, j: (i, 0, 0))
    return pl.pallas_call(
        functools.partial(_attn_prompt_body, tq=tq, tk=tk, seq=t, top_k=min(TOP_K, n_sb)),
        grid=(b, t // tq),
        in_specs=[tile(D_ATTN), tile(LANES), seq(t // CMP_BLOCK, D_KV), seq(t // CMP_BLOCK, D_KV),
                  seq(D_KV, t), seq(D_KV, t), seq(D_KV, t), seq(D_KV, t)] + [_full(c.shape) for c in consts],
        out_specs=tile(D_ATTN),
        out_shape=jax.ShapeDtypeStruct((b, t, D_ATTN), F32),
        scratch_shapes=[pltpu.VMEM((t // tk, 2 * D_KV, tk), BF16)] * 4
        + [pltpu.VMEM((GROUP, N_KV * tq, LANES), F32), pltpu.VMEM((N_HEADS * tq, 2 * D_KV), F32)] * 2,
        compiler_params=pltpu.CompilerParams(dimension_semantics=("arbitrary", "arbitrary"),
                                             vmem_limit_bytes=VMEM_LIMIT),
        name="attn_prompt",
    )(q, gate, kcmp, vcmp, ks, vs, kw, vw, *consts)


def _attn_sample_body(pt_ref, q_ref, gate_ref, ksn_ref, vsn_ref, kwn_ref, vwn_ref, kwin_ref, vwin_ref,
                      kcp_ref, vcp_ref, cvalid_ref, bias_c_ref, bias_s_ref, bias_w_ref, qrow_ref, blkc_ref, e_ref, *rest,
                      t, n_pages, page, past, w_buf, top_k, nbs):
    ks_pages = rest[:nbs * n_pages]
    vs_pages = rest[nbs * n_pages:2 * nbs * n_pages]
    (o_ref,) = rest[2 * nbs * n_pages:]
    b0 = pl.program_id(0) * nbs
    rows = N_HEADS * t
    rows2 = N_KV * t
    seqs = range(nbs)

    def stack(xs):
        return jnp.concatenate(xs, axis=0)

    def part(x, bb, n):
        return x[bb * n:(bb + 1) * n]

    def pad_rows(x):
        if x.shape[0] == LANES:
            return x
        return jnp.concatenate([x, jnp.zeros((LANES - x.shape[0], D_KV), F32)], axis=0)

    live = NEG_INF / 2
    qbd = [_make_qbd(q_ref[bb], t) for bb in seqs]
    gates = [gate_ref[bb] for bb in seqs]

    def gate_rows(c):
        return stack([_gate_rows(gates[bb], c, t) for bb in seqs])

    kc = [pad_rows(stack([kcp_ref[pt_ref[(b0 + bb) * n_pages + p] // 2] for p in range(n_pages)])) for bb in seqs]
    vc = [pad_rows(stack([vcp_ref[pt_ref[(b0 + bb) * n_pages + p] // 2] for p in range(n_pages)])) for bb in seqs]
    bias_c = bias_c_ref[...]
    s_c = stack([_mm_nt(qbd[bb], kc[bb]) for bb in seqs]) + bias_c
    cvalid = stack([jnp.broadcast_to(cvalid_ref[bb], (rows, LANES)) for bb in seqs])
    p_c = _softmax_rows(s_c, (cvalid > 0.5) & (bias_c > live))
    o_acc = gate_rows(0) * stack([_mm(part(p_c, bb, rows), vc[bb]) for bb in seqs])

    p4 = stack([_sum_groups(part(p_c, bb, rows), rows2) for bb in seqs])
    x1 = p4 + pltpu.roll(p4, LANES - 1, 1)
    imp = x1 + pltpu.roll(x1, LANES - 4, 1)
    n2 = nbs * rows2
    imp_t = jnp.concatenate([imp, jnp.zeros((LANES - n2, LANES), F32)], axis=0).T
    sel = _select_blocks(imp_t, qrow_ref[...], blkc_ref[...], float(past // SEL_BLOCK), top_k, 0).T[0:n2]
    sel4 = stack([part(sel, bb, rows2) for bb in seqs for _ in range(GROUP)])

    s_s = stack([jnp.concatenate([_mm(qbd[bb], ks_pages[bb * n_pages + p][...]) for p in range(n_pages)]
                                 + [_mm_nt(qbd[bb], pad_rows(ksn_ref[bb]))], axis=1) for bb in seqs])
    bias_s = bias_s_ref[...]
    picked = _mm(sel4, e_ref[...])
    p_s = _softmax_rows(s_s + bias_s, (picked > 0.5) & (bias_s > live))
    o_s = []
    for bb in seqs:
        pb = part(p_s, bb, rows)
        acc = _mm(pb[:, past:], pad_rows(vsn_ref[bb]))
        for p in range(n_pages):
            acc = acc + _mm_nt(pb[:, p * page:(p + 1) * page], vs_pages[bb * n_pages + p][...])
        o_s.append(acc)
    o_acc = o_acc + gate_rows(1) * stack(o_s)

    kwin = [kwin_ref[bb] for bb in seqs]
    vwin = [vwin_ref[bb] for bb in seqs]
    s_w = stack([jnp.concatenate([_mm(qbd[bb], kwin[bb]), _mm_nt(qbd[bb], pad_rows(kwn_ref[bb]))], axis=1)
                 for bb in seqs])
    bias_w = bias_w_ref[...]
    p_w = _softmax_rows(s_w + bias_w, bias_w > live)
    o_w = stack([_mm_nt(part(p_w, bb, rows)[:, :w_buf], vwin[bb])
                 + _mm(part(p_w, bb, rows)[:, w_buf:], pad_rows(vwn_ref[bb])) for bb in seqs])
    o_acc = o_acc + gate_rows(2) * o_w
    for bb in seqs:
        o_ref[bb] = _extract_heads(part(o_acc, bb, rows), t)


def _window_update_body(kwin_ref, vwin_ref, kn_ref, vn_ref, kwo_ref, vwo_ref, *, t, w_buf, nbs):
    is_new = _iota((D_KV, LANES), 1) >= LANES - t

    def shift_in(win, new):
        new_t = jnp.concatenate([jnp.zeros((LANES - t, D_KV), F32), new], axis=0).T
        rolled = pltpu.roll(win, w_buf - t, 1)
        last = jnp.where(is_new, new_t, rolled[:, w_buf - LANES:])
        return jnp.concatenate([rolled[:, :w_buf - LANES], last], axis=1)

    for bb in range(nbs):
        kwo_ref[bb] = shift_in(kwin_ref[bb], kn_ref[bb])
        vwo_ref[bb] = shift_in(vwin_ref[bb], vn_ref[bb])


def _window_update(kwin, vwin, kw_new, vw_new):
    nl, nb, _, w_buf = kwin.shape
    t = kw_new.shape[2]
    assert w_buf % LANES == 0 and t <= LANES
    nbs = next(c for c in range(min(WIN_STEP, nb), 0, -1) if nb % c == 0)
    win = pl.BlockSpec((None, nbs, D_KV, w_buf), lambda l, i: (l, i, 0, 0))
    new = pl.BlockSpec((None, nbs, t, D_KV), lambda l, i: (l, i, 0, 0))
    return pl.pallas_call(
        functools.partial(_window_update_body, t=t, w_buf=w_buf, nbs=nbs),
        grid=(nl, nb // nbs),
        in_specs=[win, win, new, new],
        out_specs=[win, win],
        out_shape=[jax.ShapeDtypeStruct(kwin.shape, F32)] * 2,
        compiler_params=pltpu.CompilerParams(dimension_semantics=("arbitrary", "arbitrary"),
                                             vmem_limit_bytes=VMEM_LIMIT),
        name="window_update",
    )(kwin, vwin, kw_new, vw_new)


def _page_map(l, row_stride, offset, i, pt):
    return (l, pt[i * row_stride + offset], 0, 0)


def _attn_sample(l, page_table, q, gate, ks_new, vs_new, kw_new, vw_new, kwin, vwin, kcp, vcp,
                 cache_ks, cache_vs):
    nb, t, _ = q.shape
    n_pages = page_table.shape[1]
    page = cache_ks.shape[3]
    past = n_pages * page
    w_buf = kwin.shape[3]
    n_sb = -(-(past + t) // SEL_BLOCK)
    assert t == SUBLANES and n_pages * 8 <= LANES and page == LANES and past % SEL_BLOCK == 0
    assert w_buf % LANES == 0
    slopes, tcol = _row_consts(t)
    lane = jnp.arange(LANES)
    pg, r = lane // 8, lane % 8
    in_range = pg < n_pages
    cend = jnp.where(in_range, (4 * pg + r % 4 + 1) * CMP_BLOCK - 1, 1 << 30).astype(jnp.int32)[None, :]
    par = page_table % 2
    par_l = jnp.take(par, jnp.minimum(pg, n_pages - 1), axis=1)
    cvalid = ((r[None, :] // 4 == par_l) & in_range[None, :]).astype(F32)[:, None, :]
    blk = jnp.where(in_range & (r % 2 == 0) & (r < 4), 2 * pg + r // 2, -1)
    blk = jnp.where(lane == 1, n_sb - 1, blk)
    blkf = blk.astype(F32)[None, :]
    e = (blkf.reshape(-1, 1) == (jnp.arange(past + LANES) // SEL_BLOCK).astype(F32)[None, :]).astype(BF16)
    nbs = next(c for c in range(min(DEC_STEP, nb), 0, -1) if nb % c == 0)
    qpos = past + tcol
    assert nbs * N_KV * t <= LANES
    qrow = jnp.pad(jnp.tile(qpos[:N_KV * t, 0], nbs), (0, LANES - nbs * N_KV * t), constant_values=past)[None, :]
    blkc = blkf.reshape(-1, 1)

    def bias(dist, ok):
        return jnp.tile(jnp.where(ok, -(slopes * dist.astype(F32)), NEG_INF), (nbs, 1))

    dist_c = qpos - cend
    dist_s = qpos - jnp.arange(past + LANES)[None, :]
    dist_w = qpos - (past - w_buf + jnp.arange(w_buf + LANES))[None, :]
    bias_c = bias(dist_c, dist_c >= 0)
    bias_s = bias(dist_s, dist_s >= 0)
    bias_w = bias(dist_w, (dist_w >= 0) & (dist_w <= WINDOW))
    tok = lambda n: pl.BlockSpec((nbs, t, n), lambda i, pt: (i, 0, 0))
    win = pl.BlockSpec((None, nbs, D_KV, w_buf), lambda i, pt: (l, i, 0, 0))
    res = pl.BlockSpec((None,) + kcp.shape[1:], lambda i, pt: (l, 0, 0, 0))
    const = lambda a: pl.BlockSpec(a.shape, lambda i, pt: (0,) * a.ndim)
    pages = [pl.BlockSpec((None, None, D_KV, page), functools.partial(_page_map, l, nbs * n_pages, bb * n_pages + p))
             for bb in range(nbs) for p in range(n_pages)]
    grid_spec = pltpu.PrefetchScalarGridSpec(
        num_scalar_prefetch=1,
        grid=(nb // nbs,),
        in_specs=[tok(D_ATTN), tok(LANES), tok(D_KV), tok(D_KV), tok(D_KV), tok(D_KV), win, win, res, res,
                  pl.BlockSpec((nbs, 1, LANES), lambda i, pt: (i, 0, 0)),
                  const(bias_c), const(bias_s), const(bias_w), const(qrow), const(blkc), const(e)] + pages + pages,
        out_specs=[tok(D_ATTN)],
    )
    return pl.pallas_call(
        functools.partial(_attn_sample_body, t=t, n_pages=n_pages, page=page, past=past, w_buf=w_buf,
                          top_k=min(TOP_K, n_sb), nbs=nbs),
        grid_spec=grid_spec,
        out_shape=[jax.ShapeDtypeStruct((nb, t, D_ATTN), F32)],
        compiler_params=pltpu.CompilerParams(dimension_semantics=("arbitrary",), vmem_limit_bytes=VMEM_LIMIT),
        name="attn_sample",
    )(page_table.reshape(-1), q, gate, ks_new, vs_new, kw_new, vw_new, kwin, vwin, kcp, vcp, cvalid,
      bias_c, bias_s, bias_w, qrow, blkc, e, *([cache_ks] * (nbs * n_pages)), *([cache_vs] * (nbs * n_pages)))


def _back_head(x, o, pab, wg2_ref, woa_ref, wmix_ref, ln1g_ref, ln1b_ref):
    y_c = _mm(o, woa_ref[...])
    g2 = jax.nn.sigmoid(_mm(x, wg2_ref[...]))
    mixed = pab + g2 * y_c
    return _layer_norm(ALPHA * x + _mm(mixed, wmix_ref[...]), ln1g_ref[...], ln1b_ref[...])


def _back_prompt_body(x_ref, o_ref, pab_ref, wg2_ref, woa_ref, wmix_ref, ln1g_ref, ln1b_ref,
                      wup_ref, fcw_ref, fcb_ref, wdn_ref, ln2g_ref, ln2b_ref,
                      y_ref, fst_ref, ext_ref, *, tt):
    j = pl.program_id(1)

    @pl.when(j == 0)
    def _():
        ext_ref[0:8, :] = jnp.zeros((8, D_FF), F32)

    h = _back_head(x_ref[0], o_ref[0], pab_ref[0], wg2_ref, woa_ref, wmix_ref, ln1g_ref, ln1b_ref)
    hb = h.astype(BF16)
    cf = D_FF // FF_CHUNKS
    f = jnp.zeros((tt, D_MODEL), F32)
    for c in range(FF_CHUNKS):
        cs = slice(c * cf, (c + 1) * cf)
        u = _mm(hb, wup_ref[:, c * cf:(c + 1) * cf])
        gt = _mm(hb, wup_ref[:, D_FF + c * cf:D_FF + (c + 1) * cf])
        ext_ref[8:8 + tt, cs] = u
        conv = fcb_ref[:, cs] + fcw_ref[0:1, cs] * ext_ref[6:6 + tt, cs]
        conv = conv + fcw_ref[1:2, cs] * ext_ref[7:7 + tt, cs]
        conv = conv + fcw_ref[2:3, cs] * u
        f = f + _mm(jax.nn.gelu(conv) * gt, wdn_ref[c * cf:(c + 1) * cf, :])
    y_ref[0] = _layer_norm(ALPHA * h + f, ln2g_ref[...], ln2b_ref[...])
    last = ext_ref[tt:tt + 8, :]
    fst_ref[0] = last
    ext_ref[0:8, :] = last


def _back_sample_body(x_ref, o_ref, pab_ref, hf_ref, wg2_ref, woa_ref, wmix_ref, ln1g_ref, ln1b_ref,
                      wup_ref, fcw_ref, fcb_ref, wdn_ref, ln2g_ref, ln2b_ref,
                      y_ref, fst_ref, ext_ref, *, ns, t):
    h = _back_head(x_ref[...], o_ref[...], pab_ref[...], wg2_ref, woa_ref, wmix_ref, ln1g_ref, ln1b_ref)
    hb = h.astype(BF16)
    cf = D_FF // FF_CHUNKS
    ext_ref[:, 6:8, :] = hf_ref[...]
    f = jnp.zeros((ns * t, D_MODEL), F32)
    for c in range(FF_CHUNKS):
        cs = slice(c * cf, (c + 1) * cf)
        u = _mm(hb, wup_ref[:, c * cf:(c + 1) * cf])
        gt = _mm(hb, wup_ref[:, D_FF + c * cf:D_FF + (c + 1) * cf])
        ext_ref[:, 8:8 + t, cs] = u.reshape(ns, t, cf)
        fcw = fcw_ref[:, cs]
        conv = fcb_ref[:, cs] + fcw[0:1] * ext_ref[:, 6:6 + t, cs]
        conv = conv + fcw[1:2] * ext_ref[:, 7:7 + t, cs]
        conv = conv + fcw[2:3] * ext_ref[:, 8:8 + t, cs]
        f = f + _mm(jax.nn.gelu(conv).reshape(ns * t, cf) * gt, wdn_ref[c * cf:(c + 1) * cf, :])
    y_ref[...] = _layer_norm(ALPHA * h + f, ln2g_ref[...], ln2b_ref[...])
    fst_ref[...] = ext_ref[:, t:t + 8, :]


def _back_prompt(x, o, pab, wts):
    b, t, _ = x.shape
    tt = min(FRONT_TILE, t)
    tok = lambda n: pl.BlockSpec((1, tt, n), lambda i, j: (i, j, 0))
    return pl.pallas_call(
        functools.partial(_back_prompt_body, tt=tt),
        grid=(b, t // tt),
        in_specs=[tok(D_MODEL), tok(D_ATTN), tok(D_MODEL)] + [_full(w.shape) for w in wts],
        out_specs=[tok(D_MODEL), pl.BlockSpec((1, 8, D_FF), lambda i, j: (i, 0, 0))],
        out_shape=[jax.ShapeDtypeStruct((b, t, D_MODEL), F32), jax.ShapeDtypeStruct((b, 8, D_FF), F32)],
        scratch_shapes=[pltpu.VMEM((8 + tt, D_FF), F32)],
        compiler_params=pltpu.CompilerParams(dimension_semantics=("arbitrary", "arbitrary"),
                                             vmem_limit_bytes=VMEM_LIMIT),
        name="back_prompt",
    )(x, o, pab, *wts)


def _back_sample(x, o, pab, hist_f, wts, t):
    n = x.shape[0]
    nb = n // t
    ns = min(FRONT_TILE // t, nb)
    rows = ns * t
    tok = lambda c: pl.BlockSpec((rows, c), lambda i: (i, 0))
    return pl.pallas_call(
        functools.partial(_back_sample_body, ns=ns, t=t),
        grid=(nb // ns,),
        in_specs=[tok(D_MODEL), tok(D_ATTN), tok(D_MODEL),
                  pl.BlockSpec((ns, CONV_W - 1, D_FF), lambda i: (i, 0, 0))] + [_full(w.shape) for w in wts],
        out_specs=[tok(D_MODEL), pl.BlockSpec((ns, 8, D_FF), lambda i: (i, 0, 0))],
        out_shape=[jax.ShapeDtypeStruct((n, D_MODEL), F32), jax.ShapeDtypeStruct((nb, 8, D_FF), F32)],
        scratch_shapes=[pltpu.VMEM((ns, 8 + t, D_FF), F32)],
        compiler_params=pltpu.CompilerParams(dimension_semantics=("arbitrary",), vmem_limit_bytes=VMEM_LIMIT),
        name="back_sample",
    )(x, o, pab, hist_f, *wts)


def _block_diag2(w):
    z = jnp.zeros_like(w)
    return jnp.concatenate([jnp.concatenate([w, z], axis=-1), jnp.concatenate([z, w], axis=-1)], axis=-2)


def kernel(x_prompt, x_sample, cache_k_cmp, cache_v_cmp, cache_k_sel, cache_v_sel, state_k_win, state_v_win, state_conv, state_pool, state_ffn_conv, page_table, w_in, conv_w, conv_b, w_out_conv, pool_w, pool_scale, w_cmp_k, w_cmp_v, w_out_attn, w_mix_out, ln1_g, ln1_b, w_ffn_up, ffn_conv_w, ffn_conv_b, w_ffn_down, ln2_g, ln2_b):
    depth = w_in.shape[0]
    bp, seq, _ = x_prompt.shape
    nb, t, _ = x_sample.shape
    n_phys, page = cache_k_cmp.shape[1], cache_k_cmp.shape[2]
    past = page_table.shape[1] * page
    w_buf = state_k_win.shape[2]

    w_front = jnp.concatenate(
        [w_in[:, :, :IN_GATES], jnp.zeros((depth, D_MODEL, C_G01 - IN_GATES), F32),
         w_in[:, :, IN_GATES:IN_GATES + 2 * D_MODEL]], axis=2).astype(BF16)
    w_g2 = w_in[:, :, IN_GATES + 2 * D_MODEL:].astype(BF16)
    ng = pool_w.shape[1]
    pool_bd = jnp.zeros((depth, D_POOL, D_MODEL), F32)
    for g in range(ng):
        pool_bd = pool_bd.at[:, g * POOL_GROUP:(g + 1) * POOL_GROUP,
                             g * (D_MODEL // ng):(g + 1) * (D_MODEL // ng)].set(pool_w[:, g])
    pool_bd = pool_bd.astype(BF16)
    wck = _block_diag2(w_cmp_k).astype(BF16)
    wcv = _block_diag2(w_cmp_v).astype(BF16)
    w_oc, w_oa, w_mix = w_out_conv.astype(BF16), w_out_attn.astype(BF16), w_mix_out.astype(BF16)
    w_up, w_dn = w_ffn_up.astype(BF16), w_ffn_down.astype(BF16)
    row = lambda a: a[:, None, :]

    keys_minor = lambda a: jnp.transpose(a, (0, 1, 3, 4, 2)).reshape(a.shape[0], a.shape[1], D_KV, a.shape[2])
    tokens_major = lambda a: jnp.transpose(a.reshape(a.shape[0], N_KV, HEAD_DIM, a.shape[2]), (0, 3, 1, 2))
    ck_s, cv_s = keys_minor(cache_k_sel), keys_minor(cache_v_sel)
    kwin_all, vwin_all = keys_minor(state_k_win), keys_minor(state_v_win)
    kcp, vcp = _compress(keys_minor(cache_k_cmp), keys_minor(cache_v_cmp), wck, wcv)
    kcp = kcp.reshape(depth, n_phys // 2, 8, D_KV)
    vcp = vcp.reshape(depth, n_phys // 2, 8, D_KV)

    y_p = x_prompt
    y_s = x_sample.reshape(nb * t, D_MODEL)
    st_p, st_s = [], []
    for l in range(depth):
        fw = (w_front[l], conv_w[l], row(conv_b)[l], w_oc[l], pool_bd[l], row(pool_scale)[l])
        bw = (w_g2[l], w_oa[l], w_mix[l], row(ln1_g)[l], row(ln1_b)[l], w_up[l], ffn_conv_w[l],
              row(ffn_conv_b)[l], w_dn[l], row(ln2_g)[l], row(ln2_b)[l])

        pab, q, kc, vc, ks, vs, kw, vw, gate, cst, pst = _front_prompt(y_p, fw)
        kcmp, vcmp = _compress(kc[None], vc[None], wck[l:l + 1], wcv[l:l + 1])
        o = _attn_prompt(q, gate, kcmp.reshape(bp, seq // CMP_BLOCK, D_KV),
                         vcmp.reshape(bp, seq // CMP_BLOCK, D_KV), ks, vs, kw, vw)
        y_p, fst = _back_prompt(y_p, o, pab, bw)
        st_p.append(tuple(tokens_major(a) for a in (kc, vc, ks, vs, kw[:, :, seq - w_buf:], vw[:, :, seq - w_buf:]))
                    + (cst[:, 8 - (CONV_W - 1):], pst[:, 16 - POOL_HIST:], fst[:, 8 - (CONV_W - 1):]))

        pab, q, kc, vc, ks, vs, kw, vw, gate, cst, pst = _front_sample(
            y_s.reshape(nb, t, D_MODEL), state_conv[l], state_pool[l], fw, past)
        r3 = lambda a: a.reshape(nb, t, a.shape[-1])
        (o,) = _attn_sample(l, page_table, r3(q), r3(gate), r3(ks), r3(vs), r3(kw), r3(vw),
                            kwin_all, vwin_all, kcp, vcp, ck_s, cv_s)
        y_s, fst = _back_sample(y_s, o.reshape(nb * t, D_ATTN), pab, state_ffn_conv[l], bw, t)
        st_s.append((kc, vc, ks, vs, kw, vw,
                     cst[:, 8 - (CONV_W - 1):], pst[:, 16 - POOL_HIST:], fst[:, 8 - (CONV_W - 1):]))

    kc_p, vc_p, ks_p, vs_p, kw_p, vw_p, conv_p, pool_p, ffn_p = [jnp.stack(a) for a in zip(*st_p)]
    kc_s, vc_s, ks_s, vs_s, kwn_s, vwn_s, conv_s, pool_s, ffn_s = [jnp.stack(a) for a in zip(*st_s)]
    kc_s, vc_s, ks_s, vs_s = [a.reshape(depth, nb, t, N_KV, HEAD_DIM) for a in (kc_s, vc_s, ks_s, vs_s)]
    kw_s, vw_s = _window_update(kwin_all, vwin_all, kwn_s.reshape(depth, nb, t, D_KV),
                                vwn_s.reshape(depth, nb, t, D_KV))
    kw_s, vw_s = [jnp.transpose(a.reshape(depth, nb, N_KV, HEAD_DIM, w_buf), (0, 1, 4, 2, 3)) for a in (kw_s, vw_s)]
    return (y_p, y_s.reshape(nb, t, D_MODEL), kc_p, kc_s, vc_p, vc_s, ks_p, ks_s, vs_p, vs_s,
            kw_p, kw_s, vw_p, vw_s, conv_p, conv_s, pool_p, pool_s, ffn_p, ffn_s)
```

```python
import functools

import jax
import jax.numpy as jnp
from jax import lax
from jax.experimental import pallas as pl
from jax.experimental.pallas import tpu as pltpu

F32 = jnp.float32
BF16 = jnp.bfloat16

D_MODEL = 1024
DEPTH = 4
D_CONV = 256
CONV_W = 3
D_POOL = 256
POOL_WINDOWS = (2, 4, 8, 16)
POOL_GROUP = D_POOL // len(POOL_WINDOWS)
POOL_HIST = 15
N_HEADS = 8
N_KV = 2
GROUP = N_HEADS // N_KV
HEAD_DIM = 64
D_ATTN = N_HEADS * HEAD_DIM
D_KV = N_KV * HEAD_DIM
CMP_BLOCK = 32
SEL_BLOCK = 64
TOP_K = 8
WINDOW = 512
N_BRANCH = 3
D_FF = 2816
ALPHA = (2.0 * DEPTH) ** 0.25
LN_EPS = 1e-5
ATTN_SCALE = HEAD_DIM ** -0.5
NEG_INF = -1e30
POS_SPLIT = 64

LANES = 128
SUBLANES = 8
VMEM_LIMIT = 56 * 1024 * 1024

C_AB, C_AC, C_AH, C_U, C_Q = 0, D_CONV, 2 * D_CONV, 3 * D_CONV, 3 * D_CONV + D_POOL
C_KV = C_Q + D_ATTN
C_NSA = C_KV + 6 * D_KV
C_G01 = C_NSA + LANES
IN_MAIN = C_NSA
IN_GATES = IN_MAIN + N_BRANCH * N_HEADS

FRONT_TILE = 256
ATTN_TQ = 128
ATTN_TK = 256
CMP_ROWS = 8192
CMP_PITCH = LANES + SUBLANES
DEC_STEP = 4
WIN_STEP = 8
FF_CHUNKS = 1


def _mm(a, b):
    return jnp.dot(a.astype(BF16), b.astype(BF16), preferred_element_type=F32)


def _mm_nt(a, b):
    return lax.dot_general(a.astype(BF16), b.astype(BF16), (((1,), (1,)), ((), ())),
                           preferred_element_type=F32)


def _layer_norm(x, g, b):
    mu = jnp.mean(x, axis=-1, keepdims=True)
    var = jnp.mean(jnp.square(x - mu), axis=-1, keepdims=True)
    return (x - mu) * lax.rsqrt(var + LN_EPS) * g + b


def _iota(shape, axis):
    return lax.broadcasted_iota(jnp.int32, shape, axis)


def _pool_delta(u, shifted, pos):
    nd = u.ndim
    acc = u
    sums = {}
    for j in range(1, POOL_WINDOWS[-1]):
        acc = acc + shifted(j)
        if j + 1 in POOL_WINDOWS:
            sums[j + 1] = acc
    grp = _iota((1,) * (nd - 1) + (D_POOL,), nd - 1) // POOL_GROUP
    s = sums[POOL_WINDOWS[-1]]
    win = jnp.full(grp.shape, POOL_WINDOWS[-1], jnp.int32)
    for g in range(len(POOL_WINDOWS) - 2, -1, -1):
        s = jnp.where(grp == g, sums[POOL_WINDOWS[g]], s)
        win = jnp.where(grp == g, POOL_WINDOWS[g], win)
    cnt = jnp.minimum(win, pos + 1).astype(F32)
    return s / cnt - u


def _front_tail(z, conv, d, woc_ref, pbd_ref, psc_ref, outs, prompt):
    pab_ref, q_ref, kv_refs, gate_ref = outs
    y_a = _mm(z[:, C_AB:C_AB + D_CONV] * conv, woc_ref[...])
    y_b = _mm(d, pbd_ref[...]) * psc_ref[...]
    g0 = jax.nn.sigmoid(z[:, C_G01:C_G01 + D_MODEL])
    g1 = jax.nn.sigmoid(z[:, C_G01 + D_MODEL:C_G01 + 2 * D_MODEL])
    pab_ref[...] = (g0 * y_a + g1 * y_b).reshape(pab_ref.shape)
    q_ref[...] = z[:, C_Q:C_Q + D_ATTN].reshape(q_ref.shape)
    for i, r in enumerate(kv_refs):
        kv = z[:, C_KV + i * D_KV:C_KV + (i + 1) * D_KV]
        r[...] = (kv.T if prompt else kv).reshape(r.shape)
    gate_ref[...] = jax.nn.sigmoid(z[:, C_NSA:C_NSA + LANES]).reshape(gate_ref.shape)


def _front_prompt_body(x_ref, w_ref, cw_ref, cb_ref, woc_ref, pbd_ref, psc_ref,
                       pab_ref, q_ref, kc_ref, vc_ref, ks_ref, vs_ref, kw_ref, vw_ref, gate_ref,
                       cst_ref, pst_ref, extc_ref, extu_ref, *, tt):
    j = pl.program_id(1)

    @pl.when(j == 0)
    def _():
        extc_ref[0:8, :] = jnp.zeros((8, D_CONV), F32)
        extu_ref[0:16, :] = jnp.zeros((16, D_POOL), F32)

    z = _mm(x_ref[0], w_ref[...])
    ch = z[:, C_AC:C_AC + D_CONV] * z[:, C_AH:C_AH + D_CONV]
    extc_ref[8:8 + tt, :] = ch
    conv = cb_ref[...] + cw_ref[0:1, :] * extc_ref[6:6 + tt, :]
    conv = conv + cw_ref[1:2, :] * extc_ref[7:7 + tt, :]
    conv = conv + cw_ref[2:3, :] * ch
    u = z[:, C_U:C_U + D_POOL]
    extu_ref[16:16 + tt, :] = u
    pos = j * tt + _iota((tt, 1), 0)
    d = _pool_delta(u, lambda s: extu_ref[16 - s:16 - s + tt, :], pos)
    _front_tail(z, conv, d, woc_ref, pbd_ref, psc_ref,
                (pab_ref, q_ref, (kc_ref, vc_ref, ks_ref, vs_ref, kw_ref, vw_ref), gate_ref), True)
    last_c = extc_ref[tt:tt + 8, :]
    last_u = extu_ref[tt:tt + 16, :]
    cst_ref[0] = last_c
    pst_ref[0] = last_u
    extc_ref[0:8, :] = last_c
    extu_ref[0:16, :] = last_u


def _front_sample_body(x_ref, hc_ref, hu_ref, w_ref, cw_ref, cb_ref, woc_ref, pbd_ref, psc_ref,
                       pab_ref, q_ref, kc_ref, vc_ref, ks_ref, vs_ref, kw_ref, vw_ref, gate_ref,
                       cst_ref, pst_ref, extc_ref, extu_ref, *, ns, t, pos0):
    z = _mm(x_ref[...], w_ref[...])
    ch = z[:, C_AC:C_AC + D_CONV] * z[:, C_AH:C_AH + D_CONV]
    extc_ref[:, 6:8, :] = hc_ref[...]
    extc_ref[:, 8:8 + t, :] = ch.reshape(ns, t, D_CONV)
    cw = cw_ref[...]
    conv = cb_ref[...] + cw[0:1] * extc_ref[:, 6:6 + t, :]
    conv = conv + cw[1:2] * extc_ref[:, 7:7 + t, :]
    conv = conv + cw[2:3] * extc_ref[:, 8:8 + t, :]
    u = z[:, C_U:C_U + D_POOL]
    extu_ref[:, 1:16, :] = hu_ref[...]
    extu_ref[:, 16:16 + t, :] = u.reshape(ns, t, D_POOL)
    pos = pos0 + _iota((1, t, 1), 1)
    d = _pool_delta(extu_ref[:, 16:16 + t, :], lambda s: extu_ref[:, 16 - s:16 - s + t, :], pos)
    _front_tail(z, conv.reshape(ns * t, D_CONV), d.reshape(ns * t, D_POOL), woc_ref, pbd_ref, psc_ref,
                (pab_ref, q_ref, (kc_ref, vc_ref, ks_ref, vs_ref, kw_ref, vw_ref), gate_ref), False)
    cst_ref[...] = extc_ref[:, t:t + 8, :]
    pst_ref[...] = extu_ref[:, t:t + 16, :]


def _full(shape):
    n = len(shape)
    return pl.BlockSpec(shape, lambda *_: (0,) * n, pipeline_mode=pl.Buffered(1))


def _front_prompt(x, wts):
    b, t, _ = x.shape
    tt = min(FRONT_TILE, t)
    tok = lambda n: pl.BlockSpec((1, tt, n), lambda i, j: (i, j, 0))
    st = lambda r, n: pl.BlockSpec((1, r, n), lambda i, j: (i, 0, 0))
    kvt = pl.BlockSpec((1, D_KV, tt), lambda i, j: (i, 0, j))
    out_shape = ([jax.ShapeDtypeStruct((b, t, D_MODEL), F32), jax.ShapeDtypeStruct((b, t, D_ATTN), F32)]
                 + [jax.ShapeDtypeStruct((b, D_KV, t), F32)] * 6
                 + [jax.ShapeDtypeStruct((b, t, LANES), F32),
                    jax.ShapeDtypeStruct((b, 8, D_CONV), F32), jax.ShapeDtypeStruct((b, 16, D_POOL), F32)])
    out_specs = ([tok(D_MODEL), tok(D_ATTN)] + [kvt] * 6 + [tok(LANES), st(8, D_CONV), st(16, D_POOL)])
    return pl.pallas_call(
        functools.partial(_front_prompt_body, tt=tt),
        grid=(b, t // tt),
        in_specs=[tok(D_MODEL)] + [_full(w.shape) for w in wts],
        out_specs=out_specs,
        out_shape=out_shape,
        scratch_shapes=[pltpu.VMEM((8 + tt, D_CONV), F32), pltpu.VMEM((16 + tt, D_POOL), F32)],
        compiler_params=pltpu.CompilerParams(dimension_semantics=("arbitrary", "arbitrary"),
                                             vmem_limit_bytes=VMEM_LIMIT),
        name="front_prompt",
    )(x, *wts)


def _front_sample(x, hist_c, hist_u, wts, pos0):
    nb, t, _ = x.shape
    ns = min(FRONT_TILE // t, nb)
    rows = ns * t
    xf = x.reshape(nb * t, D_MODEL)
    tok = lambda n: pl.BlockSpec((rows, n), lambda i: (i, 0))
    st = lambda r, n: pl.BlockSpec((ns, r, n), lambda i: (i, 0, 0))
    out_shape = ([jax.ShapeDtypeStruct((nb * t, D_MODEL), F32), jax.ShapeDtypeStruct((nb * t, D_ATTN), F32)]
                 + [jax.ShapeDtypeStruct((nb * t, D_KV), F32)] * 6
                 + [jax.ShapeDtypeStruct((nb * t, LANES), F32),
                    jax.ShapeDtypeStruct((nb, 8, D_CONV), F32), jax.ShapeDtypeStruct((nb, 16, D_POOL), F32)])
    out_specs = ([tok(D_MODEL), tok(D_ATTN)] + [tok(D_KV)] * 6 + [tok(LANES), st(8, D_CONV), st(16, D_POOL)])
    return pl.pallas_call(
        functools.partial(_front_sample_body, ns=ns, t=t, pos0=pos0),
        grid=(nb // ns,),
        in_specs=[tok(D_MODEL), st(CONV_W - 1, D_CONV), st(POOL_HIST, D_POOL)] + [_full(w.shape) for w in wts],
        out_specs=out_specs,
        out_shape=out_shape,
        scratch_shapes=[pltpu.VMEM((ns, 8 + t, D_CONV), F32), pltpu.VMEM((ns, 16 + t, D_POOL), F32)],
        compiler_params=pltpu.CompilerParams(dimension_semantics=("arbitrary",),
                                             vmem_limit_bytes=VMEM_LIMIT),
        name="front_sample",
    )(xf, hist_c, hist_u, *wts)


def _compress_body(k_ref, v_ref, wk_ref, wv_ref, ko_ref, vo_ref, kr_ref, vr_ref, *, npg, r):
    cols = r // LANES
    nvp = npg * cols
    for i in range(npg):
        for c in range(cols):
            row0 = (i * cols + c) * CMP_PITCH
            kr_ref[row0:row0 + LANES, :] = k_ref[i, :, c * LANES:(c + 1) * LANES].astype(BF16).T.astype(F32)
            vr_ref[row0:row0 + LANES, :] = v_ref[i, :, c * LANES:(c + 1) * LANES].astype(BF16).T.astype(F32)
    nj = LANES // CMP_BLOCK
    acc_k = jnp.zeros((nj * nvp, D_KV), F32)
    acc_v = jnp.zeros((nj * nvp, D_KV), F32)
    def rows(ref, l):
        return jnp.concatenate([ref[pl.ds(j * CMP_BLOCK + l, nvp, stride=CMP_PITCH), :] for j in range(nj)], axis=0)

    for l in range(0, CMP_BLOCK, 2):
        acc_k = acc_k + _mm(jnp.concatenate([rows(kr_ref, l), rows(kr_ref, l + 1)], axis=1),
                            wk_ref[l:l + 2].reshape(2 * D_KV, D_KV))
        acc_v = acc_v + _mm(jnp.concatenate([rows(vr_ref, l), rows(vr_ref, l + 1)], axis=1),
                            wv_ref[l:l + 2].reshape(2 * D_KV, D_KV))
    for j in range(nj):
        ko_ref[pl.ds(j, nvp, stride=nj), :] = acc_k[j * nvp:(j + 1) * nvp]
        vo_ref[pl.ds(j, nvp, stride=nj), :] = acc_v[j * nvp:(j + 1) * nvp]


def _compress(k, v, wk, wv):
    nl, pages, _, r = k.shape
    npg = next(c for c in range(min(max(CMP_ROWS // r, 1), pages), 0, -1) if pages % c == 0)
    nblk = npg * r // CMP_BLOCK
    rspec = pl.BlockSpec((None, npg, D_KV, r), lambda l, i: (l, i, 0, 0))
    wspec = pl.BlockSpec((None, CMP_BLOCK, D_KV, D_KV), lambda l, i: (l, 0, 0, 0))
    ospec = pl.BlockSpec((None, nblk, D_KV), lambda l, i: (l, i, 0))
    return pl.pallas_call(
        functools.partial(_compress_body, npg=npg, r=r),
        grid=(nl, pages // npg),
        in_specs=[rspec, rspec, wspec, wspec],
        out_specs=[ospec, ospec],
        out_shape=[jax.ShapeDtypeStruct((nl, pages * r // CMP_BLOCK, D_KV), F32)] * 2,
        scratch_shapes=[pltpu.VMEM((npg * (r // LANES) * CMP_PITCH, D_KV), F32)] * 2,
        compiler_params=pltpu.CompilerParams(dimension_semantics=("arbitrary", "arbitrary"),
                                             vmem_limit_bytes=VMEM_LIMIT),
        name="compress",
    )(k, v, wk, wv)


def _make_qbd(q, tq):
    lo = _iota((tq, LANES), 1) < HEAD_DIM
    blocks = {}
    for jv in range(D_ATTN // LANES):
        a = q[:, LANES * jv:LANES * (jv + 1)] * ATTN_SCALE
        r = pltpu.roll(a, HEAD_DIM, 1)
        for half in range(2):
            h = 2 * jv + half
            kv, g = h // GROUP, h % GROUP
            if kv == 0:
                blocks[(g, kv)] = jnp.where(lo, a if half == 0 else r, 0.0)
            else:
                blocks[(g, kv)] = jnp.where(lo, 0.0, a if half == 1 else r)
    return jnp.concatenate([blocks[(g, kv)] for g in range(GROUP) for kv in range(N_KV)], axis=0).astype(BF16)


def _extract_heads(o, tq):
    lo = _iota((tq, LANES), 1) < HEAD_DIM
    cols = []
    for jv in range(D_ATTN // LANES):
        parts = []
        for half in range(2):
            h = 2 * jv + half
            kv, g = h // GROUP, h % GROUP
            blk = o[(g * N_KV + kv) * tq:(g * N_KV + kv + 1) * tq]
            if (half == 1) != (kv == 1):
                blk = pltpu.roll(blk, HEAD_DIM, 1)
            parts.append(blk)
        cols.append(jnp.where(lo, parts[0], parts[1]))
    return jnp.concatenate(cols, axis=1)


def _gate_rows(gate, c, tq):
    return jnp.concatenate([gate[:, c * N_HEADS + kv * GROUP + g:c * N_HEADS + kv * GROUP + g + 1]
                            for g in range(GROUP) for kv in range(N_KV)], axis=0)


def _softmax_rows(s, valid):
    s = jnp.where(valid, s, NEG_INF)
    m = jnp.max(s, axis=-1, keepdims=True)
    e = jnp.exp(s - m)
    p = e / jnp.sum(e, axis=-1, keepdims=True)
    return jnp.where(valid, p, 0.0)


def _select_blocks(imp, qpos2, blkf, n_valid_imp, k, axis):
    cur = (qpos2 // SEL_BLOCK).astype(F32)
    imp = jnp.where(blkf < n_valid_imp, imp, 0.0)
    score = jnp.where(blkf == cur, 2.0 * GROUP, jnp.where(blkf < cur, imp, -1.0))
    score = jnp.where(blkf < 0.0, -3.0, score)
    sel = jnp.zeros(score.shape, F32)
    for _ in range(k):
        m = jnp.max(score, axis=axis, keepdims=True)
        idx = jnp.min(jnp.where(score == m, blkf, 1e9), axis=axis, keepdims=True)
        hit = blkf == idx
        sel = jnp.where(hit, 1.0, sel)
        score = jnp.where(hit, -3.0, score)
    return sel


def _sum_groups(p, rows2):
    out = p[0:rows2]
    for g in range(1, GROUP):
        out = out + p[g * rows2:(g + 1) * rows2]
    return out


def _attn_prompt_body(q_ref, gate_ref, kc_ref, vc_ref, ks_ref, vs_ref, kw_ref, vw_ref,
                      tcol_ref, qs_ref, blkc_ref, trow_ref, e_ref, pos_ref, cpos_ref, gexp_ref, o_ref,
                      kas_ref, vas_ref, kaw_ref, vaw_ref, ms_ref, as_ref, mw_ref, aw_ref, *, tq, tk, seq, top_k):
    qt = pl.program_id(1)
    q0 = qt * tq
    rows = N_HEADS * tq
    rows2 = N_KV * tq
    nt = seq // tk

    @pl.when(qt == 0)
    def _():
        ones = jnp.ones((D_KV, tk), BF16)
        for j in range(nt):
            cs = slice(j * tk, (j + 1) * tk)
            for src, dst in ((ks_ref, kas_ref), (kw_ref, kaw_ref)):
                dst[j, 0:D_KV, :] = src[0, :, cs].astype(BF16)
                dst[j, D_KV:2 * D_KV, :] = pos_ref[j]
            for src, dst in ((vs_ref, vas_ref), (vw_ref, vaw_ref)):
                dst[j, 0:D_KV, :] = src[0, :, cs].astype(BF16)
                dst[j, D_KV:2 * D_KV, :] = ones

    qbd = _make_qbd(q_ref[0], tq)
    qaug = jnp.concatenate([qbd, qs_ref[...]], axis=1)
    qpos2 = q0 + tcol_ref[...]

    g_hi = gate_ref[0].astype(BF16)
    g_r1 = gate_ref[0] - g_hi.astype(F32)
    g_mid = g_r1.astype(BF16)
    g_lo = (g_r1 - g_mid.astype(F32)).astype(BF16)
    g_all = (jnp.dot(jnp.concatenate([g_hi, g_mid], axis=1), gexp_ref[...], preferred_element_type=F32)
             + jnp.dot(g_lo, gexp_ref[0:LANES, :], preferred_element_type=F32))

    def gate_rows(c):
        return jnp.concatenate([g_all[:, (c * N_HEADS + bi) * LANES:(c * N_HEADS + bi + 1) * LANES]
                                for bi in range(N_HEADS)], axis=0)

    ncb = seq // CMP_BLOCK
    pad = jnp.zeros((LANES - ncb, D_KV), F32)
    kc = jnp.concatenate([kc_ref[0], pad], axis=0)
    vc = jnp.concatenate([vc_ref[0], pad], axis=0)
    kcaug = jnp.concatenate([kc.T.astype(BF16), cpos_ref[...]], axis=0)
    s_c = jnp.dot(qaug, kcaug, preferred_element_type=F32).reshape(GROUP, rows2, LANES)
    blk_end = (_iota((1, LANES), 1) + 1) * CMP_BLOCK - 1
    p_c = _softmax_rows(s_c, (qpos2 >= blk_end)[None]).reshape(rows, LANES)
    o_acc = gate_rows(0) * _mm(p_c, vc)

    p4 = _sum_groups(p_c, rows2)
    nbr = blkc_ref.shape[0]
    imp_t = (p4 + pltpu.roll(p4, LANES - 1, 1)).T[0:nbr]
    sel = _select_blocks(imp_t, q0 + trow_ref[...], blkc_ref[...], float(seq // SEL_BLOCK), top_k, 0).T
    sel = sel.astype(BF16)

    def make_step(kaug_ref, vaug_ref, mask_fn, m_ref, acc_ref):
        def step(j, carry):
            s = jnp.dot(qaug, kaug_ref[j], preferred_element_type=F32).reshape(GROUP, rows2, tk)
            dist = qpos2 - (j * tk + _iota((1, tk), 1))
            s = jnp.where(mask_fn(j, dist)[None], s, NEG_INF)
            m = m_ref[...]
            m_new = jnp.maximum(m, jnp.max(s, axis=2, keepdims=True))
            alpha = jnp.exp(m - m_new).reshape(rows, LANES)
            p = jnp.exp(s - jnp.concatenate([m_new] * (tk // LANES), axis=2)).astype(BF16).reshape(rows, tk)
            pv = lax.dot_general(p, vaug_ref[j], (((1,), (1,)), ((), ())), preferred_element_type=F32)
            m_ref[...] = m_new
            acc_ref[...] = jnp.concatenate([alpha, alpha], axis=1) * acc_ref[...] + pv
            return carry

        return step

    def sel_mask(j, dist):
        picked = jnp.dot(sel, e_ref[j], preferred_element_type=F32)
        return (picked > 0.5) & (dist >= 0)

    def win_mask(j, dist):
        return (dist >= 0) & (dist <= WINDOW)

    step_s = make_step(kas_ref, vas_ref, sel_mask, ms_ref, as_ref)
    step_w = make_step(kaw_ref, vaw_ref, win_mask, mw_ref, aw_ref)
    for m_ref, acc_ref in ((ms_ref, as_ref), (mw_ref, aw_ref)):
        m_ref[...] = jnp.full((GROUP, rows2, LANES), NEG_INF, F32)
        acc_ref[...] = jnp.zeros((rows, 2 * D_KV), F32)
    n_s = (q0 + tq - 1) // tk + 1
    j_lo = jnp.maximum(q0 - WINDOW, 0) // tk
    n_w = n_s - j_lo

    def both(i, carry):
        step_s(i, carry)
        return step_w(j_lo + i, carry)

    lax.fori_loop(0, n_w, both, 0)
    lax.fori_loop(n_w, n_s, step_s, 0)
    o_s = as_ref[:, 0:D_KV] / as_ref[:, D_KV:2 * D_KV]
    o_w = aw_ref[:, 0:D_KV] / aw_ref[:, D_KV:2 * D_KV]
    o_acc = o_acc + gate_rows(1) * o_s + gate_rows(2) * o_w
    o_ref[0] = _extract_heads(o_acc, tq)


def _row_consts(tq):
    r = jnp.arange(N_HEADS * tq)
    g, kv, t = r // (N_KV * tq), (r // tq) % N_KV, r % tq
    head = kv * GROUP + g
    slopes = jnp.exp2(-8.0 * (head + 1).astype(F32) / N_HEADS)
    return slopes[:, None], t.astype(jnp.int32)[:, None]


def _gate_expander():
    n = jnp.arange(N_BRANCH * N_HEADS)
    c, bi = n // N_HEADS, n % N_HEADS
    src_lane = c * N_HEADS + (bi % N_KV) * GROUP + bi // N_KV
    r = (jnp.arange(LANES)[:, None] == jnp.repeat(src_lane, LANES)[None, :]).astype(BF16)
    return jnp.concatenate([r, r], axis=0)


def _key_tiles(a, tk):
    return a.reshape(a.shape[0], -1, tk).transpose(1, 0, 2)


def _attn_prompt(q, gate, kcmp, vcmp, ks, vs, kw, vw):
    b, t, _ = q.shape
    tq = min(ATTN_TQ, t)
    tk = min(ATTN_TK, t)
    n_sb = t // SEL_BLOCK
    assert t // POS_SPLIT < 256
    slopes, tcol = _row_consts(tq)
    qs = jnp.zeros((N_HEADS * tq, LANES), F32).at[:, 0].set(POS_SPLIT * slopes[:, 0]).at[:, 1].set(slopes[:, 0])
    kpos = jnp.arange(t)
    pos = jnp.zeros((LANES, t), F32).at[0].set(kpos // POS_SPLIT).at[1].set(kpos % POS_SPLIT)
    lane = jnp.arange(LANES)
    blk = jnp.where((lane % 2 == 0) & (lane < 2 * n_sb), lane // 2, -1).astype(F32)
    e = (blk[:, None] == (kpos // SEL_BLOCK).astype(F32)[None, :])
    blk_end = (lane + 1) * CMP_BLOCK - 1
    cpos = jnp.zeros((LANES, LANES), F32).at[0].set(blk_end // POS_SPLIT).at[1].set(blk_end % POS_SPLIT)
    tcol2 = tcol[:N_KV * tq]
    nbr = min(LANES, -(-2 * n_sb // SUBLANES) * SUBLANES)
    consts = (tcol2, qs.astype(BF16), blk[:nbr, None], tcol2.reshape(1, -1),
              _key_tiles(e[:nbr].astype(BF16), tk), _key_tiles(pos.astype(BF16), tk), cpos.astype(BF16),
              _gate_expander())
    tile = lambda n: pl.BlockSpec((1, tq, n), lambda i, j: (i, j, 0))
    seq = lambda r, n: pl.BlockSpec((1, r, n), lambda i, j: (i, 0, 0))
    return pl.pallas_call(
        functools.partial(_attn_prompt_body, tq=tq, tk=tk, seq=t, top_k=min(TOP_K, n_sb)),
        grid=(b, t // tq),
        in_specs=[tile(D_ATTN), tile(LANES), seq(t // CMP_BLOCK, D_KV), seq(t // CMP_BLOCK, D_KV),
                  seq(D_KV, t), seq(D_KV, t), seq(D_KV, t), seq(D_KV, t)] + [_full(c.shape) for c in consts],
        out_specs=tile(D_ATTN),
        out_shape=jax.ShapeDtypeStruct((b, t, D_ATTN), F32),
        scratch_shapes=[pltpu.VMEM((t // tk, 2 * D_KV, tk), BF16)] * 4
        + [pltpu.VMEM((GROUP, N_KV * tq, LANES), F32), pltpu.VMEM((N_HEADS * tq, 2 * D_KV), F32)] * 2,
        compiler_params=pltpu.CompilerParams(dimension_semantics=("arbitrary", "arbitrary"),
                                             vmem_limit_bytes=VMEM_LIMIT),
        name="attn_prompt",
    )(q, gate, kcmp, vcmp, ks, vs, kw, vw, *consts)


def _attn_sample_body(pt_ref, q_ref, gate_ref, ksn_ref, vsn_ref, kwn_ref, vwn_ref, kwin_ref, vwin_ref,
                      kcp_ref, vcp_ref, cvalid_ref, bias_c_ref, bias_s_ref, bias_w_ref, qrow_ref, blkc_ref, e_ref, *rest,
                      t, n_pages, page, past, w_buf, top_k, nbs):
    ks_pages = rest[:nbs * n_pages]
    vs_pages = rest[nbs * n_pages:2 * nbs * n_pages]
    (o_ref,) = rest[2 * nbs * n_pages:]
    b0 = pl.program_id(0) * nbs
    rows = N_HEADS * t
    rows2 = N_KV * t
    seqs = range(nbs)

    def stack(xs):
        return jnp.concatenate(xs, axis=0)

    def part(x, bb, n):
        return x[bb * n:(bb + 1) * n]

    def pad_rows(x):
        if x.shape[0] == LANES:
            return x
        return jnp.concatenate([x, jnp.zeros((LANES - x.shape[0], D_KV), F32)], axis=0)

    live = NEG_INF / 2
    qbd = [_make_qbd(q_ref[bb], t) for bb in seqs]
    gates = [gate_ref[bb] for bb in seqs]

    def gate_rows(c):
        return stack([_gate_rows(gates[bb], c, t) for bb in seqs])

    kc = [pad_rows(stack([kcp_ref[pt_ref[(b0 + bb) * n_pages + p] // 2] for p in range(n_pages)])) for bb in seqs]
    vc = [pad_rows(stack([vcp_ref[pt_ref[(b0 + bb) * n_pages + p] // 2] for p in range(n_pages)])) for bb in seqs]
    bias_c = bias_c_ref[...]
    s_c = stack([_mm_nt(qbd[bb], kc[bb]) for bb in seqs]) + bias_c
    cvalid = stack([jnp.broadcast_to(cvalid_ref[bb], (rows, LANES)) for bb in seqs])
    p_c = _softmax_rows(s_c, (cvalid > 0.5) & (bias_c > live))
    o_acc = gate_rows(0) * stack([_mm(part(p_c, bb, rows), vc[bb]) for bb in seqs])

    p4 = stack([_sum_groups(part(p_c, bb, rows), rows2) for bb in seqs])
    x1 = p4 + pltpu.roll(p4, LANES - 1, 1)
    imp = x1 + pltpu.roll(x1, LANES - 4, 1)
    n2 = nbs * rows2
    imp_t = jnp.concatenate([imp, jnp.zeros((LANES - n2, LANES), F32)], axis=0).T
    sel = _select_blocks(imp_t, qrow_ref[...], blkc_ref[...], float(past // SEL_BLOCK), top_k, 0).T[0:n2]
    sel4 = stack([part(sel, bb, rows2) for bb in seqs for _ in range(GROUP)])

    s_s = stack([jnp.concatenate([_mm(qbd[bb], ks_pages[bb * n_pages + p][...]) for p in range(n_pages)]
                                 + [_mm_nt(qbd[bb], pad_rows(ksn_ref[bb]))], axis=1) for bb in seqs])
    bias_s = bias_s_ref[...]
    picked = _mm(sel4, e_ref[...])
    p_s = _softmax_rows(s_s + bias_s, (picked > 0.5) & (bias_s > live))
    o_s = []
    for bb in seqs:
        pb = part(p_s, bb, rows)
        acc = _mm(pb[:, past:], pad_rows(vsn_ref[bb]))
        for p in range(n_pages):
            acc = acc + _mm_nt(pb[:, p * page:(p + 1) * page], vs_pages[bb * n_pages + p][...])
        o_s.append(acc)
    o_acc = o_acc + gate_rows(1) * stack(o_s)

    kwin = [kwin_ref[bb] for bb in seqs]
    vwin = [vwin_ref[bb] for bb in seqs]
    s_w = stack([jnp.concatenate([_mm(qbd[bb], kwin[bb]), _mm_nt(qbd[bb], pad_rows(kwn_ref[bb]))], axis=1)
                 for bb in seqs])
    bias_w = bias_w_ref[...]
    p_w = _softmax_rows(s_w + bias_w, bias_w > live)
    o_w = stack([_mm_nt(part(p_w, bb, rows)[:, :w_buf], vwin[bb])
                 + _mm(part(p_w, bb, rows)[:, w_buf:], pad_rows(vwn_ref[bb])) for bb in seqs])
    o_acc = o_acc + gate_rows(2) * o_w
    for bb in seqs:
        o_ref[bb] = _extract_heads(part(o_acc, bb, rows), t)


def _window_update_body(kwin_ref, vwin_ref, kn_ref, vn_ref, kwo_ref, vwo_ref, *, t, w_buf, nbs):
    is_new = _iota((D_KV, LANES), 1) >= LANES - t

    def shift_in(win, new):
        new_t = jnp.concatenate([jnp.zeros((LANES - t, D_KV), F32), new], axis=0).T
        rolled = pltpu.roll(win, w_buf - t, 1)
        last = jnp.where(is_new, new_t, rolled[:, w_buf - LANES:])
        return jnp.concatenate([rolled[:, :w_buf - LANES], last], axis=1)

    for bb in range(nbs):
        kwo_ref[bb] = shift_in(kwin_ref[bb], kn_ref[bb])
        vwo_ref[bb] = shift_in(vwin_ref[bb], vn_ref[bb])


def _window_update(kwin, vwin, kw_new, vw_new):
    nl, nb, _, w_buf = kwin.shape
    t = kw_new.shape[2]
    assert w_buf % LANES == 0 and t <= LANES
    nbs = next(c for c in range(min(WIN_STEP, nb), 0, -1) if nb % c == 0)
    win = pl.BlockSpec((None, nbs, D_KV, w_buf), lambda l, i: (l, i, 0, 0))
    new = pl.BlockSpec((None, nbs, t, D_KV), lambda l, i: (l, i, 0, 0))
    return pl.pallas_call(
        functools.partial(_window_update_body, t=t, w_buf=w_buf, nbs=nbs),
        grid=(nl, nb // nbs),
        in_specs=[win, win, new, new],
        out_specs=[win, win],
        out_shape=[jax.ShapeDtypeStruct(kwin.shape, F32)] * 2,
        compiler_params=pltpu.CompilerParams(dimension_semantics=("arbitrary", "arbitrary"),
                                             vmem_limit_bytes=VMEM_LIMIT),
        name="window_update",
    )(kwin, vwin, kw_new, vw_new)


def _page_map(l, row_stride, offset, i, pt):
    return (l, pt[i * row_stride + offset], 0, 0)


def _attn_sample(l, page_table, q, gate, ks_new, vs_new, kw_new, vw_new, kwin, vwin, kcp, vcp,
                 cache_ks, cache_vs):
    nb, t, _ = q.shape
    n_pages = page_table.shape[1]
    page = cache_ks.shape[3]
    past = n_pages * page
    w_buf = kwin.shape[3]
    n_sb = -(-(past + t) // SEL_BLOCK)
    assert t == SUBLANES and n_pages * 8 <= LANES and page == LANES and past % SEL_BLOCK == 0
    assert w_buf % LANES == 0
    slopes, tcol = _row_consts(t)
    lane = jnp.arange(LANES)
    pg, r = lane // 8, lane % 8
    in_range = pg < n_pages
    cend = jnp.where(in_range, (4 * pg + r % 4 + 1) * CMP_BLOCK - 1, 1 << 30).astype(jnp.int32)[None, :]
    par = page_table % 2
    par_l = jnp.take(par, jnp.minimum(pg, n_pages - 1), axis=1)
    cvalid = ((r[None, :] // 4 == par_l) & in_range[None, :]).astype(F32)[:, None, :]
    blk = jnp.where(in_range & (r % 2 == 0) & (r < 4), 2 * pg + r // 2, -1)
    blk = jnp.where(lane == 1, n_sb - 1, blk)
    blkf = blk.astype(F32)[None, :]
    e = (blkf.reshape(-1, 1) == (jnp.arange(past + LANES) // SEL_BLOCK).astype(F32)[None, :]).astype(BF16)
    nbs = next(c for c in range(min(DEC_STEP, nb), 0, -1) if nb % c == 0)
    qpos = past + tcol
    assert nbs * N_KV * t <= LANES
    qrow = jnp.pad(jnp.tile(qpos[:N_KV * t, 0], nbs), (0, LANES - nbs * N_KV * t), constant_values=past)[None, :]
    blkc = blkf.reshape(-1, 1)

    def bias(dist, ok):
        return jnp.tile(jnp.where(ok, -(slopes * dist.astype(F32)), NEG_INF), (nbs, 1))

    dist_c = qpos - cend
    dist_s = qpos - jnp.arange(past + LANES)[None, :]
    dist_w = qpos - (past - w_buf + jnp.arange(w_buf + LANES))[None, :]
    bias_c = bias(dist_c, dist_c >= 0)
    bias_s = bias(dist_s, dist_s >= 0)
    bias_w = bias(dist_w, (dist_w >= 0) & (dist_w <= WINDOW))
    tok = lambda n: pl.BlockSpec((nbs, t, n), lambda i, pt: (i, 0, 0))
    win = pl.BlockSpec((None, nbs, D_KV, w_buf), lambda i, pt: (l, i, 0, 0))
    res = pl.BlockSpec((None,) + kcp.shape[1:], lambda i, pt: (l, 0, 0, 0))
    const = lambda a: pl.BlockSpec(a.shape, lambda i, pt: (0,) * a.ndim)
    pages = [pl.BlockSpec((None, None, D_KV, page), functools.partial(_page_map, l, nbs * n_pages, bb * n_pages + p))
             for bb in range(nbs) for p in range(n_pages)]
    grid_spec = pltpu.PrefetchScalarGridSpec(
        num_scalar_prefetch=1,
        grid=(nb // nbs,),
        in_specs=[tok(D_ATTN), tok(LANES), tok(D_KV), tok(D_KV), tok(D_KV), tok(D_KV), win, win, res, res,
                  pl.BlockSpec((nbs, 1, LANES), lambda i, pt: (i, 0, 0)),
                  const(bias_c), const(bias_s), const(bias_w), const(qrow), const(blkc), const(e)] + pages + pages,
        out_specs=[tok(D_ATTN)],
    )
    return pl.pallas_call(
        functools.partial(_attn_sample_body, t=t, n_pages=n_pages, page=page, past=past, w_buf=w_buf,
                          top_k=min(TOP_K, n_sb), nbs=nbs),
        grid_spec=grid_spec,
        out_shape=[jax.ShapeDtypeStruct((nb, t, D_ATTN), F32)],
        compiler_params=pltpu.CompilerParams(dimension_semantics=("arbitrary",), vmem_limit_bytes=VMEM_LIMIT),
        name="attn_sample",
    )(page_table.reshape(-1), q, gate, ks_new, vs_new, kw_new, vw_new, kwin, vwin, kcp, vcp, cvalid,
      bias_c, bias_s, bias_w, qrow, blkc, e, *([cache_ks] * (nbs * n_pages)), *([cache_vs] * (nbs * n_pages)))


def _back_head(x, o, pab, wg2_ref, woa_ref, wmix_ref, ln1g_ref, ln1b_ref):
    y_c = _mm(o, woa_ref[...])
    g2 = jax.nn.sigmoid(_mm(x, wg2_ref[...]))
    mixed = pab + g2 * y_c
    return _layer_norm(ALPHA * x + _mm(mixed, wmix_ref[...]), ln1g_ref[...], ln1b_ref[...])


def _back_prompt_body(x_ref, o_ref, pab_ref, wg2_ref, woa_ref, wmix_ref, ln1g_ref, ln1b_ref,
                      wup_ref, fcw_ref, fcb_ref, wdn_ref, ln2g_ref, ln2b_ref,
                      y_ref, fst_ref, ext_ref, *, tt):
    j = pl.program_id(1)

    @pl.when(j == 0)
    def _():
        ext_ref[0:8, :] = jnp.zeros((8, D_FF), F32)

    h = _back_head(x_ref[0], o_ref[0], pab_ref[0], wg2_ref, woa_ref, wmix_ref, ln1g_ref, ln1b_ref)
    hb = h.astype(BF16)
    cf = D_FF // FF_CHUNKS
    f = jnp.zeros((tt, D_MODEL), F32)
    for c in range(FF_CHUNKS):
        cs = slice(c * cf, (c + 1) * cf)
        u = _mm(hb, wup_ref[:, c * cf:(c + 1) * cf])
        gt = _mm(hb, wup_ref[:, D_FF + c * cf:D_FF + (c + 1) * cf])
        ext_ref[8:8 + tt, cs] = u
        conv = fcb_ref[:, cs] + fcw_ref[0:1, cs] * ext_ref[6:6 + tt, cs]
        conv = conv + fcw_ref[1:2, cs] * ext_ref[7:7 + tt, cs]
        conv = conv + fcw_ref[2:3, cs] * u
        f = f + _mm(jax.nn.gelu(conv) * gt, wdn_ref[c * cf:(c + 1) * cf, :])
    y_ref[0] = _layer_norm(ALPHA * h + f, ln2g_ref[...], ln2b_ref[...])
    last = ext_ref[tt:tt + 8, :]
    fst_ref[0] = last
    ext_ref[0:8, :] = last


def _back_sample_body(x_ref, o_ref, pab_ref, hf_ref, wg2_ref, woa_ref, wmix_ref, ln1g_ref, ln1b_ref,
                      wup_ref, fcw_ref, fcb_ref, wdn_ref, ln2g_ref, ln2b_ref,
                      y_ref, fst_ref, ext_ref, *, ns, t):
    h = _back_head(x_ref[...], o_ref[...], pab_ref[...], wg2_ref, woa_ref, wmix_ref, ln1g_ref, ln1b_ref)
    hb = h.astype(BF16)
    cf = D_FF // FF_CHUNKS
    ext_ref[:, 6:8, :] = hf_ref[...]
    f = jnp.zeros((ns * t, D_MODEL), F32)
    for c in range(FF_CHUNKS):
        cs = slice(c * cf, (c + 1) * cf)
        u = _mm(hb, wup_ref[:, c * cf:(c + 1) * cf])
        gt = _mm(hb, wup_ref[:, D_FF + c * cf:D_FF + (c + 1) * cf])
        ext_ref[:, 8:8 + t, cs] = u.reshape(ns, t, cf)
        fcw = fcw_ref[:, cs]
        conv = fcb_ref[:, cs] + fcw[0:1] * ext_ref[:, 6:6 + t, cs]
        conv = conv + fcw[1:2] * ext_ref[:, 7:7 + t, cs]
        conv = conv + fcw[2:3] * ext_ref[:, 8:8 + t, cs]
        f = f + _mm(jax.nn.gelu(conv).reshape(ns * t, cf) * gt, wdn_ref[c * cf:(c + 1) * cf, :])
    y_ref[...] = _layer_norm(ALPHA * h + f, ln2g_ref[...], ln2b_ref[...])
    fst_ref[...] = ext_ref[:, t:t + 8, :]


def _back_prompt(x, o, pab, wts):
    b, t, _ = x.shape
    tt = min(FRONT_TILE, t)
    tok = lambda n: pl.BlockSpec((1, tt, n), lambda i, j: (i, j, 0))
    return pl.pallas_call(
        functools.partial(_back_prompt_body, tt=tt),
        grid=(b, t // tt),
        in_specs=[tok(D_MODEL), tok(D_ATTN), tok(D_MODEL)] + [_full(w.shape) for w in wts],
        out_specs=[tok(D_MODEL), pl.BlockSpec((1, 8, D_FF), lambda i, j: (i, 0, 0))],
        out_shape=[jax.ShapeDtypeStruct((b, t, D_MODEL), F32), jax.ShapeDtypeStruct((b, 8, D_FF), F32)],
        scratch_shapes=[pltpu.VMEM((8 + tt, D_FF), F32)],
        compiler_params=pltpu.CompilerParams(dimension_semantics=("arbitrary", "arbitrary"),
                                             vmem_limit_bytes=VMEM_LIMIT),
        name="back_prompt",
    )(x, o, pab, *wts)


def _back_sample(x, o, pab, hist_f, wts, t):
    n = x.shape[0]
    nb = n // t
    ns = min(FRONT_TILE // t, nb)
    rows = ns * t
    tok = lambda c: pl.BlockSpec((rows, c), lambda i: (i, 0))
    return pl.pallas_call(
        functools.partial(_back_sample_body, ns=ns, t=t),
        grid=(nb // ns,),
        in_specs=[tok(D_MODEL), tok(D_ATTN), tok(D_MODEL),
                  pl.BlockSpec((ns, CONV_W - 1, D_FF), lambda i: (i, 0, 0))] + [_full(w.shape) for w in wts],
        out_specs=[tok(D_MODEL), pl.BlockSpec((ns, 8, D_FF), lambda i: (i, 0, 0))],
        out_shape=[jax.ShapeDtypeStruct((n, D_MODEL), F32), jax.ShapeDtypeStruct((nb, 8, D_FF), F32)],
        scratch_shapes=[pltpu.VMEM((ns, 8 + t, D_FF), F32)],
        compiler_params=pltpu.CompilerParams(dimension_semantics=("arbitrary",), vmem_limit_bytes=VMEM_LIMIT),
        name="back_sample",
    )(x, o, pab, hist_f, *wts)


def _block_diag2(w):
    z = jnp.zeros_like(w)
    return jnp.concatenate([jnp.concatenate([w, z], axis=-1), jnp.concatenate([z, w], axis=-1)], axis=-2)


def kernel(x_prompt, x_sample, cache_k_cmp, cache_v_cmp, cache_k_sel, cache_v_sel, state_k_win, state_v_win, state_conv, state_pool, state_ffn_conv, page_table, w_in, conv_w, conv_b, w_out_conv, pool_w, pool_scale, w_cmp_k, w_cmp_v, w_out_attn, w_mix_out, ln1_g, ln1_b, w_ffn_up, ffn_conv_w, ffn_conv_b, w_ffn_down, ln2_g, ln2_b):
    depth = w_in.shape[0]
    bp, seq, _ = x_prompt.shape
    nb, t, _ = x_sample.shape
    n_phys, page = cache_k_cmp.shape[1], cache_k_cmp.shape[2]
    past = page_table.shape[1] * page
    w_buf = state_k_win.shape[2]

    w_front = jnp.concatenate(
        [w_in[:, :, :IN_GATES], jnp.zeros((depth, D_MODEL, C_G01 - IN_GATES), F32),
         w_in[:, :, IN_GATES:IN_GATES + 2 * D_MODEL]], axis=2).astype(BF16)
    w_g2 = w_in[:, :, IN_GATES + 2 * D_MODEL:].astype(BF16)
    ng = pool_w.shape[1]
    pool_bd = jnp.zeros((depth, D_POOL, D_MODEL), F32)
    for g in range(ng):
        pool_bd = pool_bd.at[:, g * POOL_GROUP:(g + 1) * POOL_GROUP,
                             g * (D_MODEL // ng):(g + 1) * (D_MODEL // ng)].set(pool_w[:, g])
    pool_bd = pool_bd.astype(BF16)
    wck = _block_diag2(w_cmp_k).astype(BF16)
    wcv = _block_diag2(w_cmp_v).astype(BF16)
    w_oc, w_oa, w_mix = w_out_conv.astype(BF16), w_out_attn.astype(BF16), w_mix_out.astype(BF16)
    w_up, w_dn = w_ffn_up.astype(BF16), w_ffn_down.astype(BF16)
    row = lambda a: a[:, None, :]

    keys_minor = lambda a: jnp.transpose(a, (0, 1, 3, 4, 2)).reshape(a.shape[0], a.shape[1], D_KV, a.shape[2])
    tokens_major = lambda a: jnp.transpose(a.reshape(a.shape[0], N_KV, HEAD_DIM, a.shape[2]), (0, 3, 1, 2))
    ck_s, cv_s = keys_minor(cache_k_sel), keys_minor(cache_v_sel)
    kwin_all, vwin_all = keys_minor(state_k_win), keys_minor(state_v_win)
    kcp, vcp = _compress(keys_minor(cache_k_cmp), keys_minor(cache_v_cmp), wck, wcv)
    kcp = kcp.reshape(depth, n_phys // 2, 8, D_KV)
    vcp = vcp.reshape(depth, n_phys // 2, 8, D_KV)

    y_p = x_prompt
    y_s = x_sample.reshape(nb * t, D_MODEL)
    st_p, st_s = [], []
    for l in range(depth):
        fw = (w_front[l], conv_w[l], row(conv_b)[l], w_oc[l], pool_bd[l], row(pool_scale)[l])
        bw = (w_g2[l], w_oa[l], w_mix[l], row(ln1_g)[l], row(ln1_b)[l], w_up[l], ffn_conv_w[l],
              row(ffn_conv_b)[l], w_dn[l], row(ln2_g)[l], row(ln2_b)[l])

        pab, q, kc, vc, ks, vs, kw, vw, gate, cst, pst = _front_prompt(y_p, fw)
        kcmp, vcmp = _compress(kc[None], vc[None], wck[l:l + 1], wcv[l:l + 1])
        o = _attn_prompt(q, gate, kcmp.reshape(bp, seq // CMP_BLOCK, D_KV),
                         vcmp.reshape(bp, seq // CMP_BLOCK, D_KV), ks, vs, kw, vw)
        y_p, fst = _back_prompt(y_p, o, pab, bw)
        st_p.append(tuple(tokens_major(a) for a in (kc, vc, ks, vs, kw[:, :, seq - w_buf:], vw[:, :, seq - w_buf:]))
                    + (cst[:, 8 - (CONV_W - 1):], pst[:, 16 - POOL_HIST:], fst[:, 8 - (CONV_W - 1):]))

        pab, q, kc, vc, ks, vs, kw, vw, gate, cst, pst = _front_sample(
            y_s.reshape(nb, t, D_MODEL), state_conv[l], state_pool[l], fw, past)
        r3 = lambda a: a.reshape(nb, t, a.shape[-1])
        (o,) = _attn_sample(l, page_table, r3(q), r3(gate), r3(ks), r3(vs), r3(kw), r3(vw),
                            kwin_all, vwin_all, kcp, vcp, ck_s, cv_s)
        y_s, fst = _back_sample(y_s, o.reshape(nb * t, D_ATTN), pab, state_ffn_conv[l], bw, t)
        st_s.append((kc, vc, ks, vs, kw, vw,
                     cst[:, 8 - (CONV_W - 1):], pst[:, 16 - POOL_HIST:], fst[:, 8 - (CONV_W - 1):]))

    kc_p, vc_p, ks_p, vs_p, kw_p, vw_p, conv_p, pool_p, ffn_p = [jnp.stack(a) for a in zip(*st_p)]
    kc_s, vc_s, ks_s, vs_s, kwn_s, vwn_s, conv_s, pool_s, ffn_s = [jnp.stack(a) for a in zip(*st_s)]
    kc_s, vc_s, ks_s, vs_s = [a.reshape(depth, nb, t, N_KV, HEAD_DIM) for a in (kc_s, vc_s, ks_s, vs_s)]
    kw_s, vw_s = _window_update(kwin_all, vwin_all, kwn_s.reshape(depth, nb, t, D_KV),
                                vwn_s.reshape(depth, nb, t, D_KV))
    kw_s, vw_s = [jnp.transpose(a.reshape(depth, nb, N_KV, HEAD_DIM, w_buf), (0, 1, 4, 2, 3)) for a in (kw_s, vw_s)]
    return (y_p, y_s.reshape(nb, t, D_MODEL), kc_p, kc_s, vc_p, vc_s, ks_p, ks_s, vs_p, vs_s,
            kw_p, kw_s, vw_p, vw_s, conv_p, conv_s, pool_p, pool_s, ffn_p, ffn_s)
```

```python
import functools

import jax
import jax.numpy as jnp
from jax import lax
from jax.experimental import pallas as pl
from jax.experimental.pallas import tpu as pltpu

F32 = jnp.float32
BF16 = jnp.bfloat16

D_MODEL = 1024
DEPTH = 4
D_CONV = 256
CONV_W = 3
D_POOL = 256
POOL_WINDOWS = (2, 4, 8, 16)
POOL_GROUP = D_POOL // len(POOL_WINDOWS)
POOL_HIST = 15
N_HEADS = 8
N_KV = 2
GROUP = N_HEADS // N_KV
HEAD_DIM = 64
D_ATTN = N_HEADS * HEAD_DIM
D_KV = N_KV * HEAD_DIM
CMP_BLOCK = 32
SEL_BLOCK = 64
TOP_K = 8
WINDOW = 512
N_BRANCH = 3
D_FF = 2816
ALPHA = (2.0 * DEPTH) ** 0.25
LN_EPS = 1e-5
ATTN_SCALE = HEAD_DIM ** -0.5
NEG_INF = -1e30
POS_SPLIT = 64

LANES = 128
SUBLANES = 8
VMEM_LIMIT = 56 * 1024 * 1024

C_AB, C_AC, C_AH, C_U, C_Q = 0, D_CONV, 2 * D_CONV, 3 * D_CONV, 3 * D_CONV + D_POOL
C_KV = C_Q + D_ATTN
C_NSA = C_KV + 6 * D_KV
C_G01 = C_NSA + LANES
IN_MAIN = C_NSA
IN_GATES = IN_MAIN + N_BRANCH * N_HEADS

FRONT_TILE = 256
ATTN_TQ = 128
ATTN_TK = 256
CMP_ROWS = 8192
CMP_PITCH = LANES + SUBLANES
DEC_STEP = 4
WIN_STEP = 8
FF_CHUNKS = 1


def _mm(a, b):
    return jnp.dot(a.astype(BF16), b.astype(BF16), preferred_element_type=F32)


def _mm_nt(a, b):
    return lax.dot_general(a.astype(BF16), b.astype(BF16), (((1,), (1,)), ((), ())),
                           preferred_element_type=F32)


def _layer_norm(x, g, b):
    mu = jnp.mean(x, axis=-1, keepdims=True)
    var = jnp.mean(jnp.square(x - mu), axis=-1, keepdims=True)
    return (x - mu) * lax.rsqrt(var + LN_EPS) * g + b


def _iota(shape, axis):
    return lax.broadcasted_iota(jnp.int32, shape, axis)


def _pool_delta(u, shifted, pos):
    nd = u.ndim
    acc = u
    sums = {}
    for j in range(1, POOL_WINDOWS[-1]):
        acc = acc + shifted(j)
        if j + 1 in POOL_WINDOWS:
            sums[j + 1] = acc
    grp = _iota((1,) * (nd - 1) + (D_POOL,), nd - 1) // POOL_GROUP
    s = sums[POOL_WINDOWS[-1]]
    win = jnp.full(grp.shape, POOL_WINDOWS[-1], jnp.int32)
    for g in range(len(POOL_WINDOWS) - 2, -1, -1):
        s = jnp.where(grp == g, sums[POOL_WINDOWS[g]], s)
        win = jnp.where(grp == g, POOL_WINDOWS[g], win)
    cnt = jnp.minimum(win, pos + 1).astype(F32)
    return s / cnt - u


def _front_tail(z, conv, d, woc_ref, pbd_ref, psc_ref, outs, prompt):
    pab_ref, q_ref, kv_refs, gate_ref = outs
    y_a = _mm(z[:, C_AB:C_AB + D_CONV] * conv, woc_ref[...])
    y_b = _mm(d, pbd_ref[...]) * psc_ref[...]
    g0 = jax.nn.sigmoid(z[:, C_G01:C_G01 + D_MODEL])
    g1 = jax.nn.sigmoid(z[:, C_G01 + D_MODEL:C_G01 + 2 * D_MODEL])
    pab_ref[...] = (g0 * y_a + g1 * y_b).reshape(pab_ref.shape)
    q_ref[...] = z[:, C_Q:C_Q + D_ATTN].reshape(q_ref.shape)
    for i, r in enumerate(kv_refs):
        kv = z[:, C_KV + i * D_KV:C_KV + (i + 1) * D_KV]
        r[...] = (kv.T if prompt else kv).reshape(r.shape)
    gate_ref[...] = jax.nn.sigmoid(z[:, C_NSA:C_NSA + LANES]).reshape(gate_ref.shape)


def _front_prompt_body(x_ref, w_ref, cw_ref, cb_ref, woc_ref, pbd_ref, psc_ref,
                       pab_ref, q_ref, kc_ref, vc_ref, ks_ref, vs_ref, kw_ref, vw_ref, gate_ref,
                       cst_ref, pst_ref, extc_ref, extu_ref, *, tt):
    j = pl.program_id(1)

    @pl.when(j == 0)
    def _():
        extc_ref[0:8, :] = jnp.zeros((8, D_CONV), F32)
        extu_ref[0:16, :] = jnp.zeros((16, D_POOL), F32)

    z = _mm(x_ref[0], w_ref[...])
    ch = z[:, C_AC:C_AC + D_CONV] * z[:, C_AH:C_AH + D_CONV]
    extc_ref[8:8 + tt, :] = ch
    conv = cb_ref[...] + cw_ref[0:1, :] * extc_ref[6:6 + tt, :]
    conv = conv + cw_ref[1:2, :] * extc_ref[7:7 + tt, :]
    conv = conv + cw_ref[2:3, :] * ch
    u = z[:, C_U:C_U + D_POOL]
    extu_ref[16:16 + tt, :] = u
    pos = j * tt + _iota((tt, 1), 0)
    d = _pool_delta(u, lambda s: extu_ref[16 - s:16 - s + tt, :], pos)
    _front_tail(z, conv, d, woc_ref, pbd_ref, psc_ref,
                (pab_ref, q_ref, (kc_ref, vc_ref, ks_ref, vs_ref, kw_ref, vw_ref), gate_ref), True)
    last_c = extc_ref[tt:tt + 8, :]
    last_u = extu_ref[tt:tt + 16, :]
    cst_ref[0] = last_c
    pst_ref[0] = last_u
    extc_ref[0:8, :] = last_c
    extu_ref[0:16, :] = last_u


def _front_sample_body(x_ref, hc_ref, hu_ref, w_ref, cw_ref, cb_ref, woc_ref, pbd_ref, psc_ref,
                       pab_ref, q_ref, kc_ref, vc_ref, ks_ref, vs_ref, kw_ref, vw_ref, gate_ref,
                       cst_ref, pst_ref, extc_ref, extu_ref, *, ns, t, pos0):
    z = _mm(x_ref[...], w_ref[...])
    ch = z[:, C_AC:C_AC + D_CONV] * z[:, C_AH:C_AH + D_CONV]
    extc_ref[:, 6:8, :] = hc_ref[...]
    extc_ref[:, 8:8 + t, :] = ch.reshape(ns, t, D_CONV)
    cw = cw_ref[...]
    conv = cb_ref[...] + cw[0:1] * extc_ref[:, 6:6 + t, :]
    conv = conv + cw[1:2] * extc_ref[:, 7:7 + t, :]
    conv = conv + cw[2:3] * extc_ref[:, 8:8 + t, :]
    u = z[:, C_U:C_U + D_POOL]
    extu_ref[:, 1:16, :] = hu_ref[...]
    extu_ref[:, 16:16 + t, :] = u.reshape(ns, t, D_POOL)
    pos = pos0 + _iota((1, t, 1), 1)
    d = _pool_delta(extu_ref[:, 16:16 + t, :], lambda s: extu_ref[:, 16 - s:16 - s + t, :], pos)
    _front_tail(z, conv.reshape(ns * t, D_CONV), d.reshape(ns * t, D_POOL), woc_ref, pbd_ref, psc_ref,
                (pab_ref, q_ref, (kc_ref, vc_ref, ks_ref, vs_ref, kw_ref, vw_ref), gate_ref), False)
    cst_ref[...] = extc_ref[:, t:t + 8, :]
    pst_ref[...] = extu_ref[:, t:t + 16, :]


def _full(shape):
    n = len(shape)
    return pl.BlockSpec(shape, lambda *_: (0,) * n, pipeline_mode=pl.Buffered(1))


def _front_prompt(x, wts):
    b, t, _ = x.shape
    tt = min(FRONT_TILE, t)
    tok = lambda n: pl.BlockSpec((1, tt, n), lambda i, j: (i, j, 0))
    st = lambda r, n: pl.BlockSpec((1, r, n), lambda i, j: (i, 0, 0))
    kvt = pl.BlockSpec((1, D_KV, tt), lambda i, j: (i, 0, j))
    out_shape = ([jax.ShapeDtypeStruct((b, t, D_MODEL), F32), jax.ShapeDtypeStruct((b, t, D_ATTN), F32)]
                 + [jax.ShapeDtypeStruct((b, D_KV, t), F32)] * 6
                 + [jax.ShapeDtypeStruct((b, t, LANES), F32),
                    jax.ShapeDtypeStruct((b, 8, D_CONV), F32), jax.ShapeDtypeStruct((b, 16, D_POOL), F32)])
    out_specs = ([tok(D_MODEL), tok(D_ATTN)] + [kvt] * 6 + [tok(LANES), st(8, D_CONV), st(16, D_POOL)])
    return pl.pallas_call(
        functools.partial(_front_prompt_body, tt=tt),
        grid=(b, t // tt),
        in_specs=[tok(D_MODEL)] + [_full(w.shape) for w in wts],
        out_specs=out_specs,
        out_shape=out_shape,
        scratch_shapes=[pltpu.VMEM((8 + tt, D_CONV), F32), pltpu.VMEM((16 + tt, D_POOL), F32)],
        compiler_params=pltpu.CompilerParams(dimension_semantics=("arbitrary", "arbitrary"),
                                             vmem_limit_bytes=VMEM_LIMIT),
        name="front_prompt",
    )(x, *wts)


def _front_sample(x, hist_c, hist_u, wts, pos0):
    nb, t, _ = x.shape
    ns = min(FRONT_TILE // t, nb)
    rows = ns * t
    xf = x.reshape(nb * t, D_MODEL)
    tok = lambda n: pl.BlockSpec((rows, n), lambda i: (i, 0))
    st = lambda r, n: pl.BlockSpec((ns, r, n), lambda i: (i, 0, 0))
    out_shape = ([jax.ShapeDtypeStruct((nb * t, D_MODEL), F32), jax.ShapeDtypeStruct((nb * t, D_ATTN), F32)]
                 + [jax.ShapeDtypeStruct((nb * t, D_KV), F32)] * 6
                 + [jax.ShapeDtypeStruct((nb * t, LANES), F32),
                    jax.ShapeDtypeStruct((nb, 8, D_CONV), F32), jax.ShapeDtypeStruct((nb, 16, D_POOL), F32)])
    out_specs = ([tok(D_MODEL), tok(D_ATTN)] + [tok(D_KV)] * 6 + [tok(LANES), st(8, D_CONV), st(16, D_POOL)])
    return pl.pallas_call(
        functools.partial(_front_sample_body, ns=ns, t=t, pos0=pos0),
        grid=(nb // ns,),
        in_specs=[tok(D_MODEL), st(CONV_W - 1, D_CONV), st(POOL_HIST, D_POOL)] + [_full(w.shape) for w in wts],
        out_specs=out_specs,
        out_shape=out_shape,
        scratch_shapes=[pltpu.VMEM((ns, 8 + t, D_CONV), F32), pltpu.VMEM((ns, 16 + t, D_POOL), F32)],
        compiler_params=pltpu.CompilerParams(dimension_semantics=("arbitrary",),
                                             vmem_limit_bytes=VMEM_LIMIT),
        name="front_sample",
    )(xf, hist_c, hist_u, *wts)


def _compress_body(k_ref, v_ref, wk_ref, wv_ref, ko_ref, vo_ref, kr_ref, vr_ref, *, npg, r):
    cols = r // LANES
    nvp = npg * cols
    for i in range(npg):
        for c in range(cols):
            row0 = (i * cols + c) * CMP_PITCH
            kr_ref[row0:row0 + LANES, :] = k_ref[i, :, c * LANES:(c + 1) * LANES].astype(BF16).T.astype(F32)
            vr_ref[row0:row0 + LANES, :] = v_ref[i, :, c * LANES:(c + 1) * LANES].astype(BF16).T.astype(F32)
    nj = LANES // CMP_BLOCK
    acc_k = jnp.zeros((nj * nvp, D_KV), F32)
    acc_v = jnp.zeros((nj * nvp, D_KV), F32)
    def rows(ref, l):
        return jnp.concatenate([ref[pl.ds(j * CMP_BLOCK + l, nvp, stride=CMP_PITCH), :] for j in range(nj)], axis=0)

    for l in range(0, CMP_BLOCK, 2):
        acc_k = acc_k + _mm(jnp.concatenate([rows(kr_ref, l), rows(kr_ref, l + 1)], axis=1),
                            wk_ref[l:l + 2].reshape(2 * D_KV, D_KV))
        acc_v = acc_v + _mm(jnp.concatenate([rows(vr_ref, l), rows(vr_ref, l + 1)], axis=1),
                            wv_ref[l:l + 2].reshape(2 * D_KV, D_KV))
    for j in range(nj):
        ko_ref[pl.ds(j, nvp, stride=nj), :] = acc_k[j * nvp:(j + 1) * nvp]
        vo_ref[pl.ds(j, nvp, stride=nj), :] = acc_v[j * nvp:(j + 1) * nvp]


def _compress(k, v, wk, wv):
    nl, pages, _, r = k.shape
    npg = next(c for c in range(min(max(CMP_ROWS // r, 1), pages), 0, -1) if pages % c == 0)
    nblk = npg * r // CMP_BLOCK
    rspec = pl.BlockSpec((None, npg, D_KV, r), lambda l, i: (l, i, 0, 0))
    wspec = pl.BlockSpec((None, CMP_BLOCK, D_KV, D_KV), lambda l, i: (l, 0, 0, 0))
    ospec = pl.BlockSpec((None, nblk, D_KV), lambda l, i: (l, i, 0))
    return pl.pallas_call(
        functools.partial(_compress_body, npg=npg, r=r),
        grid=(nl, pages // npg),
        in_specs=[rspec, rspec, wspec, wspec],
        out_specs=[ospec, ospec],
        out_shape=[jax.ShapeDtypeStruct((nl, pages * r // CMP_BLOCK, D_KV), F32)] * 2,
        scratch_shapes=[pltpu.VMEM((npg * (r // LANES) * CMP_PITCH, D_KV), F32)] * 2,
        compiler_params=pltpu.CompilerParams(dimension_semantics=("arbitrary", "arbitrary"),
                                             vmem_limit_bytes=VMEM_LIMIT),
        name="compress",
    )(k, v, wk, wv)


def _make_qbd(q, tq):
    lo = _iota((tq, LANES), 1) < HEAD_DIM
    blocks = {}
    for jv in range(D_ATTN // LANES):
        a = q[:, LANES * jv:LANES * (jv + 1)] * ATTN_SCALE
        r = pltpu.roll(a, HEAD_DIM, 1)
        for half in range(2):
            h = 2 * jv + half
            kv, g = h // GROUP, h % GROUP
            if kv == 0:
                blocks[(g, kv)] = jnp.where(lo, a if half == 0 else r, 0.0)
            else:
                blocks[(g, kv)] = jnp.where(lo, 0.0, a if half == 1 else r)
    return jnp.concatenate([blocks[(g, kv)] for g in range(GROUP) for kv in range(N_KV)], axis=0).astype(BF16)


def _extract_heads(o, tq):
    lo = _iota((tq, LANES), 1) < HEAD_DIM
    cols = []
    for jv in range(D_ATTN // LANES):
        parts = []
        for half in range(2):
            h = 2 * jv + half
            kv, g = h // GROUP, h % GROUP
            blk = o[(g * N_KV + kv) * tq:(g * N_KV + kv + 1) * tq]
            if (half == 1) != (kv == 1):
                blk = pltpu.roll(blk, HEAD_DIM, 1)
            parts.append(blk)
        cols.append(jnp.where(lo, parts[0], parts[1]))
    return jnp.concatenate(cols, axis=1)


def _gate_rows(gate, c, tq):
    return jnp.concatenate([gate[:, c * N_HEADS + kv * GROUP + g:c * N_HEADS + kv * GROUP + g + 1]
                            for g in range(GROUP) for kv in range(N_KV)], axis=0)


def _softmax_rows(s, valid):
    s = jnp.where(valid, s, NEG_INF)
    m = jnp.max(s, axis=-1, keepdims=True)
    e = jnp.exp(s - m)
    p = e / jnp.sum(e, axis=-1, keepdims=True)
    return jnp.where(valid, p, 0.0)


def _select_blocks(imp, qpos2, blkf, n_valid_imp, k, axis):
    cur = (qpos2 // SEL_BLOCK).astype(F32)
    imp = jnp.where(blkf < n_valid_imp, imp, 0.0)
    score = jnp.where(blkf == cur, 2.0 * GROUP, jnp.where(blkf < cur, imp, -1.0))
    score = jnp.where(blkf < 0.0, -3.0, score)
    sel = jnp.zeros(score.shape, F32)
    for _ in range(k):
        m = jnp.max(score, axis=axis, keepdims=True)
        idx = jnp.min(jnp.where(score == m, blkf, 1e9), axis=axis, keepdims=True)
        hit = blkf == idx
        sel = jnp.where(hit, 1.0, sel)
        score = jnp.where(hit, -3.0, score)
    return sel


def _sum_groups(p, rows2):
    out = p[0:rows2]
    for g in range(1, GROUP):
        out = out + p[g * rows2:(g + 1) * rows2]
    return out


def _attn_prompt_body(q_ref, gate_ref, kc_ref, vc_ref, ks_ref, vs_ref, kw_ref, vw_ref,
                      tcol_ref, qs_ref, blkc_ref, trow_ref, e_ref, pos_ref, cpos_ref, gexp_ref, o_ref,
                      kas_ref, vas_ref, kaw_ref, vaw_ref, ms_ref, as_ref, mw_ref, aw_ref, kca_ref, vcb_ref,
                      *, tq, tk, seq, top_k):
    qt = pl.program_id(1)
    q0 = qt * tq
    rows = N_HEADS * tq
    rows2 = N_KV * tq
    nt = seq // tk

    @pl.when(qt == 0)
    def _():
        ones = jnp.ones((D_KV, tk), BF16)
        for j in range(nt):
            cs = slice(j * tk, (j + 1) * tk)
            for src, dst in ((ks_ref, kas_ref), (kw_ref, kaw_ref)):
                dst[j, 0:D_KV, :] = src[0, :, cs].astype(BF16)
                dst[j, D_KV:2 * D_KV, :] = pos_ref[j]
            for src, dst in ((vs_ref, vas_ref), (vw_ref, vaw_ref)):
                dst[j, 0:D_KV, :] = src[0, :, cs].astype(BF16)
                dst[j, D_KV:2 * D_KV, :] = ones
        pad = jnp.zeros((LANES - seq // CMP_BLOCK, D_KV), F32)
        kca_ref[0:D_KV, :] = jnp.concatenate([kc_ref[0], pad], axis=0).T.astype(BF16)
        kca_ref[D_KV:2 * D_KV, :] = cpos_ref[...]
        vcb_ref[...] = jnp.concatenate([vc_ref[0], pad], axis=0).astype(BF16)

    qbd = _make_qbd(q_ref[0], tq)
    qaug = jnp.concatenate([qbd, qs_ref[...]], axis=1)
    qpos2 = q0 + tcol_ref[...]

    g_hi = gate_ref[0].astype(BF16)
    g_r1 = gate_ref[0] - g_hi.astype(F32)
    g_mid = g_r1.astype(BF16)
    g_lo = (g_r1 - g_mid.astype(F32)).astype(BF16)
    g_all = (jnp.dot(jnp.concatenate([g_hi, g_mid], axis=1), gexp_ref[...], preferred_element_type=F32)
             + jnp.dot(g_lo, gexp_ref[0:LANES, :], preferred_element_type=F32))

    def gate_rows(c):
        return jnp.concatenate([g_all[:, (c * N_HEADS + bi) * LANES:(c * N_HEADS + bi + 1) * LANES]
                                for bi in range(N_HEADS)], axis=0)

    s_c = jnp.dot(qaug, kca_ref[...], preferred_element_type=F32).reshape(GROUP, rows2, LANES)
    blk_end = (_iota((1, LANES), 1) + 1) * CMP_BLOCK - 1
    p_c = _softmax_rows(s_c, (qpos2 >= blk_end)[None]).reshape(rows, LANES)
    o_acc = gate_rows(0) * _mm(p_c, vcb_ref[...])

    p4 = _sum_groups(p_c, rows2)
    nbr = blkc_ref.shape[0]
    imp_t = (p4 + pltpu.roll(p4, LANES - 1, 1)).T[0:nbr]
    sel = _select_blocks(imp_t, q0 + trow_ref[...], blkc_ref[...], float(seq // SEL_BLOCK), top_k, 0).T
    sel = sel.astype(BF16)

    def make_step(kaug_ref, vaug_ref, mask_fn, m_ref, acc_ref):
        def step(j, carry):
            s = jnp.dot(qaug, kaug_ref[j], preferred_element_type=F32).reshape(GROUP, rows2, tk)
            dist = qpos2 - (j * tk + _iota((1, tk), 1))
            s = jnp.where(mask_fn(j, dist)[None], s, NEG_INF)
            m = m_ref[...]
            m_new = jnp.maximum(m, jnp.max(s, axis=2, keepdims=True))
            alpha = jnp.exp(m - m_new).reshape(rows, LANES)
            p = jnp.exp(s - jnp.concatenate([m_new] * (tk // LANES), axis=2)).astype(BF16).reshape(rows, tk)
            pv = lax.dot_general(p, vaug_ref[j], (((1,), (1,)), ((), ())), preferred_element_type=F32)
            m_ref[...] = m_new
            acc_ref[...] = jnp.concatenate([alpha, alpha], axis=1) * acc_ref[...] + pv
            return carry

        return step

    def sel_mask(j, dist):
        picked = jnp.dot(sel, e_ref[j], preferred_element_type=F32)
        return (picked > 0.5) & (dist >= 0)

    def win_mask(j, dist):
        return (dist >= 0) & (dist <= WINDOW)

    step_s = make_step(kas_ref, vas_ref, sel_mask, ms_ref, as_ref)
    step_w = make_step(kaw_ref, vaw_ref, win_mask, mw_ref, aw_ref)
    for m_ref, acc_ref in ((ms_ref, as_ref), (mw_ref, aw_ref)):
        m_ref[...] = jnp.full((GROUP, rows2, LANES), NEG_INF, F32)
        acc_ref[...] = jnp.zeros((rows, 2 * D_KV), F32)
    n_s = (q0 + tq - 1) // tk + 1
    j_lo = jnp.maximum(q0 - WINDOW, 0) // tk
    n_w = n_s - j_lo

    def both(i, carry):
        step_s(i, carry)
        return step_w(j_lo + i, carry)

    lax.fori_loop(0, n_w, both, 0)
    lax.fori_loop(n_w, n_s, step_s, 0)
    o_s = as_ref[:, 0:D_KV] / as_ref[:, D_KV:2 * D_KV]
    o_w = aw_ref[:, 0:D_KV] / aw_ref[:, D_KV:2 * D_KV]
    o_acc = o_acc + gate_rows(1) * o_s + gate_rows(2) * o_w
    o_ref[0] = _extract_heads(o_acc, tq)


def _row_consts(tq):
    r = jnp.arange(N_HEADS * tq)
    g, kv, t = r // (N_KV * tq), (r // tq) % N_KV, r % tq
    head = kv * GROUP + g
    slopes = jnp.exp2(-8.0 * (head + 1).astype(F32) / N_HEADS)
    return slopes[:, None], t.astype(jnp.int32)[:, None]


def _gate_expander():
    n = jnp.arange(N_BRANCH * N_HEADS)
    c, bi = n // N_HEADS, n % N_HEADS
    src_lane = c * N_HEADS + (bi % N_KV) * GROUP + bi // N_KV
    r = (jnp.arange(LANES)[:, None] == jnp.repeat(src_lane, LANES)[None, :]).astype(BF16)
    return jnp.concatenate([r, r], axis=0)


def _key_tiles(a, tk):
    return a.reshape(a.shape[0], -1, tk).transpose(1, 0, 2)


def _attn_prompt(q, gate, kcmp, vcmp, ks, vs, kw, vw):
    b, t, _ = q.shape
    tq = min(ATTN_TQ, t)
    tk = min(ATTN_TK, t)
    n_sb = t // SEL_BLOCK
    assert t // POS_SPLIT < 256
    slopes, tcol = _row_consts(tq)
    qs = jnp.zeros((N_HEADS * tq, LANES), F32).at[:, 0].set(POS_SPLIT * slopes[:, 0]).at[:, 1].set(slopes[:, 0])
    kpos = jnp.arange(t)
    pos = jnp.zeros((LANES, t), F32).at[0].set(kpos // POS_SPLIT).at[1].set(kpos % POS_SPLIT)
    lane = jnp.arange(LANES)
    blk = jnp.where((lane % 2 == 0) & (lane < 2 * n_sb), lane // 2, -1).astype(F32)
    e = (blk[:, None] == (kpos // SEL_BLOCK).astype(F32)[None, :])
    blk_end = (lane + 1) * CMP_BLOCK - 1
    cpos = jnp.zeros((LANES, LANES), F32).at[0].set(blk_end // POS_SPLIT).at[1].set(blk_end % POS_SPLIT)
    tcol2 = tcol[:N_KV * tq]
    nbr = min(LANES, -(-2 * n_sb // SUBLANES) * SUBLANES)
    consts = (tcol2, qs.astype(BF16), blk[:nbr, None], tcol2.reshape(1, -1),
              _key_tiles(e[:nbr].astype(BF16), tk), _key_tiles(pos.astype(BF16), tk), cpos.astype(BF16),
              _gate_expander())
    tile = lambda n: pl.BlockSpec((1, tq, n), lambda i, j: (i, j, 0))
    seq = lambda r, n: pl.BlockSpec((1, r, n), lambda i, j: (i, 0, 0))
    return pl.pallas_call(
        functools.partial(_attn_prompt_body, tq=tq, tk=tk, seq=t, top_k=min(TOP_K, n_sb)),
        grid=(b, t // tq),
        in_specs=[tile(D_ATTN), tile(LANES), seq(t // CMP_BLOCK, D_KV), seq(t // CMP_BLOCK, D_KV),
                  seq(D_KV, t), seq(D_KV, t), seq(D_KV, t), seq(D_KV, t)] + [_full(c.shape) for c in consts],
        out_specs=tile(D_ATTN),
        out_shape=jax.ShapeDtypeStruct((b, t, D_ATTN), F32),
        scratch_shapes=[pltpu.VMEM((t // tk, 2 * D_KV, tk), BF16)] * 4
        + [pltpu.VMEM((GROUP, N_KV * tq, LANES), F32), pltpu.VMEM((N_HEADS * tq, 2 * D_KV), F32)] * 2
        + [pltpu.VMEM((2 * D_KV, LANES), BF16), pltpu.VMEM((LANES, D_KV), BF16)],
        compiler_params=pltpu.CompilerParams(dimension_semantics=("arbitrary", "arbitrary"),
                                             vmem_limit_bytes=VMEM_LIMIT),
        name="attn_prompt",
    )(q, gate, kcmp, vcmp, ks, vs, kw, vw, *consts)


def _attn_sample_body(pt_ref, q_ref, gate_ref, ksn_ref, vsn_ref, kwn_ref, vwn_ref, kwin_ref, vwin_ref,
                      kcp_ref, vcp_ref, cvalid_ref, bias_c_ref, bias_s_ref, bias_w_ref, qrow_ref, blkc_ref, e_ref, *rest,
                      t, n_pages, page, past, w_buf, top_k, nbs):
    ks_pages = rest[:nbs * n_pages]
    vs_pages = rest[nbs * n_pages:2 * nbs * n_pages]
    (o_ref,) = rest[2 * nbs * n_pages:]
    b0 = pl.program_id(0) * nbs
    rows = N_HEADS * t
    rows2 = N_KV * t
    seqs = range(nbs)

    def stack(xs):
        return jnp.concatenate(xs, axis=0)

    def part(x, bb, n):
        return x[bb * n:(bb + 1) * n]

    def pad_rows(x):
        if x.shape[0] == LANES:
            return x
        return jnp.concatenate([x, jnp.zeros((LANES - x.shape[0], D_KV), F32)], axis=0)

    live = NEG_INF / 2
    qbd = [_make_qbd(q_ref[bb], t) for bb in seqs]
    gates = [gate_ref[bb] for bb in seqs]

    def gate_rows(c):
        return stack([_gate_rows(gates[bb], c, t) for bb in seqs])

    kc = [pad_rows(stack([kcp_ref[pt_ref[(b0 + bb) * n_pages + p] // 2] for p in range(n_pages)])) for bb in seqs]
    vc = [pad_rows(stack([vcp_ref[pt_ref[(b0 + bb) * n_pages + p] // 2] for p in range(n_pages)])) for bb in seqs]
    bias_c = bias_c_ref[...]
    s_c = stack([_mm_nt(qbd[bb], kc[bb]) for bb in seqs]) + bias_c
    cvalid = stack([jnp.broadcast_to(cvalid_ref[bb], (rows, LANES)) for bb in seqs])
    p_c = _softmax_rows(s_c, (cvalid > 0.5) & (bias_c > live))
    o_acc = gate_rows(0) * stack([_mm(part(p_c, bb, rows), vc[bb]) for bb in seqs])

    p4 = stack([_sum_groups(part(p_c, bb, rows), rows2) for bb in seqs])
    x1 = p4 + pltpu.roll(p4, LANES - 1, 1)
    imp = x1 + pltpu.roll(x1, LANES - 4, 1)
    n2 = nbs * rows2
    imp_t = jnp.concatenate([imp, jnp.zeros((LANES - n2, LANES), F32)], axis=0).T
    sel = _select_blocks(imp_t, qrow_ref[...], blkc_ref[...], float(past // SEL_BLOCK), top_k, 0).T[0:n2]
    sel4 = stack([part(sel, bb, rows2) for bb in seqs for _ in range(GROUP)])

    s_s = stack([jnp.concatenate([_mm(qbd[bb], ks_pages[bb * n_pages + p][...]) for p in range(n_pages)]
                                 + [_mm_nt(qbd[bb], pad_rows(ksn_ref[bb]))], axis=1) for bb in seqs])
    bias_s = bias_s_ref[...]
    picked = _mm(sel4, e_ref[...])
    p_s = _softmax_rows(s_s + bias_s, (picked > 0.5) & (bias_s > live))
    o_s = []
    for bb in seqs:
        pb = part(p_s, bb, rows)
        acc = _mm(pb[:, past:], pad_rows(vsn_ref[bb]))
        for p in range(n_pages):
            acc = acc + _mm_nt(pb[:, p * page:(p + 1) * page], vs_pages[bb * n_pages + p][...])
        o_s.append(acc)
    o_acc = o_acc + gate_rows(1) * stack(o_s)

    kwin = [kwin_ref[bb] for bb in seqs]
    vwin = [vwin_ref[bb] for bb in seqs]
    s_w = stack([jnp.concatenate([_mm(qbd[bb], kwin[bb]), _mm_nt(qbd[bb], pad_rows(kwn_ref[bb]))], axis=1)
                 for bb in seqs])
    bias_w = bias_w_ref[...]
    p_w = _softmax_rows(s_w + bias_w, bias_w > live)
    o_w = stack([_mm_nt(part(p_w, bb, rows)[:, :w_buf], vwin[bb])
                 + _mm(part(p_w, bb, rows)[:, w_buf:], pad_rows(vwn_ref[bb])) for bb in seqs])
    o_acc = o_acc + gate_rows(2) * o_w
    for bb in seqs:
        o_ref[bb] = _extract_heads(part(o_acc, bb, rows), t)


def _window_update_body(kwin_ref, vwin_ref, kn_ref, vn_ref, kwo_ref, vwo_ref, *, t, w_buf, nbs):
    is_new = _iota((D_KV, LANES), 1) >= LANES - t

    def shift_in(win, new):
        new_t = jnp.concatenate([jnp.zeros((LANES - t, D_KV), F32), new], axis=0).T
        rolled = pltpu.roll(win, w_buf - t, 1)
        last = jnp.where(is_new, new_t, rolled[:, w_buf - LANES:])
        return jnp.concatenate([rolled[:, :w_buf - LANES], last], axis=1)

    for bb in range(nbs):
        kwo_ref[bb] = shift_in(kwin_ref[bb], kn_ref[bb])
        vwo_ref[bb] = shift_in(vwin_ref[bb], vn_ref[bb])


def _window_update(kwin, vwin, kw_new, vw_new):
    nl, nb, _, w_buf = kwin.shape
    t = kw_new.shape[2]
    assert w_buf % LANES == 0 and t <= LANES
    nbs = next(c for c in range(min(WIN_STEP, nb), 0, -1) if nb % c == 0)
    win = pl.BlockSpec((None, nbs, D_KV, w_buf), lambda l, i: (l, i, 0, 0))
    new = pl.BlockSpec((None, nbs, t, D_KV), lambda l, i: (l, i, 0, 0))
    return pl.pallas_call(
        functools.partial(_window_update_body, t=t, w_buf=w_buf, nbs=nbs),
        grid=(nl, nb // nbs),
        in_specs=[win, win, new, new],
        out_specs=[win, win],
        out_shape=[jax.ShapeDtypeStruct(kwin.shape, F32)] * 2,
        compiler_params=pltpu.CompilerParams(dimension_semantics=("arbitrary", "arbitrary"),
                                             vmem_limit_bytes=VMEM_LIMIT),
        name="window_update",
    )(kwin, vwin, kw_new, vw_new)


def _page_map(l, row_stride, offset, i, pt):
    return (l, pt[i * row_stride + offset], 0, 0)


def _attn_sample(l, page_table, q, gate, ks_new, vs_new, kw_new, vw_new, kwin, vwin, kcp, vcp,
                 cache_ks, cache_vs):
    nb, t, _ = q.shape
    n_pages = page_table.shape[1]
    page = cache_ks.shape[3]
    past = n_pages * page
    w_buf = kwin.shape[3]
    n_sb = -(-(past + t) // SEL_BLOCK)
    assert t == SUBLANES and n_pages * 8 <= LANES and page == LANES and past % SEL_BLOCK == 0
    assert w_buf % LANES == 0
    slopes, tcol = _row_consts(t)
    lane = jnp.arange(LANES)
    pg, r = lane // 8, lane % 8
    in_range = pg < n_pages
    cend = jnp.where(in_range, (4 * pg + r % 4 + 1) * CMP_BLOCK - 1, 1 << 30).astype(jnp.int32)[None, :]
    par = page_table % 2
    par_l = jnp.take(par, jnp.minimum(pg, n_pages - 1), axis=1)
    cvalid = ((r[None, :] // 4 == par_l) & in_range[None, :]).astype(F32)[:, None, :]
    blk = jnp.where(in_range & (r % 2 == 0) & (r < 4), 2 * pg + r // 2, -1)
    blk = jnp.where(lane == 1, n_sb - 1, blk)
    blkf = blk.astype(F32)[None, :]
    e = (blkf.reshape(-1, 1) == (jnp.arange(past + LANES) // SEL_BLOCK).astype(F32)[None, :]).astype(BF16)
    nbs = next(c for c in range(min(DEC_STEP, nb), 0, -1) if nb % c == 0)
    qpos = past + tcol
    assert nbs * N_KV * t <= LANES
    qrow = jnp.pad(jnp.tile(qpos[:N_KV * t, 0], nbs), (0, LANES - nbs * N_KV * t), constant_values=past)[None, :]
    blkc = blkf.reshape(-1, 1)

    def bias(dist, ok):
        return jnp.tile(jnp.where(ok, -(slopes * dist.astype(F32)), NEG_INF), (nbs, 1))

    dist_c = qpos - cend
    dist_s = qpos - jnp.arange(past + LANES)[None, :]
    dist_w = qpos - (past - w_buf + jnp.arange(w_buf + LANES))[None, :]
    bias_c = bias(dist_c, dist_c >= 0)
    bias_s = bias(dist_s, dist_s >= 0)
    bias_w = bias(dist_w, (dist_w >= 0) & (dist_w <= WINDOW))
    tok = lambda n: pl.BlockSpec((nbs, t, n), lambda i, pt: (i, 0, 0))
    win = pl.BlockSpec((None, nbs, D_KV, w_buf), lambda i, pt: (l, i, 0, 0))
    res = pl.BlockSpec((None,) + kcp.shape[1:], lambda i, pt: (l, 0, 0, 0))
    const = lambda a: pl.BlockSpec(a.shape, lambda i, pt: (0,) * a.ndim)
    pages = [pl.BlockSpec((None, None, D_KV, page), functools.partial(_page_map, l, nbs * n_pages, bb * n_pages + p))
             for bb in range(nbs) for p in range(n_pages)]
    grid_spec = pltpu.PrefetchScalarGridSpec(
        num_scalar_prefetch=1,
        grid=(nb // nbs,),
        in_specs=[tok(D_ATTN), tok(LANES), tok(D_KV), tok(D_KV), tok(D_KV), tok(D_KV), win, win, res, res,
                  pl.BlockSpec((nbs, 1, LANES), lambda i, pt: (i, 0, 0)),
                  const(bias_c), const(bias_s), const(bias_w), const(qrow), const(blkc), const(e)] + pages + pages,
        out_specs=[tok(D_ATTN)],
    )
    return pl.pallas_call(
        functools.partial(_attn_sample_body, t=t, n_pages=n_pages, page=page, past=past, w_buf=w_buf,
                          top_k=min(TOP_K, n_sb), nbs=nbs),
        grid_spec=grid_spec,
        out_shape=[jax.ShapeDtypeStruct((nb, t, D_ATTN), F32)],
        compiler_params=pltpu.CompilerParams(dimension_semantics=("arbitrary",), vmem_limit_bytes=VMEM_LIMIT),
        name="attn_sample",
    )(page_table.reshape(-1), q, gate, ks_new, vs_new, kw_new, vw_new, kwin, vwin, kcp, vcp, cvalid,
      bias_c, bias_s, bias_w, qrow, blkc, e, *([cache_ks] * (nbs * n_pages)), *([cache_vs] * (nbs * n_pages)))


def _back_head(x, o, pab, wg2_ref, woa_ref, wmix_ref, ln1g_ref, ln1b_ref):
    y_c = _mm(o, woa_ref[...])
    g2 = jax.nn.sigmoid(_mm(x, wg2_ref[...]))
    mixed = pab + g2 * y_c
    return _layer_norm(ALPHA * x + _mm(mixed, wmix_ref[...]), ln1g_ref[...], ln1b_ref[...])


def _back_prompt_body(x_ref, o_ref, pab_ref, wg2_ref, woa_ref, wmix_ref, ln1g_ref, ln1b_ref,
                      wup_ref, fcw_ref, fcb_ref, wdn_ref, ln2g_ref, ln2b_ref,
                      y_ref, fst_ref, ext_ref, *, tt):
    j = pl.program_id(1)

    @pl.when(j == 0)
    def _():
        ext_ref[0:8, :] = jnp.zeros((8, D_FF), F32)

    h = _back_head(x_ref[0], o_ref[0], pab_ref[0], wg2_ref, woa_ref, wmix_ref, ln1g_ref, ln1b_ref)
    hb = h.astype(BF16)
    cf = D_FF // FF_CHUNKS
    f = jnp.zeros((tt, D_MODEL), F32)
    for c in range(FF_CHUNKS):
        cs = slice(c * cf, (c + 1) * cf)
        u = _mm(hb, wup_ref[:, c * cf:(c + 1) * cf])
        gt = _mm(hb, wup_ref[:, D_FF + c * cf:D_FF + (c + 1) * cf])
        ext_ref[8:8 + tt, cs] = u
        conv = fcb_ref[:, cs] + fcw_ref[0:1, cs] * ext_ref[6:6 + tt, cs]
        conv = conv + fcw_ref[1:2, cs] * ext_ref[7:7 + tt, cs]
        conv = conv + fcw_ref[2:3, cs] * u
        f = f + _mm(jax.nn.gelu(conv) * gt, wdn_ref[c * cf:(c + 1) * cf, :])
    y_ref[0] = _layer_norm(ALPHA * h + f, ln2g_ref[...], ln2b_ref[...])
    last = ext_ref[tt:tt + 8, :]
    fst_ref[0] = last
    ext_ref[0:8, :] = last


def _back_sample_body(x_ref, o_ref, pab_ref, hf_ref, wg2_ref, woa_ref, wmix_ref, ln1g_ref, ln1b_ref,
                      wup_ref, fcw_ref, fcb_ref, wdn_ref, ln2g_ref, ln2b_ref,
                      y_ref, fst_ref, ext_ref, *, ns, t):
    h = _back_head(x_ref[...], o_ref[...], pab_ref[...], wg2_ref, woa_ref, wmix_ref, ln1g_ref, ln1b_ref)
    hb = h.astype(BF16)
    cf = D_FF // FF_CHUNKS
    ext_ref[:, 6:8, :] = hf_ref[...]
    f = jnp.zeros((ns * t, D_MODEL), F32)
    for c in range(FF_CHUNKS):
        cs = slice(c * cf, (c + 1) * cf)
        u = _mm(hb, wup_ref[:, c * cf:(c + 1) * cf])
        gt = _mm(hb, wup_ref[:, D_FF + c * cf:D_FF + (c + 1) * cf])
        ext_ref[:, 8:8 + t, cs] = u.reshape(ns, t, cf)
        fcw = fcw_ref[:, cs]
        conv = fcb_ref[:, cs] + fcw[0:1] * ext_ref[:, 6:6 + t, cs]
        conv = conv + fcw[1:2] * ext_ref[:, 7:7 + t, cs]
        conv = conv + fcw[2:3] * ext_ref[:, 8:8 + t, cs]
        f = f + _mm(jax.nn.gelu(conv).reshape(ns * t, cf) * gt, wdn_ref[c * cf:(c + 1) * cf, :])
    y_ref[...] = _layer_norm(ALPHA * h + f, ln2g_ref[...], ln2b_ref[...])
    fst_ref[...] = ext_ref[:, t:t + 8, :]


def _back_prompt(x, o, pab, wts):
    b, t, _ = x.shape
    tt = min(FRONT_TILE, t)
    tok = lambda n: pl.BlockSpec((1, tt, n), lambda i, j: (i, j, 0))
    return pl.pallas_call(
        functools.partial(_back_prompt_body, tt=tt),
        grid=(b, t // tt),
        in_specs=[tok(D_MODEL), tok(D_ATTN), tok(D_MODEL)] + [_full(w.shape) for w in wts],
        out_specs=[tok(D_MODEL), pl.BlockSpec((1, 8, D_FF), lambda i, j: (i, 0, 0))],
        out_shape=[jax.ShapeDtypeStruct((b, t, D_MODEL), F32), jax.ShapeDtypeStruct((b, 8, D_FF), F32)],
        scratch_shapes=[pltpu.VMEM((8 + tt, D_FF), F32)],
        compiler_params=pltpu.CompilerParams(dimension_semantics=("arbitrary", "arbitrary"),
                                             vmem_limit_bytes=VMEM_LIMIT),
        name="back_prompt",
    )(x, o, pab, *wts)


def _back_sample(x, o, pab, hist_f, wts, t):
    n = x.shape[0]
    nb = n // t
    ns = min(FRONT_TILE // t, nb)
    rows = ns * t
    tok = lambda c: pl.BlockSpec((rows, c), lambda i: (i, 0))
    return pl.pallas_call(
        functools.partial(_back_sample_body, ns=ns, t=t),
        grid=(nb // ns,),
        in_specs=[tok(D_MODEL), tok(D_ATTN), tok(D_MODEL),
                  pl.BlockSpec((ns, CONV_W - 1, D_FF), lambda i: (i, 0, 0))] + [_full(w.shape) for w in wts],
        out_specs=[tok(D_MODEL), pl.BlockSpec((ns, 8, D_FF), lambda i: (i, 0, 0))],
        out_shape=[jax.ShapeDtypeStruct((n, D_MODEL), F32), jax.ShapeDtypeStruct((nb, 8, D_FF), F32)],
        scratch_shapes=[pltpu.VMEM((ns, 8 + t, D_FF), F32)],
        compiler_params=pltpu.CompilerParams(dimension_semantics=("arbitrary",), vmem_limit_bytes=VMEM_LIMIT),
        name="back_sample",
    )(x, o, pab, hist_f, *wts)


def _block_diag2(w):
    z = jnp.zeros_like(w)
    return jnp.concatenate([jnp.concatenate([w, z], axis=-1), jnp.concatenate([z, w], axis=-1)], axis=-2)


def kernel(x_prompt, x_sample, cache_k_cmp, cache_v_cmp, cache_k_sel, cache_v_sel, state_k_win, state_v_win, state_conv, state_pool, state_ffn_conv, page_table, w_in, conv_w, conv_b, w_out_conv, pool_w, pool_scale, w_cmp_k, w_cmp_v, w_out_attn, w_mix_out, ln1_g, ln1_b, w_ffn_up, ffn_conv_w, ffn_conv_b, w_ffn_down, ln2_g, ln2_b):
    depth = w_in.shape[0]
    bp, seq, _ = x_prompt.shape
    nb, t, _ = x_sample.shape
    n_phys, page = cache_k_cmp.shape[1], cache_k_cmp.shape[2]
    past = page_table.shape[1] * page
    w_buf = state_k_win.shape[2]

    w_front = jnp.concatenate(
        [w_in[:, :, :IN_GATES], jnp.zeros((depth, D_MODEL, C_G01 - IN_GATES), F32),
         w_in[:, :, IN_GATES:IN_GATES + 2 * D_MODEL]], axis=2).astype(BF16)
    w_g2 = w_in[:, :, IN_GATES + 2 * D_MODEL:].astype(BF16)
    ng = pool_w.shape[1]
    pool_bd = jnp.zeros((depth, D_POOL, D_MODEL), F32)
    for g in range(ng):
        pool_bd = pool_bd.at[:, g * POOL_GROUP:(g + 1) * POOL_GROUP,
                             g * (D_MODEL // ng):(g + 1) * (D_MODEL // ng)].set(pool_w[:, g])
    pool_bd = pool_bd.astype(BF16)
    wck = _block_diag2(w_cmp_k).astype(BF16)
    wcv = _block_diag2(w_cmp_v).astype(BF16)
    w_oc, w_oa, w_mix = w_out_conv.astype(BF16), w_out_attn.astype(BF16), w_mix_out.astype(BF16)
    w_up, w_dn = w_ffn_up.astype(BF16), w_ffn_down.astype(BF16)
    row = lambda a: a[:, None, :]

    keys_minor = lambda a: jnp.transpose(a, (0, 1, 3, 4, 2)).reshape(a.shape[0], a.shape[1], D_KV, a.shape[2])
    tokens_major = lambda a: jnp.transpose(a.reshape(a.shape[0], N_KV, HEAD_DIM, a.shape[2]), (0, 3, 1, 2))
    ck_s, cv_s = keys_minor(cache_k_sel), keys_minor(cache_v_sel)
    kwin_all, vwin_all = keys_minor(state_k_win), keys_minor(state_v_win)
    kcp, vcp = _compress(keys_minor(cache_k_cmp), keys_minor(cache_v_cmp), wck, wcv)
    kcp = kcp.reshape(depth, n_phys // 2, 8, D_KV)
    vcp = vcp.reshape(depth, n_phys // 2, 8, D_KV)

    y_p = x_prompt
    y_s = x_sample.reshape(nb * t, D_MODEL)
    st_p, st_s = [], []
    for l in range(depth):
        fw = (w_front[l], conv_w[l], row(conv_b)[l], w_oc[l], pool_bd[l], row(pool_scale)[l])
        bw = (w_g2[l], w_oa[l], w_mix[l], row(ln1_g)[l], row(ln1_b)[l], w_up[l], ffn_conv_w[l],
              row(ffn_conv_b)[l], w_dn[l], row(ln2_g)[l], row(ln2_b)[l])

        pab, q, kc, vc, ks, vs, kw, vw, gate, cst, pst = _front_prompt(y_p, fw)
        kcmp, vcmp = _compress(kc[None], vc[None], wck[l:l + 1], wcv[l:l + 1])
        o = _attn_prompt(q, gate, kcmp.reshape(bp, seq // CMP_BLOCK, D_KV),
                         vcmp.reshape(bp, seq // CMP_BLOCK, D_KV), ks, vs, kw, vw)
        y_p, fst = _back_prompt(y_p, o, pab, bw)
        st_p.append(tuple(tokens_major(a) for a in (kc, vc, ks, vs, kw[:, :, seq - w_buf:], vw[:, :, seq - w_buf:]))
                    + (cst[:, 8 - (CONV_W - 1):], pst[:, 16 - POOL_HIST:], fst[:, 8 - (CONV_W - 1):]))

        pab, q, kc, vc, ks, vs, kw, vw, gate, cst, pst = _front_sample(
            y_s.reshape(nb, t, D_MODEL), state_conv[l], state_pool[l], fw, past)
        r3 = lambda a: a.reshape(nb, t, a.shape[-1])
        (o,) = _attn_sample(l, page_table, r3(q), r3(gate), r3(ks), r3(vs), r3(kw), r3(vw),
                            kwin_all, vwin_all, kcp, vcp, ck_s, cv_s)
        y_s, fst = _back_sample(y_s, o.reshape(nb * t, D_ATTN), pab, state_ffn_conv[l], bw, t)
        st_s.append((kc, vc, ks, vs, kw, vw,
                     cst[:, 8 - (CONV_W - 1):], pst[:, 16 - POOL_HIST:], fst[:, 8 - (CONV_W - 1):]))

    kc_p, vc_p, ks_p, vs_p, kw_p, vw_p, conv_p, pool_p, ffn_p = [jnp.stack(a) for a in zip(*st_p)]
    kc_s, vc_s, ks_s, vs_s, kwn_s, vwn_s, conv_s, pool_s, ffn_s = [jnp.stack(a) for a in zip(*st_s)]
    kc_s, vc_s, ks_s, vs_s = [a.reshape(depth, nb, t, N_KV, HEAD_DIM) for a in (kc_s, vc_s, ks_s, vs_s)]
    kw_s, vw_s = _window_update(kwin_all, vwin_all, kwn_s.reshape(depth, nb, t, D_KV),
                                vwn_s.reshape(depth, nb, t, D_KV))
    kw_s, vw_s = [jnp.transpose(a.reshape(depth, nb, N_KV, HEAD_DIM, w_buf), (0, 1, 4, 2, 3)) for a in (kw_s, vw_s)]
    return (y_p, y_s.reshape(nb, t, D_MODEL), kc_p, kc_s, vc_p, vc_s, ks_p, ks_s, vs_p, vs_s,
            kw_p, kw_s, vw_p, vw_s, conv_p, conv_s, pool_p, pool_s, ffn_p, ffn_s)
```
